```python
import math
import jax, jax.numpy as jnp
from jax import lax
import numpy as np


D_MODEL = 4096
BATCH = 2
SEQ = 8192
DEPTH = 4

HEAD_DIM = 128
MIX_W = D_MODEL // 4
N_BRANCH = 4
POOL_WINDOWS = (2, 4, 8, 16)
POOL_GROUPS = len(POOL_WINDOWS)
POOL_GW = MIX_W // POOL_GROUPS
NSA_HEADS = MIX_W // HEAD_DIM
NSA_KV = 2
NSA_HPG = NSA_HEADS // NSA_KV
CMP_LEN = 32
CMP_STRIDE = 16
SLC_LEN = 64
SLC_TOPK = 16
WINDOW = 512
Q_BLOCK = 128
FORCE_BONUS = 1e4
NEG = -1e9
GMLP_CHUNK = 128
GMLP_GROUPS = MIX_W // HEAD_DIM
GMLP_GW = MIX_W // GMLP_GROUPS
CONV_W = 31
GATE_RANK = 512
REL_BUCKETS = 32
REL_MAX_DIST = 128
D_FF = 2 * D_MODEL
N_EXPERTS = 8
TOP_K = 2
D_FF_EXPERT = (7 * D_MODEL) // 16
EPS = 1e-6

NSA_KV_COLS = 3 * 2 * NSA_KV * HEAD_DIM
NSA_GATE_COLS = NSA_HEADS * 3
IN_SPLITS = (MIX_W, MIX_W, NSA_KV_COLS, NSA_GATE_COLS, 2 * MIX_W, 2 * MIX_W, GATE_RANK)
IN_COLS = sum(IN_SPLITS)
IN_OFFSETS = tuple(int(v) for v in np.cumsum(IN_SPLITS)[:-1])

kernel_name = 'hybrid_gated_pool_nsa_gmlp_conv_moe'


def rmsnorm(x, g):
    xf = x.astype(jnp.float32)
    r = lax.rsqrt(jnp.mean(xf * xf, axis=-1, keepdims=True) + EPS)
    return (xf * r).astype(x.dtype) * g


def layernorm(x, g, b):
    xf = x.astype(jnp.float32)
    mu = jnp.mean(xf, axis=-1, keepdims=True)
    var = jnp.mean(jnp.square(xf - mu), axis=-1, keepdims=True)
    return ((xf - mu) * lax.rsqrt(var + EPS)).astype(x.dtype) * g + b


def t5_bucket(dist):
    n = jnp.maximum(dist, 0)
    max_exact = REL_BUCKETS // 2
    nf = jnp.maximum(n, 1).astype(jnp.float32)
    large = max_exact + (jnp.log(nf / max_exact) / math.log(REL_MAX_DIST / max_exact)
                         * (REL_BUCKETS - max_exact)).astype(jnp.int32)
    large = jnp.minimum(large, REL_BUCKETS - 1)
    return jnp.where(n < max_exact, n, large)


def masked_softmax(logits, mask):
    l = jnp.where(mask, logits.astype(jnp.float32), -1e30)
    m = jnp.max(l, axis=-1, keepdims=True)
    p = jnp.where(mask, jnp.exp(l - m), 0.0)
    return p / jnp.maximum(jnp.sum(p, axis=-1, keepdims=True), 1e-30)


def pool_mixer(a, w_pool, scale):
    S = a.shape[1]
    af = a.astype(jnp.float32)
    cs = jnp.cumsum(af, axis=1)
    cnt = jnp.arange(1, S + 1, dtype=jnp.float32)[None, :, None]
    outs = []
    for gi, w in enumerate(POOL_WINDOWS):
        sl = slice(gi * POOL_GW, (gi + 1) * POOL_GW)
        c = cs[..., sl]
        shifted = jnp.pad(c, ((0, 0), (w, 0), (0, 0)))[:, :S]
        diff = ((c - shifted) / jnp.minimum(cnt, w) - af[..., sl]).astype(a.dtype)
        outs.append(diff @ w_pool[gi])
    return jnp.concatenate(outs, axis=-1) * scale


def head_rmsnorm(x, g):
    return rmsnorm(x, g)


def nsa_mixer(q_in, kv_in, gate_in, q_g, k_g, rel_bias,
              pos_k, w1_k, w2_k, pos_v, w1_v, w2_v):
    B, S, _ = q_in.shape
    H, G, R, d = NSA_HEADS, NSA_KV, NSA_HPG, HEAD_DIM
    q = head_rmsnorm(q_in.reshape(B, S, H, d), q_g) * (d ** -0.5)
    k_c, v_c, k_s, v_s, k_w, v_w = [t.reshape(B, S, G, d) for t in jnp.split(kv_in, 6, axis=-1)]

    n_cmp = (S - CMP_LEN) // CMP_STRIDE + 1
    cmp_idx = jnp.arange(n_cmp)[:, None] * CMP_STRIDE + jnp.arange(CMP_LEN)[None, :]

    def compress(t, pos, w1, w2):
        blk = t[:, cmp_idx] + pos[None, None, :, None, :]
        hdn = jax.nn.gelu(jnp.einsum('bnlgd,lde->bnge', blk, w1))
        return jnp.einsum('bnge,ef->bngf', hdn, w2)

    k_cmp = head_rmsnorm(compress(k_c, pos_k, w1_k, w2_k), k_g)
    v_cmp = compress(v_c, pos_v, w1_v, w2_v)
    cmp_end = jnp.arange(n_cmp) * CMP_STRIDE + CMP_LEN - 1

    n_slc = S // SLC_LEN
    n_sel = min(SLC_TOPK, n_slc)
    k_blk = head_rmsnorm(k_s, k_g).reshape(B, n_slc, SLC_LEN, G, d).transpose(0, 3, 1, 2, 4)
    v_blk = v_s.reshape(B, n_slc, SLC_LEN, G, d).transpose(0, 3, 1, 2, 4)
    c0 = jnp.arange(n_cmp)[:, None] * CMP_STRIDE
    s0b = jnp.arange(n_slc)[None, :] * SLC_LEN
    overlap = ((c0 < s0b + SLC_LEN) & (c0 + CMP_LEN > s0b)).astype(jnp.float32)
    blk_start = jnp.arange(n_slc) * SLC_LEN
    jj = jnp.arange(n_slc)

    win_len = WINDOW + Q_BLOCK
    k_wp = jnp.pad(head_rmsnorm(k_w, k_g), ((0, 0), (WINDOW, 0), (0, 0), (0, 0)))
    v_wp = jnp.pad(v_w, ((0, 0), (WINDOW, 0), (0, 0), (0, 0)))
    rel_w = jnp.arange(Q_BLOCK)[:, None] + WINDOW - jnp.arange(win_len)[None, :]
    bias_w = rel_bias[t5_bucket(rel_w)].transpose(2, 0, 1).reshape(G, R, Q_BLOCK, win_len)

    table_g = rel_bias.reshape(REL_BUCKETS, G, R).transpose(1, 0, 2)
    gates = jax.nn.sigmoid(gate_in.astype(jnp.float32)).reshape(B, S, H, 3)
    gather = jax.vmap(jax.vmap(lambda kb, ix: kb[ix]))
    g_ar = jnp.arange(G)[None, :, None, None]

    def block(qb):
        s0 = qb * Q_BLOCK
        t = s0 + jnp.arange(Q_BLOCK)
        qq = lax.dynamic_slice_in_dim(q, s0, Q_BLOCK, axis=1).reshape(B, Q_BLOCK, G, R, d)
        dist_c = t[:, None] - cmp_end[None, :]
        bias_c = rel_bias[t5_bucket(dist_c)].transpose(2, 0, 1).reshape(G, R, Q_BLOCK, n_cmp)
        lc = jnp.einsum('bqgrd,bngd->bgrqn', qq, k_cmp) + bias_c
        pc = masked_softmax(lc, dist_c >= 0)
        o_c = jnp.einsum('bgrqn,bngd->bqgrd', pc, v_cmp)
        imp = jnp.einsum('bgrqn,nj->bgqj', pc, overlap)
        cur = t // SLC_LEN
        forced = (jj[None, :] == 0) | (jj[None, :] == cur[:, None]) | (jj[None, :] == cur[:, None] - 1)
        causal_blk = blk_start[None, :] <= t[:, None]
        score = jnp.where(causal_blk, imp + FORCE_BONUS * forced.astype(jnp.float32), NEG)
        _, idx = lax.top_k(score, n_sel)
        ks = gather(k_blk, idx).reshape(B, G, Q_BLOCK, n_sel * SLC_LEN, d)
        vs = gather(v_blk, idx).reshape(B, G, Q_BLOCK, n_sel * SLC_LEN, d)
        pos_s = (idx[..., None] * SLC_LEN + jnp.arange(SLC_LEN)).reshape(B, G, Q_BLOCK, n_sel * SLC_LEN)
        dist_s = t[None, None, :, None] - pos_s
        bias_s = jnp.moveaxis(table_g[g_ar, t5_bucket(dist_s)], -1, 2)
        ls = jnp.einsum('bqgrd,bgqkd->bgrqk', qq, ks) + bias_s
        ps = masked_softmax(ls, (dist_s >= 0)[:, :, None])
        o_s = jnp.einsum('bgrqk,bgqkd->bqgrd', ps, vs)
        kw = lax.dynamic_slice_in_dim(k_wp, s0, win_len, axis=1)
        vw = lax.dynamic_slice_in_dim(v_wp, s0, win_len, axis=1)
        pos_w = s0 - WINDOW + jnp.arange(win_len)
        dist_w = t[:, None] - pos_w[None, :]
        mask_w = (pos_w[None, :] >= 0) & (dist_w >= 0) & (dist_w < WINDOW)
        lw = jnp.einsum('bqgrd,bkgd->bgrqk', qq, kw) + bias_w
        pw = masked_softmax(lw, mask_w)
        o_w = jnp.einsum('bgrqk,bkgd->bqgrd', pw, vw)
        gb = lax.dynamic_slice_in_dim(gates, s0, Q_BLOCK, axis=1).reshape(B, Q_BLOCK, G, R, 3)
        o = gb[..., 0:1] * o_c + gb[..., 1:2] * o_s + gb[..., 2:3] * o_w
        return o.reshape(B, Q_BLOCK, H * d)

    out = lax.map(block, jnp.arange(S // Q_BLOCK))
    return out.transpose(1, 0, 2, 3).reshape(B, S, H * d)


def gmlp_mixer(z, ln_g, ln_b, w_s, b_s):
    B, S, _ = z.shape
    z = jax.nn.gelu(z)
    u, v = jnp.split(z, 2, axis=-1)
    v = layernorm(v, ln_g, ln_b)
    v = v.reshape(B, S // GMLP_CHUNK, GMLP_CHUNK, GMLP_GROUPS, GMLP_GW)
    tri = jnp.tril(jnp.ones((GMLP_CHUNK, GMLP_CHUNK), dtype=w_s.dtype))
    s = jnp.einsum('hij,bcjhd->bcihd', w_s * tri, v) + b_s.T[None, None, :, :, None]
    return u * s.reshape(B, S, MIX_W)


def conv_mixer(z, w, b, ln_g, ln_b):
    a, gte = jnp.split(z, 2, axis=-1)
    h = a * jax.nn.sigmoid(gte)
    h = lax.conv_general_dilated(h, w[:, None, :], window_strides=(1,),
                                 padding=((CONV_W - 1, 0),),
                                 dimension_numbers=('NWC', 'WIO', 'NWC'),
                                 feature_group_count=MIX_W) + b
    return jax.nn.silu(layernorm(h, ln_g, ln_b))


def swiglu(h, wg, wu, wd):
    return (jax.nn.silu(h @ wg) * (h @ wu)) @ wd


def moe_swiglu(h, router, router_b, wg, wu, wd):
    logits = (h @ router).astype(jnp.float32) + router_b
    vals, idx = lax.top_k(logits, TOP_K)
    w = jax.nn.softmax(vals, axis=-1)
    gate = jnp.sum(jax.nn.one_hot(idx, N_EXPERTS, dtype=jnp.float32) * w[..., None], axis=-2)
    out = jnp.zeros_like(h)
    for e in range(N_EXPERTS):
        out = out + gate[..., e:e + 1].astype(h.dtype) * swiglu(h, wg[e], wu[e], wd[e])
    return out


def setup_inputs(seed: int = 0) -> dict:
    key = jax.random.key(seed)
    ks = iter(jax.random.split(key, 40))

    def nrm(shape, scale):
        return jax.random.normal(next(ks), shape, jnp.float32) * scale

    L = DEPTH
    LD = (DEPTH + 1) // 2
    LM = DEPTH // 2
    d = HEAD_DIM
    return {
        'x': nrm((BATCH, SEQ, D_MODEL), 1.0),
        'rel_bias': nrm((REL_BUCKETS, NSA_HEADS), 0.5),
        'norm_mix_g': 1.0 + nrm((L, D_MODEL), 0.05),
        'norm_ffn_g': 1.0 + nrm((L, D_MODEL), 0.05),
        'w_in': nrm((L, D_MODEL, IN_COLS), D_MODEL ** -0.5),
        'pool_w': nrm((L, POOL_GROUPS, POOL_GW, POOL_GW), POOL_GW ** -0.5),
        'pool_scale': 1.0 + nrm((L, MIX_W), 0.1),
        'q_norm_g': 1.0 + nrm((L, d), 0.05),
        'k_norm_g': 1.0 + nrm((L, d), 0.05),
        'cmp_pos_k': nrm((L, CMP_LEN, d), 0.1),
        'cmp_w1_k': nrm((L, CMP_LEN, d, d), (CMP_LEN * d) ** -0.5),
        'cmp_w2_k': nrm((L, d, d), d ** -0.5),
        'cmp_pos_v': nrm((L, CMP_LEN, d), 0.1),
        'cmp_w1_v': nrm((L, CMP_LEN, d, d), (CMP_LEN * d) ** -0.5),
        'cmp_w2_v': nrm((L, d, d), d ** -0.5),
        'gmlp_ln_g': 1.0 + nrm((L, MIX_W), 0.05),
        'gmlp_ln_b': nrm((L, MIX_W), 0.02),
        'gmlp_ws': nrm((L, GMLP_GROUPS, GMLP_CHUNK, GMLP_CHUNK), 0.5 * GMLP_CHUNK ** -0.5),
        'gmlp_bs': 1.0 + nrm((L, GMLP_GROUPS, GMLP_CHUNK), 0.1),
        'conv_w': nrm((L, CONV_W, MIX_W), CONV_W ** -0.5),
        'conv_b': nrm((L, MIX_W), 0.02),
        'conv_ln_g': 1.0 + nrm((L, MIX_W), 0.05),
        'conv_ln_b': nrm((L, MIX_W), 0.02),
        'w_branch': nrm((L, N_BRANCH, MIX_W, D_MODEL), MIX_W ** -0.5),
        'w_gate_up': nrm((L, N_BRANCH, GATE_RANK, D_MODEL), GATE_RANK ** -0.5),
        'w_out': nrm((L, D_MODEL, D_MODEL), D_MODEL ** -0.5),
        'ffn_w_gate': nrm((LD, D_MODEL, D_FF), D_MODEL ** -0.5),
        'ffn_w_up': nrm((LD, D_MODEL, D_FF), D_MODEL ** -0.5),
        'ffn_w_down': nrm((LD, D_FF, D_MODEL), D_FF ** -0.5),
        'moe_router': nrm((LM, D_MODEL, N_EXPERTS), D_MODEL ** -0.5),
        'moe_router_b': nrm((LM, N_EXPERTS), 0.01),
        'moe_w_gate': nrm((LM, N_EXPERTS, D_MODEL, D_FF_EXPERT), D_MODEL ** -0.5),
        'moe_w_up': nrm((LM, N_EXPERTS, D_MODEL, D_FF_EXPERT), D_MODEL ** -0.5),
        'moe_w_down': nrm((LM, N_EXPERTS, D_FF_EXPERT, D_MODEL), D_FF_EXPERT ** -0.5),
    }


def reference(x, rel_bias, norm_mix_g, norm_ffn_g, w_in, pool_w, pool_scale,
              q_norm_g, k_norm_g, cmp_pos_k, cmp_w1_k, cmp_w2_k,
              cmp_pos_v, cmp_w1_v, cmp_w2_v, gmlp_ln_g, gmlp_ln_b, gmlp_ws, gmlp_bs,
              conv_w, conv_b, conv_ln_g, conv_ln_b, w_branch, w_gate_up, w_out,
              ffn_w_gate, ffn_w_up, ffn_w_down, moe_router, moe_router_b,
              moe_w_gate, moe_w_up, moe_w_down):
    for l in range(DEPTH):
        h = rmsnorm(x, norm_mix_g[l])
        p = h @ w_in[l]
        a_in, q_in, kv_in, ng_in, c_in, d_in, gate_in = jnp.split(p, list(IN_OFFSETS), axis=-1)
        o_a = pool_mixer(a_in, pool_w[l], pool_scale[l])
        o_b = nsa_mixer(q_in, kv_in, ng_in, q_norm_g[l], k_norm_g[l], rel_bias,
                        cmp_pos_k[l], cmp_w1_k[l], cmp_w2_k[l],
                        cmp_pos_v[l], cmp_w1_v[l], cmp_w2_v[l])
        o_c = gmlp_mixer(c_in, gmlp_ln_g[l], gmlp_ln_b[l], gmlp_ws[l], gmlp_bs[l])
        o_d = conv_mixer(d_in, conv_w[l], conv_b[l], conv_ln_g[l], conv_ln_b[l])
        mix = jnp.zeros_like(x)
        for bi, ob in enumerate((o_a, o_b, o_c, o_d)):
            gate = jax.nn.sigmoid(gate_in @ w_gate_up[l, bi])
            mix = mix + gate * (ob @ w_branch[l, bi])
        x = x + mix @ w_out[l]
        h = rmsnorm(x, norm_ffn_g[l])
        if l % 2 == 0:
            i = l // 2
            x = x + swiglu(h, ffn_w_gate[i], ffn_w_up[i], ffn_w_down[i])
        else:
            i = l // 2
            x = x + moe_swiglu(h, moe_router[i], moe_router_b[i],
                               moe_w_gate[i], moe_w_up[i], moe_w_down[i])
    return x
```

```python
import functools
import math

import jax
import jax.numpy as jnp
import numpy as np
from jax import lax
from jax.experimental import pallas as pl
from jax.experimental.pallas import tpu as pltpu

F32 = jnp.float32
BF16 = jnp.bfloat16

VMEM_LIMIT_BYTES = 56 * 1024 * 1024
LANES = 128

EPS = 1e-6
HEAD_DIM = 128
MIX_W = 1024
POOL_WINDOWS = (2, 4, 8, 16)
POOL_GW = MIX_W // len(POOL_WINDOWS)
NSA_HEADS = 8
NSA_KV = 2
NSA_HPG = NSA_HEADS // NSA_KV
CMP_LEN = 32
CMP_STRIDE = 16
SLC_LEN = 64
SLC_TOPK = 16
WINDOW = 512
Q_BLOCK = 128
FORCE_BONUS = 1e4
NEG = -1e9
MASKED = -1e30
GMLP_CHUNK = 128
CONV_W = 31
GATE_RANK = 512
REL_BUCKETS = 32
REL_MAX_DIST = 128
N_EXPERTS = 8
TOP_K = 2

C0 = 0
D0 = 2048
A0 = 4096
Q0 = 5120
KV0 = 6144
G0 = 7680
NG0 = 8192
PCOLS = 8704
KC_PAD = 128


def _cparams(*sem):
    return pltpu.CompilerParams(dimension_semantics=sem, vmem_limit_bytes=VMEM_LIMIT_BYTES)


def _dot(a, b):
    return jnp.dot(a, b, preferred_element_type=F32)


def _dot_nt(a, b):
    return lax.dot_general(a, b, (((1,), (1,)), ((), ())), preferred_element_type=F32)


def _gelu_tanh(x):
    return 0.5 * x * (1.0 + jnp.tanh(math.sqrt(2.0 / math.pi) * (x + 0.044715 * (x * x * x))))


def _sigmoid(x):
    return 1.0 / (1.0 + jnp.exp(-x))


def _rmsnorm_kernel(x_ref, g_ref, o_ref):
    xf = x_ref[...]
    r = lax.rsqrt(jnp.mean(xf * xf, axis=-1, keepdims=True) + EPS)
    o_ref[...] = ((xf * r) * g_ref[...]).astype(o_ref.dtype)


def rmsnorm_call(x, g, tr=256):
    t, d = x.shape
    return pl.pallas_call(
        _rmsnorm_kernel,
        grid=(t // tr,),
        in_specs=[pl.BlockSpec((tr, d), lambda i: (i, 0)), pl.BlockSpec((1, d), lambda i: (0, 0))],
        out_specs=pl.BlockSpec((tr, d), lambda i: (i, 0)),
        out_shape=jax.ShapeDtypeStruct((t, d), BF16),
        compiler_params=_cparams("parallel"),
        name="rmsnorm",
    )(x, g)


def _mm_kernel(*refs, nk, has_res):
    if has_res:
        a_ref, w_ref, r_ref, o_ref = refs[:4]
        scr = refs[4:]
    else:
        a_ref, w_ref, o_ref = refs[:3]
        r_ref = None
        scr = refs[3:]
    part = _dot(a_ref[...], w_ref[...])

    def finish(acc):
        if has_res:
            acc = acc + r_ref[...]
        o_ref[...] = acc.astype(o_ref.dtype)

    if nk == 1:
        finish(part)
    else:
        acc_ref = scr[0]
        k = pl.program_id(2)

        @pl.when(k == 0)
        def _():
            acc_ref[...] = part

        @pl.when(k > 0)
        def _():
            acc_ref[...] += part

        @pl.when(k == nk - 1)
        def _():
            finish(acc_ref[...])


def matmul_call(a, w, res=None, out_dtype=BF16, tm=1024, tn=512, tk=4096, name="matmul"):
    m, kdim = a.shape
    n = w.shape[1]
    tk = min(tk, kdim)
    nk = kdim // tk
    assert m % tm == 0 and n % tn == 0 and kdim % tk == 0
    in_specs = [pl.BlockSpec((tm, tk), lambda i, j, k: (i, k)), pl.BlockSpec((tk, tn), lambda i, j, k: (k, j))]
    args = [a, w]
    if res is not None:
        in_specs.append(pl.BlockSpec((tm, tn), lambda i, j, k: (i, j)))
        args.append(res)
    return pl.pallas_call(
        functools.partial(_mm_kernel, nk=nk, has_res=res is not None),
        grid=(m // tm, n // tn, nk),
        in_specs=in_specs,
        out_specs=pl.BlockSpec((tm, tn), lambda i, j, k: (i, j)),
        out_shape=jax.ShapeDtypeStruct((m, n), out_dtype),
        scratch_shapes=[pltpu.VMEM((tm, tn), F32)] if nk > 1 else [],
        compiler_params=_cparams("parallel", "parallel", "arbitrary"),
        name=name,
    )(*args)


def _swiglu_kernel(*refs, has_gate):
    if has_gate:
        h_ref, wg_ref, wu_ref, gate_ref, o_ref = refs
    else:
        h_ref, wg_ref, wu_ref, o_ref = refs
    h = h_ref[...]
    hg = _dot(h, wg_ref[...])
    hu = _dot(h, wu_ref[...])
    act = hg * _sigmoid(hg) * hu
    if has_gate:
        e = pl.program_id(1)
        gate = gate_ref[...]
        lane = lax.broadcasted_iota(jnp.int32, gate.shape, 1)
        act = act * jnp.sum(jnp.where(lane == e, gate, 0.0), axis=-1, keepdims=True)
    o_ref[...] = act.astype(o_ref.dtype)


def swiglu_call(h, wg, wu, gate=None, tm=1024, tn=512):
    t, d = h.shape
    ne, _, f = wg.shape
    assert f % tn == 0 and t % tm == 0
    nj = f // tn
    in_specs = [
        pl.BlockSpec((tm, d), lambda i, e, j: (i, 0)),
        pl.BlockSpec((None, d, tn), lambda i, e, j: (e, 0, j)),
        pl.BlockSpec((None, d, tn), lambda i, e, j: (e, 0, j)),
    ]
    args = [h, wg, wu]
    if gate is not None:
        in_specs.append(pl.BlockSpec((tm, LANES), lambda i, e, j: (i, 0)))
        args.append(gate)
    return pl.pallas_call(
        functools.partial(_swiglu_kernel, has_gate=gate is not None),
        grid=(t // tm, ne, nj),
        in_specs=in_specs,
        out_specs=pl.BlockSpec((tm, tn), lambda i, e, j: (i, e * nj + j)),
        out_shape=jax.ShapeDtypeStruct((t, ne * f), BF16),
        compiler_params=_cparams("parallel", "arbitrary", "arbitrary"),
        name="swiglu",
    )(*args)


def _router_kernel(h_ref, w_ref, b_ref, o_ref):
    logits = _dot(h_ref[...], w_ref[...]) + b_ref[...]
    lane = lax.broadcasted_iota(jnp.int32, logits.shape, 1)
    logits = jnp.where(lane < N_EXPERTS, logits, MASKED)
    v1 = jnp.max(logits, axis=-1, keepdims=True)
    i1 = jnp.min(jnp.where(logits == v1, lane, LANES), axis=-1, keepdims=True)
    rest = jnp.where(lane == i1, MASKED, logits)
    v2 = jnp.max(rest, axis=-1, keepdims=True)
    i2 = jnp.min(jnp.where(rest == v2, lane, LANES), axis=-1, keepdims=True)
    e2 = jnp.exp(v2 - v1)
    inv = 1.0 / (1.0 + e2)
    o_ref[...] = jnp.where(lane == i1, inv, jnp.where(lane == i2, e2 * inv, 0.0))


def router_call(h, w, b, tm=1024):
    t, d = h.shape
    return pl.pallas_call(
        _router_kernel,
        grid=(t // tm,),
        in_specs=[pl.BlockSpec((tm, d), lambda i: (i, 0)), pl.BlockSpec((d, LANES), lambda i: (0, 0)),
                  pl.BlockSpec((1, LANES), lambda i: (0, 0))],
        out_specs=pl.BlockSpec((tm, LANES), lambda i: (i, 0)),
        out_shape=jax.ShapeDtypeStruct((t, LANES), F32),
        compiler_params=_cparams("parallel"),
        name="router",
    )(h, w, b)


def _pool_kernel(cur_ref, halo_ref, w_ref, scale_ref, o_ref, xs_ref, *, ts, halo):
    i = pl.program_id(1)
    prev = halo_ref[...].astype(F32)
    xs_ref[0:halo, :] = jnp.where(i == 0, 0.0, prev)
    xs_ref[halo:halo + ts, :] = cur_ref[...].astype(F32)
    row = lax.broadcasted_iota(jnp.int32, (ts, POOL_GW), 0) + i * ts + 1
    for gi, w in enumerate(POOL_WINDOWS):
        c0 = gi * POOL_GW
        x = xs_ref[halo:halo + ts, c0:c0 + POOL_GW]
        wsum = x
        for k in range(1, w):
            wsum = wsum + xs_ref[halo - k:halo - k + ts, c0:c0 + POOL_GW]
        cnt = jnp.minimum(row, w).astype(F32)
        diff = (wsum / cnt - x).astype(BF16)
        y = _dot(diff, w_ref[gi]) * scale_ref[:, c0:c0 + POOL_GW]
        o_ref[:, c0:c0 + POOL_GW] = y.astype(o_ref.dtype)


def pool_call(p, w_pool, scale, batch, ts=512):
    t = p.shape[0]
    seq = t // batch
    nts = seq // ts
    halo = 16
    cb = A0 // MIX_W
    return pl.pallas_call(
        functools.partial(_pool_kernel, ts=ts, halo=halo),
        grid=(batch, nts),
        in_specs=[
            pl.BlockSpec((ts, MIX_W), lambda b, i: (b * nts + i, cb)),
            pl.BlockSpec((halo, MIX_W), lambda b, i: (jnp.maximum((b * nts + i) * (ts // halo) - 1, 0), cb)),
            pl.BlockSpec((len(POOL_WINDOWS), POOL_GW, POOL_GW), lambda b, i: (0, 0, 0)),
            pl.BlockSpec((1, MIX_W), lambda b, i: (0, 0)),
        ],
        out_specs=pl.BlockSpec((ts, MIX_W), lambda b, i: (b * nts + i, 0)),
        out_shape=jax.ShapeDtypeStruct((t, MIX_W), BF16),
        scratch_shapes=[pltpu.VMEM((halo + ts, MIX_W), F32)],
        compiler_params=_cparams("parallel", "arbitrary"),
        name="pool",
    )(p, p, w_pool, scale)


def _layernorm(v, g, b):
    mu = jnp.mean(v, axis=-1, keepdims=True)
    vc = v - mu
    var = jnp.mean(vc * vc, axis=-1, keepdims=True)
    return (vc * lax.rsqrt(var + EPS)) * g + b


def _gmlp_kernel(z_ref, g_ref, b_ref, ws_ref, bs_ref, o_ref, *, ts):
    z = _gelu_tanh(z_ref[...].astype(F32))
    u = z[:, :MIX_W]
    v = _layernorm(z[:, MIX_W:], g_ref[...], b_ref[...]).astype(BF16)
    ri = lax.broadcasted_iota(jnp.int32, (GMLP_CHUNK, GMLP_CHUNK), 0)
    ci = lax.broadcasted_iota(jnp.int32, (GMLP_CHUNK, GMLP_CHUNK), 1)
    tri = ci <= ri
    for h in range(MIX_W // HEAD_DIM):
        w = jnp.where(tri, ws_ref[h], 0.0).astype(BF16)
        bias = bs_ref[:, h * HEAD_DIM:(h + 1) * HEAD_DIM]
        for c in range(ts // GMLP_CHUNK):
            rows = slice(c * GMLP_CHUNK, (c + 1) * GMLP_CHUNK)
            cols = slice(h * HEAD_DIM, (h + 1) * HEAD_DIM)
            s = _dot(w, v[rows, cols]) + bias
            o_ref[rows, cols] = (u[rows, cols] * s).astype(o_ref.dtype)


def gmlp_call(p, ln_g, ln_b, ws, bs_b, ts=512):
    t = p.shape[0]
    return pl.pallas_call(
        functools.partial(_gmlp_kernel, ts=ts),
        grid=(t // ts,),
        in_specs=[
            pl.BlockSpec((ts, 2 * MIX_W), lambda i: (i, C0 // (2 * MIX_W))),
            pl.BlockSpec((1, MIX_W), lambda i: (0, 0)),
            pl.BlockSpec((1, MIX_W), lambda i: (0, 0)),
            pl.BlockSpec(ws.shape, lambda i: (0, 0, 0)),
            pl.BlockSpec(bs_b.shape, lambda i: (0, 0)),
        ],
        out_specs=pl.BlockSpec((ts, MIX_W), lambda i: (i, 0)),
        out_shape=jax.ShapeDtypeStruct((t, MIX_W), BF16),
        compiler_params=_cparams("parallel"),
        name="gmlp",
    )(p, ln_g, ln_b, ws, bs_b)


def _conv_kernel(cur_ref, halo_ref, w_ref, b_ref, g_ref, beta_ref, o_ref, hs_ref, *, ts, halo):
    i = pl.program_id(1)

    def glu(z):
        zf = z.astype(F32)
        return zf[:, :MIX_W] * _sigmoid(zf[:, MIX_W:])

    hs_ref[0:halo, :] = jnp.where(i == 0, 0.0, glu(halo_ref[...]))
    hs_ref[halo:halo + ts, :] = glu(cur_ref[...])
    off = halo - (CONV_W - 1)
    acc = jnp.zeros((ts, MIX_W), F32) + b_ref[...]
    for k in range(CONV_W):
        acc = acc + w_ref[k:k + 1, :] * hs_ref[off + k:off + k + ts, :]
    y = _layernorm(acc, g_ref[...], beta_ref[...])
    o_ref[...] = (y * _sigmoid(y)).astype(o_ref.dtype)


def conv_call(p, w, b, ln_g, ln_b, batch, ts=256):
    t = p.shape[0]
    seq = t // batch
    nts = seq // ts
    halo = 32
    cb = D0 // (2 * MIX_W)
    return pl.pallas_call(
        functools.partial(_conv_kernel, ts=ts, halo=halo),
        grid=(batch, nts),
        in_specs=[
            pl.BlockSpec((ts, 2 * MIX_W), lambda b_, i: (b_ * nts + i, cb)),
            pl.BlockSpec((halo, 2 * MIX_W), lambda b_, i: (jnp.maximum((b_ * nts + i) * (ts // halo) - 1, 0), cb)),
            pl.BlockSpec((CONV_W, MIX_W), lambda b_, i: (0, 0)),
            pl.BlockSpec((1, MIX_W), lambda b_, i: (0, 0)),
            pl.BlockSpec((1, MIX_W), lambda b_, i: (0, 0)),
            pl.BlockSpec((1, MIX_W), lambda b_, i: (0, 0)),
        ],
        out_specs=pl.BlockSpec((ts, MIX_W), lambda b_, i: (b_ * nts + i, 0)),
        out_shape=jax.ShapeDtypeStruct((t, MIX_W), BF16),
        scratch_shapes=[pltpu.VMEM((halo + ts, MIX_W), F32)],
        compiler_params=_cparams("parallel", "arbitrary"),
        name="conv",
    )(p, p, w, b, ln_g, ln_b)


def _head_rms(x, g):
    xf = x.astype(F32)
    r = lax.rsqrt(jnp.mean(xf * xf, axis=-1, keepdims=True) + EPS)
    return (xf * r) * g


def _nsa_prep_kernel(q_ref, ks_ref, kw_ref, qg_ref, kg_ref, qo_ref, kso_ref, kwo_ref):
    qg = qg_ref[...]
    kg = kg_ref[...]
    for h in range(NSA_HEADS):
        cols = slice(h * HEAD_DIM, (h + 1) * HEAD_DIM)
        qo_ref[:, cols] = (_head_rms(q_ref[:, cols], qg) * (HEAD_DIM ** -0.5)).astype(qo_ref.dtype)
    for g in range(NSA_KV):
        cols = slice(g * HEAD_DIM, (g + 1) * HEAD_DIM)
        kso_ref[:, cols] = _head_rms(ks_ref[:, cols], kg).astype(kso_ref.dtype)
        kwo_ref[:, cols] = _head_rms(kw_ref[:, cols], kg).astype(kwo_ref.dtype)


def nsa_prep_call(p, q_g, k_g, tr=512):
    t = p.shape[0]
    kvw = NSA_KV * HEAD_DIM
    return pl.pallas_call(
        _nsa_prep_kernel,
        grid=(t // tr,),
        in_specs=[
            pl.BlockSpec((tr, MIX_W), lambda i: (i, Q0 // MIX_W)),
            pl.BlockSpec((tr, kvw), lambda i: (i, (KV0 + 2 * kvw) // kvw)),
            pl.BlockSpec((tr, kvw), lambda i: (i, (KV0 + 4 * kvw) // kvw)),
            pl.BlockSpec((1, HEAD_DIM), lambda i: (0, 0)),
            pl.BlockSpec((1, HEAD_DIM), lambda i: (0, 0)),
        ],
        out_specs=[
            pl.BlockSpec((tr, MIX_W), lambda i: (i, 0)),
            pl.BlockSpec((tr, kvw), lambda i: (i, 0)),
            pl.BlockSpec((tr, kvw), lambda i: (i, 0)),
        ],
        out_shape=[jax.ShapeDtypeStruct((t, MIX_W), BF16), jax.ShapeDtypeStruct((t, kvw), BF16),
                   jax.ShapeDtypeStruct((t, kvw), BF16)],
        compiler_params=_cparams("parallel"),
        name="nsa_prep",
    )(p, p, p, q_g, k_g)


def _compress_kernel(ck_ref, cv_ref, pk_ref, pv_ref, w1k_ref, w2k_ref, w1v_ref, w2v_ref, kg_ref, ko_ref, vo_ref, *, nc):
    def comp(c_ref, p_ref, w1_ref, w2_ref):
        c = c_ref[...].astype(F32)
        xa = (c + p_ref[0:1, :]).astype(BF16)
        xb = (c + p_ref[1:2, :]).astype(BF16)
        first = _dot(xa, w1_ref[0])
        second = _dot(xb, w1_ref[1])
        hdn = _gelu_tanh(first + pltpu.roll(second, nc - 1, 0))
        return _dot(hdn.astype(BF16), w2_ref[...])

    row = lax.broadcasted_iota(jnp.int32, (nc, HEAD_DIM), 0)
    real = row < nc - 1
    kc = _head_rms(comp(ck_ref, pk_ref, w1k_ref, w2k_ref), kg_ref[...])
    vc = comp(cv_ref, pv_ref, w1v_ref, w2v_ref)
    zeros = jnp.zeros((KC_PAD, HEAD_DIM), F32)
    ko_ref[0:KC_PAD, :] = zeros
    vo_ref[0:KC_PAD, :] = zeros
    ko_ref[KC_PAD:KC_PAD + nc, :] = jnp.where(real, kc, 0.0)
    vo_ref[KC_PAD:KC_PAD + nc, :] = jnp.where(real, vc, 0.0)


def compress_call(ck, cv, pos_k, pos_v, w1k, w2k, w1v, w2v, k_g):
    bg, nc, half = ck.shape
    full2 = lambda shape: pl.BlockSpec(shape, lambda i: (0,) * len(shape))
    out_sds = jax.ShapeDtypeStruct((bg, KC_PAD + nc, HEAD_DIM), F32)
    return pl.pallas_call(
        functools.partial(_compress_kernel, nc=nc),
        grid=(bg,),
        in_specs=[
            pl.BlockSpec((None, nc, half), lambda i: (i, 0, 0)),
            pl.BlockSpec((None, nc, half), lambda i: (i, 0, 0)),
            full2((2, half)), full2((2, half)),
            full2((2, half, HEAD_DIM)), full2((HEAD_DIM, HEAD_DIM)),
            full2((2, half, HEAD_DIM)), full2((HEAD_DIM, HEAD_DIM)),
            full2((1, HEAD_DIM)),
        ],
        out_specs=[pl.BlockSpec((None, KC_PAD + nc, HEAD_DIM), lambda i: (i, 0, 0))] * 2,
        out_shape=[out_sds, out_sds],
        compiler_params=_cparams("parallel"),
        name="nsa_compress",
    )(ck, cv, pos_k, pos_v, w1k, w2k, w1v, w2v, k_g)


def _softmax_rows(parts):
    m = parts[0].max(axis=-1, keepdims=True)
    for x in parts[1:]:
        m = jnp.maximum(m, x.max(axis=-1, keepdims=True))
    m = jnp.maximum(m, -1e20)
    ps = [jnp.exp(x - m) for x in parts]
    s = ps[0].sum(axis=-1, keepdims=True)
    for p in ps[1:]:
        s = s + p.sum(axis=-1, keepdims=True)
    inv = 1.0 / jnp.maximum(s, 1e-30)
    return ps, inv


def _split_hi_lo(x):
    hi = x.astype(BF16)
    lo = (x - hi.astype(F32)).astype(BF16)
    return hi, lo


def _nsa_kernel(q_ref, kc_ref, vc_ref, ks_ref, vs_ref, kw_ref, vw_ref, ng_ref,
                bcn_ref, bs_ref, bw_ref, ov_ref, o_ref, m_ref, l_ref, acc_ref):
    i = pl.program_id(2)
    qb = Q_BLOCK
    rows = NSA_HPG * qb
    hd = HEAD_DIM
    qs = jnp.concatenate([q_ref[:, r * hd:(r + 1) * hd] for r in range(NSA_HPG)], axis=0)
    kcp = kc_ref.shape[0]

    near0 = pl.multiple_of(8 * i + 8, 8)
    lf = _dot_nt(qs, kc_ref[...].astype(BF16))
    npad = lax.broadcasted_iota(jnp.int32, (rows, kcp), 1)
    lf = jnp.where((npad >= KC_PAD) & (npad < near0), lf, MASKED)
    ln = _dot_nt(qs, kc_ref[pl.ds(near0, qb), :].astype(BF16)) + bcn_ref[...]
    ncol = lax.broadcasted_iota(jnp.int32, (rows, qb), 1)
    ln = jnp.where(ncol >= KC_PAD - 8 - 8 * i, ln, MASKED)
    (pf, pn), inv = _softmax_rows([lf, ln])
    pf = pf * inv
    pn = pn * inv
    o_c = _dot(pf.astype(BF16), vc_ref[...].astype(BF16)) + _dot(pn.astype(BF16), vc_ref[pl.ds(near0, qb), :].astype(BF16))

    pf_g = pf[0:qb]
    pn_g = pn[0:qb]
    for r in range(1, NSA_HPG):
        pf_g = pf_g + pf[r * qb:(r + 1) * qb]
        pn_g = pn_g + pn[r * qb:(r + 1) * qb]
    ov_all = ov_ref[...].astype(BF16)
    ov_near = ov_ref[pl.ds(near0, qb), :].astype(BF16)
    imp = jnp.zeros((qb, ov_ref.shape[1]), F32)
    for part in _split_hi_lo(pf_g):
        imp = imp + _dot(part, ov_all)
    for part in _split_hi_lo(pn_g):
        imp = imp + _dot(part, ov_near)
    nblk = imp.shape[1]
    jj = lax.broadcasted_iota(jnp.int32, (qb, nblk), 1)
    tq = lax.broadcasted_iota(jnp.int32, (qb, nblk), 0) + i * qb
    cur = jnp.right_shift(tq, 6)
    forced = (jj == 0) | (jj == cur) | (jj == cur - 1)
    score = jnp.where(jj <= cur, imp + jnp.where(forced, FORCE_BONUS, 0.0), NEG)
    sc = score.T
    jrow = lax.broadcasted_iota(jnp.int32, sc.shape, 0).astype(F32)
    sel_t = jnp.zeros(sc.shape, F32)
    for _ in range(min(SLC_TOPK, nblk)):
        mx = jnp.max(sc, axis=0, keepdims=True)
        first = jnp.min(jnp.where(sc == mx, jrow, float(nblk)), axis=0, keepdims=True)
        pick = jrow == first
        sel_t = jnp.where(pick, 1.0, sel_t)
        sc = jnp.where(pick, -3e38, sc)
    sel = sel_t.T.astype(BF16)

    def sel_penalty(kpos, valid):
        jcol = lax.broadcasted_iota(jnp.int32, kpos.shape, 1)
        hit = (jnp.right_shift(kpos, 6) == jcol) & valid
        expand = jnp.where(hit, 1.0, 0.0).astype(BF16)
        return jnp.where(_dot_nt(sel, expand) > 0.5, 0.0, MASKED)

    prev0 = pl.multiple_of(jnp.maximum(i - 1, 0) * qb, qb)
    diag0 = pl.multiple_of(i * qb, qb)
    k_near = jnp.concatenate([ks_ref[pl.ds(prev0, qb), :], ks_ref[pl.ds(diag0, qb), :]], axis=0)
    v_near = jnp.concatenate([vs_ref[pl.ds(prev0, qb), :], vs_ref[pl.ds(diag0, qb), :]], axis=0)
    kpos = lax.broadcasted_iota(jnp.int32, (2 * qb, nblk), 0) + (i - 1) * qb
    pen = sel_penalty(kpos, kpos >= 0)
    l_near = _dot_nt(qs, k_near) + bs_ref[...]
    for r in range(NSA_HPG):
        rs = slice(r * qb, (r + 1) * qb)
        l_r = l_near[rs] + pen
        m_r = jnp.maximum(l_r.max(axis=-1, keepdims=True), -1e20)
        p_r = jnp.exp(l_r - m_r)
        m_ref[rs, :] = m_r
        l_ref[rs, :] = p_r.sum(axis=-1, keepdims=True)
        acc_ref[rs, :] = _dot(p_r.astype(BF16), v_near)

    far_end = (i - 1) * qb
    chunk = 4 * qb

    def far_body(c, carry):
        c0 = pl.multiple_of(c * chunk, chunk)
        k_c = ks_ref[pl.ds(c0, chunk), :]
        v_c = vs_ref[pl.ds(c0, chunk), :]
        kp = lax.broadcasted_iota(jnp.int32, (chunk, nblk), 0) + c0
        pen_c = sel_penalty(kp, kp < far_end)
        l_far = _dot_nt(qs, k_c)
        for r in range(NSA_HPG):
            rs = slice(r * qb, (r + 1) * qb)
            l_r = l_far[rs] + pen_c
            m_old = m_ref[rs, :]
            m_new = jnp.maximum(m_old, l_r.max(axis=-1, keepdims=True))
            alpha = jnp.exp(m_old - m_new)
            p_r = jnp.exp(l_r - m_new)
            m_ref[rs, :] = m_new
            l_ref[rs, :] = alpha * l_ref[rs, :] + p_r.sum(axis=-1, keepdims=True)
            acc_ref[rs, :] = alpha * acc_ref[rs, :] + _dot(p_r.astype(BF16), v_c)
        return carry

    lax.fori_loop(0, (i + 2) // 4, far_body, 0)
    o_s = acc_ref[...] * (1.0 / jnp.maximum(l_ref[...], 1e-30))

    nwin = WINDOW // qb + 1
    starts = [pl.multiple_of(jnp.maximum(i - (nwin - 1 - tt), 0) * qb, qb) for tt in range(nwin)]
    k_win = jnp.concatenate([kw_ref[pl.ds(s0, qb), :] for s0 in starts], axis=0)
    v_win = jnp.concatenate([vw_ref[pl.ds(s0, qb), :] for s0 in starts], axis=0)
    lw = _dot_nt(qs, k_win) + bw_ref[...]
    wcol = lax.broadcasted_iota(jnp.int32, lw.shape, 1)
    lw = jnp.where(wcol >= (nwin - 1 - i) * qb, lw, MASKED)
    (pw,), inv_w = _softmax_rows([lw])
    o_w = _dot(pw.astype(BF16), v_win) * inv_w

    gate = _sigmoid(ng_ref[...].astype(F32))
    for r in range(NSA_HPG):
        rs = slice(r * qb, (r + 1) * qb)
        o = (gate[:, 3 * r:3 * r + 1] * o_c[rs] + gate[:, 3 * r + 1:3 * r + 2] * o_s[rs]
             + gate[:, 3 * r + 2:3 * r + 3] * o_w[rs])
        o_ref[:, r * hd:(r + 1) * hd] = o.astype(o_ref.dtype)


def nsa_call(p, qn, ksn, kwn, kcp, vcp, bcn, bsl, bwn, ovl, batch):
    t = p.shape[0]
    seq = t // batch
    nq = seq // Q_BLOCK
    g_ = NSA_KV
    kcp_rows = kcp.shape[1]
    rows = NSA_HPG * Q_BLOCK
    vs_blk = (KV0 + 3 * g_ * HEAD_DIM) // HEAD_DIM
    vw_blk = (KV0 + 5 * g_ * HEAD_DIM) // HEAD_DIM
    ng_blk = NG0 // LANES
    seq_spec = lambda col0: pl.BlockSpec((seq, HEAD_DIM), lambda b, g, i: (b, col0 + g))
    tbl_spec = lambda w: pl.BlockSpec((None, rows, w), lambda b, g, i: (g, 0, 0))
    return pl.pallas_call(
        _nsa_kernel,
        grid=(batch, g_, nq),
        in_specs=[
            pl.BlockSpec((Q_BLOCK, rows), lambda b, g, i: (b * nq + i, g)),
            pl.BlockSpec((None, kcp_rows, HEAD_DIM), lambda b, g, i: (b * g_ + g, 0, 0)),
            pl.BlockSpec((None, kcp_rows, HEAD_DIM), lambda b, g, i: (b * g_ + g, 0, 0)),
            seq_spec(0), seq_spec(vs_blk), seq_spec(0), seq_spec(vw_blk),
            pl.BlockSpec((Q_BLOCK, LANES), lambda b, g, i: (b * nq + i, ng_blk + g)),
            tbl_spec(Q_BLOCK), tbl_spec(2 * Q_BLOCK), tbl_spec(WINDOW + Q_BLOCK),
            pl.BlockSpec(ovl.shape, lambda b, g, i: (0, 0)),
        ],
        out_specs=pl.BlockSpec((Q_BLOCK, rows), lambda b, g, i: (b * nq + i, g)),
        out_shape=jax.ShapeDtypeStruct((t, MIX_W), BF16),
        scratch_shapes=[pltpu.VMEM((rows, 1), F32), pltpu.VMEM((rows, 1), F32), pltpu.VMEM((rows, HEAD_DIM), F32)],
        compiler_params=_cparams("parallel", "parallel", "arbitrary"),
        name="nsa_attention",
    )(qn, kcp, vcp, ksn, p, kwn, p, p, bcn, bsl, bwn, ovl)


def _t5_bucket_np(dist):
    n = np.maximum(dist, 0)
    max_exact = REL_BUCKETS // 2
    nf = np.maximum(n, 1).astype(np.float32)
    large = max_exact + (np.log(nf / max_exact) / math.log(REL_MAX_DIST / max_exact)
                         * (REL_BUCKETS - max_exact)).astype(np.int32)
    large = np.minimum(large, REL_BUCKETS - 1)
    return np.where(n < max_exact, n, large)


def _bias_tables(rel_bias, seq):
    qb = Q_BLOCK
    iq = np.arange(qb)[:, None]

    def tile(dist, visible):
        b = rel_bias[_t5_bucket_np(dist)] - rel_bias[REL_BUCKETS - 1][None, None, :]
        b = jnp.where(visible[:, :, None], b, MASKED)
        w = dist.shape[1]
        return b.transpose(2, 0, 1).reshape(NSA_KV, NSA_HPG * qb, w)

    jk = np.arange(qb)[None, :]
    d_diag = iq - jk
    d_prev = qb + iq - jk
    d_edge = WINDOW + iq - jk
    d_far = np.full((qb, qb), 2 * qb)
    d_sel = np.concatenate([d_prev, d_diag], axis=1)
    d_win = np.concatenate([d_edge] + [d_far] * (WINDOW // qb - 2) + [d_prev, d_diag], axis=1)
    d_cmp = iq - CMP_STRIDE * (jk - (KC_PAD - 8)) - (CMP_LEN - 1)
    bsl = tile(d_sel, d_sel >= 0)
    bwn = tile(d_win, (d_win >= 0) & (d_win < WINDOW))
    bcn = tile(d_cmp, d_cmp >= 0)
    nc = seq // CMP_STRIDE
    n_slc = seq // SLC_LEN
    c0 = (np.arange(nc) * CMP_STRIDE)[:, None]
    s0 = (np.arange(n_slc) * SLC_LEN)[None, :]
    ov = ((c0 < s0 + SLC_LEN) & (c0 + CMP_LEN > s0) & (np.arange(nc)[:, None] < nc - 1)).astype(np.float32)
    ovl = np.concatenate([np.zeros((KC_PAD, n_slc), np.float32), ov], axis=0)
    return bcn, bsl, bwn, jnp.asarray(ovl)


def _merge_kernel(g_ref, a_ref, b_ref, c_ref, d_ref, wgu_ref, wbr_ref, o_ref):
    gin = g_ref[...]
    acc = None
    for bi, br in enumerate((a_ref, b_ref, c_ref, d_ref)):
        gate = _sigmoid(_dot(gin, wgu_ref[bi]))
        term = gate * _dot(br[...], wbr_ref[bi])
        acc = term if acc is None else acc + term
    o_ref[...] = acc.astype(o_ref.dtype)


def merge_call(p, branches, wgu, wbr, tm=1024, tn=512):
    t = p.shape[0]
    d = wgu.shape[2]
    row = lambda w, cb: pl.BlockSpec((tm, w), lambda i, j: (i, cb))
    return pl.pallas_call(
        _merge_kernel,
        grid=(t // tm, d // tn),
        in_specs=[row(GATE_RANK, G0 // GATE_RANK)] + [row(MIX_W, 0)] * 4 + [
            pl.BlockSpec((4, GATE_RANK, tn), lambda i, j: (0, 0, j)),
            pl.BlockSpec((4, MIX_W, tn), lambda i, j: (0, 0, j)),
        ],
        out_specs=pl.BlockSpec((tm, tn), lambda i, j: (i, j)),
        out_shape=jax.ShapeDtypeStruct((t, d), BF16),
        compiler_params=_cparams("parallel", "arbitrary"),
        name="merge",
    )(p, *branches, wgu, wbr)


def _rearranged_w_in(w_in):
    o_a, o_q, o_kv, o_ng, o_c, o_d, o_g = 0, 1024, 2048, 3584, 3608, 5656, 7704
    l, d, _ = w_in.shape
    hpg3 = NSA_HPG * 3
    zpad = lambda n: jnp.zeros((l, d, n), w_in.dtype)
    parts = [
        w_in[:, :, o_c:o_c + 2048], w_in[:, :, o_d:o_d + 2048], w_in[:, :, o_a:o_a + 1024],
        w_in[:, :, o_q:o_q + 1024], w_in[:, :, o_kv:o_kv + 1536], w_in[:, :, o_g:o_g + 512],
        w_in[:, :, o_ng:o_ng + hpg3], zpad(LANES - hpg3),
        w_in[:, :, o_ng + hpg3:o_ng + 2 * hpg3], zpad(LANES - hpg3), zpad(PCOLS - NG0 - 2 * LANES),
    ]
    return jnp.concatenate(parts, axis=-1).astype(BF16)


def kernel(x, rel_bias, norm_mix_g, norm_ffn_g, w_in, pool_w, pool_scale, q_norm_g, k_norm_g, cmp_pos_k, cmp_w1_k, cmp_w2_k, cmp_pos_v, cmp_w1_v, cmp_w2_v, gmlp_ln_g, gmlp_ln_b, gmlp_ws, gmlp_bs, conv_w, conv_b, conv_ln_g, conv_ln_b, w_branch, w_gate_up, w_out, ffn_w_gate, ffn_w_up, ffn_w_down, moe_router, moe_router_b, moe_w_gate, moe_w_up, moe_w_down):
    batch, seq, d = x.shape
    t = batch * seq
    depth = w_in.shape[0]
    nc = seq // CMP_STRIDE
    half = CMP_STRIDE * HEAD_DIM
    xf = x.reshape(t, d)

    w_in_r = _rearranged_w_in(w_in)
    bcn, bsl, bwn, ovl = _bias_tables(rel_bias, seq)
    row = lambda v: v.reshape(1, -1)

    for l in range(depth):
        h = rmsnorm_call(xf, row(norm_mix_g[l]))
        p = matmul_call(h, w_in_r[l], name="in_proj")
        o_a = pool_call(p, pool_w[l].astype(BF16), row(pool_scale[l]), batch)
        qn, ksn, kwn = nsa_prep_call(p, row(q_norm_g[l]), row(k_norm_g[l]))
        kvc = p[:, KV0:KV0 + 2 * NSA_KV * HEAD_DIM].reshape(batch, nc, CMP_STRIDE, 2, NSA_KV, HEAD_DIM)
        kvc = kvc.transpose(3, 0, 4, 1, 2, 5).reshape(2, batch * NSA_KV, nc, half)
        kcp, vcp = compress_call(
            kvc[0], kvc[1], cmp_pos_k[l].reshape(2, half), cmp_pos_v[l].reshape(2, half),
            cmp_w1_k[l].reshape(2, half, HEAD_DIM).astype(BF16), cmp_w2_k[l].astype(BF16),
            cmp_w1_v[l].reshape(2, half, HEAD_DIM).astype(BF16), cmp_w2_v[l].astype(BF16), row(k_norm_g[l]))
        o_b = nsa_call(p, qn, ksn, kwn, kcp, vcp, bcn, bsl, bwn, ovl, batch)
        bs_b = jnp.repeat(gmlp_bs[l].T, HEAD_DIM, axis=1)
        o_c = gmlp_call(p, row(gmlp_ln_g[l]), row(gmlp_ln_b[l]), gmlp_ws[l], bs_b)
        o_d = conv_call(p, conv_w[l], row(conv_b[l]), row(conv_ln_g[l]), row(conv_ln_b[l]), batch)
        mix = merge_call(p, (o_a, o_b, o_c, o_d), w_gate_up[l].astype(BF16), w_branch[l].astype(BF16))
        xf = matmul_call(mix, w_out[l].astype(BF16), res=xf, out_dtype=F32, name="out_proj")
        h = rmsnorm_call(xf, row(norm_ffn_g[l]))
        i = l // 2
        if l % 2 == 0:
            act = swiglu_call(h, ffn_w_gate[i][None].astype(BF16), ffn_w_up[i][None].astype(BF16))
            xf = matmul_call(act, ffn_w_down[i].astype(BF16), res=xf, out_dtype=F32, name="ffn_down")
        else:
            rw = jnp.pad(moe_router[i], ((0, 0), (0, LANES - N_EXPERTS))).astype(BF16)
            rb = jnp.pad(moe_router_b[i], (0, LANES - N_EXPERTS)).reshape(1, LANES)
            gate = router_call(h, rw, rb)
            act = swiglu_call(h, moe_w_gate[i].astype(BF16), moe_w_up[i].astype(BF16), gate=gate, tn=256)
            wd = moe_w_down[i].reshape(-1, d).astype(BF16)
            xf = matmul_call(act, wd, res=xf, out_dtype=F32, tk=3584, name="moe_down")
    return xf.reshape(batch, seq, d)
```

```python
import functools
import math

import jax
import jax.numpy as jnp
import numpy as np
from jax import lax
from jax.experimental import pallas as pl
from jax.experimental.pallas import tpu as pltpu

F32 = jnp.float32
BF16 = jnp.bfloat16

VMEM_LIMIT_BYTES = 56 * 1024 * 1024
LANES = 128

EPS = 1e-6
HEAD_DIM = 128
MIX_W = 1024
POOL_WINDOWS = (2, 4, 8, 16)
POOL_GW = MIX_W // len(POOL_WINDOWS)
NSA_HEADS = 8
NSA_KV = 2
NSA_HPG = NSA_HEADS // NSA_KV
CMP_LEN = 32
CMP_STRIDE = 16
SLC_LEN = 64
SLC_TOPK = 16
WINDOW = 512
Q_BLOCK = 128
FORCE_BONUS = 1e4
NEG = -1e9
MASKED = -1e30
GMLP_CHUNK = 128
CONV_W = 31
GATE_RANK = 512
REL_BUCKETS = 32
REL_MAX_DIST = 128
N_EXPERTS = 8
TOP_K = 2

C0 = 0
D0 = 2048
A0 = 4096
Q0 = 5120
KV0 = 6144
G0 = 7680
NG0 = 8192
PCOLS = 8704
KC_PAD = 128
VT_ROWS = HEAD_DIM + 16


def _cparams(*sem):
    return pltpu.CompilerParams(dimension_semantics=sem, vmem_limit_bytes=VMEM_LIMIT_BYTES)


def _dot(a, b):
    return jnp.dot(a, b, preferred_element_type=F32)


def _dot_nt(a, b):
    return lax.dot_general(a, b, (((1,), (1,)), ((), ())), preferred_element_type=F32)


def _gelu_tanh(x):
    return 0.5 * x * (1.0 + jnp.tanh(math.sqrt(2.0 / math.pi) * (x + 0.044715 * (x * x * x))))


def _sigmoid(x):
    return 1.0 / (1.0 + jnp.exp(-x))


def _rmsnorm_kernel(x_ref, g_ref, o_ref):
    xf = x_ref[...]
    r = lax.rsqrt(jnp.mean(xf * xf, axis=-1, keepdims=True) + EPS)
    o_ref[...] = ((xf * r) * g_ref[...]).astype(o_ref.dtype)


def rmsnorm_call(x, g, tr=256):
    t, d = x.shape
    return pl.pallas_call(
        _rmsnorm_kernel,
        grid=(t // tr,),
        in_specs=[pl.BlockSpec((tr, d), lambda i: (i, 0)), pl.BlockSpec((1, d), lambda i: (0, 0))],
        out_specs=pl.BlockSpec((tr, d), lambda i: (i, 0)),
        out_shape=jax.ShapeDtypeStruct((t, d), BF16),
        compiler_params=_cparams("parallel"),
        name="rmsnorm",
    )(x, g)


def _mm_kernel(*refs, nk, has_res):
    if has_res:
        a_ref, w_ref, r_ref, o_ref = refs[:4]
        scr = refs[4:]
    else:
        a_ref, w_ref, o_ref = refs[:3]
        r_ref = None
        scr = refs[3:]
    part = _dot(a_ref[...], w_ref[...])

    def finish(acc):
        if has_res:
            acc = acc + r_ref[...]
        o_ref[...] = acc.astype(o_ref.dtype)

    if nk == 1:
        finish(part)
    else:
        acc_ref = scr[0]
        k = pl.program_id(2)

        @pl.when(k == 0)
        def _():
            acc_ref[...] = part

        @pl.when(k > 0)
        def _():
            acc_ref[...] += part

        @pl.when(k == nk - 1)
        def _():
            finish(acc_ref[...])


def matmul_call(a, w, res=None, out_dtype=BF16, tm=1024, tn=512, tk=4096, name="matmul"):
    m, kdim = a.shape
    n = w.shape[1]
    tk = min(tk, kdim)
    nk = kdim // tk
    assert m % tm == 0 and n % tn == 0 and kdim % tk == 0
    in_specs = [pl.BlockSpec((tm, tk), lambda i, j, k: (i, k)), pl.BlockSpec((tk, tn), lambda i, j, k: (k, j))]
    args = [a, w]
    if res is not None:
        in_specs.append(pl.BlockSpec((tm, tn), lambda i, j, k: (i, j)))
        args.append(res)
    return pl.pallas_call(
        functools.partial(_mm_kernel, nk=nk, has_res=res is not None),
        grid=(m // tm, n // tn, nk),
        in_specs=in_specs,
        out_specs=pl.BlockSpec((tm, tn), lambda i, j, k: (i, j)),
        out_shape=jax.ShapeDtypeStruct((m, n), out_dtype),
        scratch_shapes=[pltpu.VMEM((tm, tn), F32)] if nk > 1 else [],
        compiler_params=_cparams("parallel", "parallel", "arbitrary"),
        name=name,
    )(*args)


def _swiglu_kernel(*refs, has_gate):
    if has_gate:
        h_ref, wg_ref, wu_ref, gate_ref, o_ref = refs
    else:
        h_ref, wg_ref, wu_ref, o_ref = refs
    h = h_ref[...]
    hg = _dot(h, wg_ref[...])
    hu = _dot(h, wu_ref[...])
    act = hg * _sigmoid(hg) * hu
    if has_gate:
        e = pl.program_id(1)
        gate = gate_ref[...]
        lane = lax.broadcasted_iota(jnp.int32, gate.shape, 1)
        act = act * jnp.sum(jnp.where(lane == e, gate, 0.0), axis=-1, keepdims=True)
    o_ref[...] = act.astype(o_ref.dtype)


def swiglu_call(h, wg, wu, gate=None, tm=1024, tn=512):
    t, d = h.shape
    ne, _, f = wg.shape
    assert f % tn == 0 and t % tm == 0
    nj = f // tn
    in_specs = [
        pl.BlockSpec((tm, d), lambda i, e, j: (i, 0)),
        pl.BlockSpec((None, d, tn), lambda i, e, j: (e, 0, j)),
        pl.BlockSpec((None, d, tn), lambda i, e, j: (e, 0, j)),
    ]
    args = [h, wg, wu]
    if gate is not None:
        in_specs.append(pl.BlockSpec((tm, LANES), lambda i, e, j: (i, 0)))
        args.append(gate)
    return pl.pallas_call(
        functools.partial(_swiglu_kernel, has_gate=gate is not None),
        grid=(t // tm, ne, nj),
        in_specs=in_specs,
        out_specs=pl.BlockSpec((tm, tn), lambda i, e, j: (i, e * nj + j)),
        out_shape=jax.ShapeDtypeStruct((t, ne * f), BF16),
        compiler_params=_cparams("parallel", "arbitrary", "arbitrary"),
        name="swiglu",
    )(*args)


def _router_kernel(h_ref, w_ref, b_ref, o_ref):
    logits = _dot(h_ref[...], w_ref[...]) + b_ref[...]
    lane = lax.broadcasted_iota(jnp.int32, logits.shape, 1)
    logits = jnp.where(lane < N_EXPERTS, logits, MASKED)
    v1 = jnp.max(logits, axis=-1, keepdims=True)
    i1 = jnp.min(jnp.where(logits == v1, lane, LANES), axis=-1, keepdims=True)
    rest = jnp.where(lane == i1, MASKED, logits)
    v2 = jnp.max(rest, axis=-1, keepdims=True)
    i2 = jnp.min(jnp.where(rest == v2, lane, LANES), axis=-1, keepdims=True)
    e2 = jnp.exp(v2 - v1)
    inv = 1.0 / (1.0 + e2)
    o_ref[...] = jnp.where(lane == i1, inv, jnp.where(lane == i2, e2 * inv, 0.0))


def router_call(h, w, b, tm=1024):
    t, d = h.shape
    return pl.pallas_call(
        _router_kernel,
        grid=(t // tm,),
        in_specs=[pl.BlockSpec((tm, d), lambda i: (i, 0)), pl.BlockSpec((d, LANES), lambda i: (0, 0)),
                  pl.BlockSpec((1, LANES), lambda i: (0, 0))],
        out_specs=pl.BlockSpec((tm, LANES), lambda i: (i, 0)),
        out_shape=jax.ShapeDtypeStruct((t, LANES), F32),
        compiler_params=_cparams("parallel"),
        name="router",
    )(h, w, b)


def _pool_kernel(cur_ref, halo_ref, w_ref, scale_ref, o_ref, xs_ref, *, ts, halo):
    i = pl.program_id(1)
    prev = halo_ref[...].astype(F32)
    xs_ref[0:halo, :] = jnp.where(i == 0, 0.0, prev)
    xs_ref[halo:halo + ts, :] = cur_ref[...].astype(F32)
    row = lax.broadcasted_iota(jnp.int32, (ts, POOL_GW), 0) + i * ts + 1
    for gi, w in enumerate(POOL_WINDOWS):
        c0 = gi * POOL_GW
        x = xs_ref[halo:halo + ts, c0:c0 + POOL_GW]
        wsum = x
        for k in range(1, w):
            wsum = wsum + xs_ref[halo - k:halo - k + ts, c0:c0 + POOL_GW]
        cnt = jnp.minimum(row, w).astype(F32)
        diff = (wsum / cnt - x).astype(BF16)
        y = _dot(diff, w_ref[gi]) * scale_ref[:, c0:c0 + POOL_GW]
        o_ref[:, c0:c0 + POOL_GW] = y.astype(o_ref.dtype)


def pool_call(p, w_pool, scale, batch, ts=512):
    t = p.shape[0]
    seq = t // batch
    nts = seq // ts
    halo = 16
    cb = A0 // MIX_W
    return pl.pallas_call(
        functools.partial(_pool_kernel, ts=ts, halo=halo),
        grid=(batch, nts),
        in_specs=[
            pl.BlockSpec((ts, MIX_W), lambda b, i: (b * nts + i, cb)),
            pl.BlockSpec((halo, MIX_W), lambda b, i: (jnp.maximum((b * nts + i) * (ts // halo) - 1, 0), cb)),
            pl.BlockSpec((len(POOL_WINDOWS), POOL_GW, POOL_GW), lambda b, i: (0, 0, 0)),
            pl.BlockSpec((1, MIX_W), lambda b, i: (0, 0)),
        ],
        out_specs=pl.BlockSpec((ts, MIX_W), lambda b, i: (b * nts + i, 0)),
        out_shape=jax.ShapeDtypeStruct((t, MIX_W), BF16),
        scratch_shapes=[pltpu.VMEM((halo + ts, MIX_W), F32)],
        compiler_params=_cparams("parallel", "arbitrary"),
        name="pool",
    )(p, p, w_pool, scale)


def _layernorm(v, g, b):
    mu = jnp.mean(v, axis=-1, keepdims=True)
    vc = v - mu
    var = jnp.mean(vc * vc, axis=-1, keepdims=True)
    return (vc * lax.rsqrt(var + EPS)) * g + b


def _gmlp_kernel(z_ref, g_ref, b_ref, ws_ref, bs_ref, o_ref, *, ts):
    z = _gelu_tanh(z_ref[...].astype(F32))
    u = z[:, :MIX_W]
    v = _layernorm(z[:, MIX_W:], g_ref[...], b_ref[...]).astype(BF16)
    ri = lax.broadcasted_iota(jnp.int32, (GMLP_CHUNK, GMLP_CHUNK), 0)
    ci = lax.broadcasted_iota(jnp.int32, (GMLP_CHUNK, GMLP_CHUNK), 1)
    tri = ci <= ri
    for h in range(MIX_W // HEAD_DIM):
        w = jnp.where(tri, ws_ref[h], 0.0).astype(BF16)
        bias = bs_ref[:, h * HEAD_DIM:(h + 1) * HEAD_DIM]
        for c in range(ts // GMLP_CHUNK):
            rows = slice(c * GMLP_CHUNK, (c + 1) * GMLP_CHUNK)
            cols = slice(h * HEAD_DIM, (h + 1) * HEAD_DIM)
            s = _dot(w, v[rows, cols]) + bias
            o_ref[rows, cols] = (u[rows, cols] * s).astype(o_ref.dtype)


def gmlp_call(p, ln_g, ln_b, ws, bs_b, ts=512):
    t = p.shape[0]
    return pl.pallas_call(
        functools.partial(_gmlp_kernel, ts=ts),
        grid=(t // ts,),
        in_specs=[
            pl.BlockSpec((ts, 2 * MIX_W), lambda i: (i, C0 // (2 * MIX_W))),
            pl.BlockSpec((1, MIX_W), lambda i: (0, 0)),
            pl.BlockSpec((1, MIX_W), lambda i: (0, 0)),
            pl.BlockSpec(ws.shape, lambda i: (0, 0, 0)),
            pl.BlockSpec(bs_b.shape, lambda i: (0, 0)),
        ],
        out_specs=pl.BlockSpec((ts, MIX_W), lambda i: (i, 0)),
        out_shape=jax.ShapeDtypeStruct((t, MIX_W), BF16),
        compiler_params=_cparams("parallel"),
        name="gmlp",
    )(p, ln_g, ln_b, ws, bs_b)


def _conv_kernel(cur_ref, halo_ref, w_ref, b_ref, g_ref, beta_ref, o_ref, hs_ref, *, ts, halo):
    i = pl.program_id(1)

    def glu(z):
        zf = z.astype(F32)
        return zf[:, :MIX_W] * _sigmoid(zf[:, MIX_W:])

    hs_ref[0:halo, :] = jnp.where(i == 0, 0.0, glu(halo_ref[...]))
    hs_ref[halo:halo + ts, :] = glu(cur_ref[...])
    off = halo - (CONV_W - 1)
    acc = jnp.zeros((ts, MIX_W), F32) + b_ref[...]
    for k in range(CONV_W):
        acc = acc + w_ref[k:k + 1, :] * hs_ref[off + k:off + k + ts, :]
    y = _layernorm(acc, g_ref[...], beta_ref[...])
    o_ref[...] = (y * _sigmoid(y)).astype(o_ref.dtype)


def conv_call(p, w, b, ln_g, ln_b, batch, ts=256):
    t = p.shape[0]
    seq = t // batch
    nts = seq // ts
    halo = 32
    cb = D0 // (2 * MIX_W)
    return pl.pallas_call(
        functools.partial(_conv_kernel, ts=ts, halo=halo),
        grid=(batch, nts),
        in_specs=[
            pl.BlockSpec((ts, 2 * MIX_W), lambda b_, i: (b_ * nts + i, cb)),
            pl.BlockSpec((halo, 2 * MIX_W), lambda b_, i: (jnp.maximum((b_ * nts + i) * (ts // halo) - 1, 0), cb)),
            pl.BlockSpec((CONV_W, MIX_W), lambda b_, i: (0, 0)),
            pl.BlockSpec((1, MIX_W), lambda b_, i: (0, 0)),
            pl.BlockSpec((1, MIX_W), lambda b_, i: (0, 0)),
            pl.BlockSpec((1, MIX_W), lambda b_, i: (0, 0)),
        ],
        out_specs=pl.BlockSpec((ts, MIX_W), lambda b_, i: (b_ * nts + i, 0)),
        out_shape=jax.ShapeDtypeStruct((t, MIX_W), BF16),
        scratch_shapes=[pltpu.VMEM((halo + ts, MIX_W), F32)],
        compiler_params=_cparams("parallel", "arbitrary"),
        name="conv",
    )(p, p, w, b, ln_g, ln_b)


def _head_rms(x, g):
    xf = x.astype(F32)
    r = lax.rsqrt(jnp.mean(xf * xf, axis=-1, keepdims=True) + EPS)
    return (xf * r) * g


def _nsa_prep_kernel(q_ref, ks_ref, vs_ref, kw_ref, vw_ref, qg_ref, kg_ref, qo_ref, kso_ref, kwo_ref, vso_ref, vwo_ref, *, tr):
    qg = qg_ref[...]
    kg = kg_ref[...]
    for h in range(NSA_HEADS):
        cols = slice(h * HEAD_DIM, (h + 1) * HEAD_DIM)
        qo_ref[:, cols] = (_head_rms(q_ref[:, cols], qg) * (HEAD_DIM ** -0.5)).astype(qo_ref.dtype)
    ones = jnp.ones((VT_ROWS - HEAD_DIM, Q_BLOCK), vso_ref.dtype)
    for g in range(NSA_KV):
        cols = slice(g * HEAD_DIM, (g + 1) * HEAD_DIM)
        kso_ref[:, cols] = _head_rms(ks_ref[:, cols], kg).astype(kso_ref.dtype)
        kwo_ref[:, cols] = _head_rms(kw_ref[:, cols], kg).astype(kwo_ref.dtype)
        for u in range(tr // Q_BLOCK):
            rows = slice(u * Q_BLOCK, (u + 1) * Q_BLOCK)
            for v_ref, vo_ref in ((vs_ref, vso_ref), (vw_ref, vwo_ref)):
                vo_ref[g, u, 0:HEAD_DIM, :] = v_ref[rows, cols].astype(F32).T.astype(vo_ref.dtype)
                vo_ref[g, u, HEAD_DIM:VT_ROWS, :] = ones


def nsa_prep_call(p, q_g, k_g, batch, tr=512):
    t = p.shape[0]
    seq = t // batch
    nst = seq // tr
    kvw = NSA_KV * HEAD_DIM
    kv_spec = lambda c: pl.BlockSpec((tr, kvw), lambda i: (i, (KV0 + c * kvw) // kvw))
    vt_sds = jax.ShapeDtypeStruct((batch * NSA_KV, seq // Q_BLOCK, VT_ROWS, Q_BLOCK), BF16)
    vt_spec = pl.BlockSpec((NSA_KV, tr // Q_BLOCK, VT_ROWS, Q_BLOCK), lambda i: (i // nst, i % nst, 0, 0))
    return pl.pallas_call(
        functools.partial(_nsa_prep_kernel, tr=tr),
        grid=(t // tr,),
        in_specs=[
            pl.BlockSpec((tr, MIX_W), lambda i: (i, Q0 // MIX_W)),
            kv_spec(2), kv_spec(3), kv_spec(4), kv_spec(5),
            pl.BlockSpec((1, HEAD_DIM), lambda i: (0, 0)),
            pl.BlockSpec((1, HEAD_DIM), lambda i: (0, 0)),
        ],
        out_specs=[
            pl.BlockSpec((tr, MIX_W), lambda i: (i, 0)),
            pl.BlockSpec((tr, kvw), lambda i: (i, 0)),
            pl.BlockSpec((tr, kvw), lambda i: (i, 0)),
            vt_spec, vt_spec,
        ],
        out_shape=[jax.ShapeDtypeStruct((t, MIX_W), BF16), jax.ShapeDtypeStruct((t, kvw), BF16),
                   jax.ShapeDtypeStruct((t, kvw), BF16), vt_sds, vt_sds],
        compiler_params=_cparams("parallel"),
        name="nsa_prep",
    )(p, p, p, p, p, q_g, k_g)


def _compress_kernel(ck_ref, cv_ref, pk_ref, pv_ref, w1k_ref, w2k_ref, w1v_ref, w2v_ref, kg_ref, ko_ref, vo_ref, *, nc):
    def comp(c_ref, p_ref, w1_ref, w2_ref):
        c = c_ref[...].astype(F32)
        xa = (c + p_ref[0:1, :]).astype(BF16)
        xb = (c + p_ref[1:2, :]).astype(BF16)
        first = _dot(xa, w1_ref[0])
        second = _dot(xb, w1_ref[1])
        hdn = _gelu_tanh(first + pltpu.roll(second, nc - 1, 0))
        return _dot(hdn.astype(BF16), w2_ref[...])

    row = lax.broadcasted_iota(jnp.int32, (nc, HEAD_DIM), 0)
    real = row < nc - 1
    kc = _head_rms(comp(ck_ref, pk_ref, w1k_ref, w2k_ref), kg_ref[...])
    vc = comp(cv_ref, pv_ref, w1v_ref, w2v_ref)
    zeros = jnp.zeros((KC_PAD, HEAD_DIM), F32)
    ko_ref[0:KC_PAD, :] = zeros
    vo_ref[0:KC_PAD, :] = zeros
    ko_ref[KC_PAD:KC_PAD + nc, :] = jnp.where(real, kc, 0.0)
    vo_ref[KC_PAD:KC_PAD + nc, :] = jnp.where(real, vc, 0.0)


def compress_call(ck, cv, pos_k, pos_v, w1k, w2k, w1v, w2v, k_g):
    bg, nc, half = ck.shape
    full2 = lambda shape: pl.BlockSpec(shape, lambda i: (0,) * len(shape))
    out_sds = jax.ShapeDtypeStruct((bg, KC_PAD + nc, HEAD_DIM), F32)
    return pl.pallas_call(
        functools.partial(_compress_kernel, nc=nc),
        grid=(bg,),
        in_specs=[
            pl.BlockSpec((None, nc, half), lambda i: (i, 0, 0)),
            pl.BlockSpec((None, nc, half), lambda i: (i, 0, 0)),
            full2((2, half)), full2((2, half)),
            full2((2, half, HEAD_DIM)), full2((HEAD_DIM, HEAD_DIM)),
            full2((2, half, HEAD_DIM)), full2((HEAD_DIM, HEAD_DIM)),
            full2((1, HEAD_DIM)),
        ],
        out_specs=[pl.BlockSpec((None, KC_PAD + nc, HEAD_DIM), lambda i: (i, 0, 0))] * 2,
        out_shape=[out_sds, out_sds],
        compiler_params=_cparams("parallel"),
        name="nsa_compress",
    )(ck, cv, pos_k, pos_v, w1k, w2k, w1v, w2v, k_g)


def _softmax_rows(parts):
    m = parts[0].max(axis=-1, keepdims=True)
    for x in parts[1:]:
        m = jnp.maximum(m, x.max(axis=-1, keepdims=True))
    m = jnp.maximum(m, -1e20)
    ps = [jnp.exp(x - m) for x in parts]
    s = ps[0].sum(axis=-1, keepdims=True)
    for p in ps[1:]:
        s = s + p.sum(axis=-1, keepdims=True)
    inv = 1.0 / jnp.maximum(s, 1e-30)
    return ps, inv


def _split_hi_lo(x):
    hi = x.astype(BF16)
    lo = (x - hi.astype(F32)).astype(BF16)
    return hi, lo


def _nsa_kernel(q_ref, kc_ref, vc_ref, ks_ref, vs_ref, kw_ref, vw_ref, ng_ref,
                bcn_ref, bs_ref, bw_ref, ov_ref, e_ref, o_ref, m_ref, acc_ref):
    i = pl.program_id(2)
    qb = Q_BLOCK
    rows = NSA_HPG * qb
    hd = HEAD_DIM
    qs = jnp.concatenate([q_ref[:, r * hd:(r + 1) * hd] for r in range(NSA_HPG)], axis=0)
    kcp = kc_ref.shape[0]

    near0 = pl.multiple_of(8 * i + 8, 8)
    lf = _dot_nt(qs, kc_ref[...].astype(BF16))
    npad = lax.broadcasted_iota(jnp.int32, (rows, kcp), 1)
    lf = jnp.where((npad >= KC_PAD) & (npad < near0), lf, MASKED)
    ln = _dot_nt(qs, kc_ref[pl.ds(near0, qb), :].astype(BF16)) + bcn_ref[...]
    ncol = lax.broadcasted_iota(jnp.int32, (rows, qb), 1)
    ln = jnp.where(ncol >= KC_PAD - 8 - 8 * i, ln, MASKED)
    (pf, pn), inv = _softmax_rows([lf, ln])
    pf = pf * inv
    pn = pn * inv
    o_c = _dot(pf.astype(BF16), vc_ref[...].astype(BF16)) + _dot(pn.astype(BF16), vc_ref[pl.ds(near0, qb), :].astype(BF16))

    pf_g = pf[0:qb]
    pn_g = pn[0:qb]
    for r in range(1, NSA_HPG):
        pf_g = pf_g + pf[r * qb:(r + 1) * qb]
        pn_g = pn_g + pn[r * qb:(r + 1) * qb]
    ov_all = ov_ref[...].astype(BF16)
    ov_near = ov_ref[pl.ds(near0, qb), :].astype(BF16)
    imp = jnp.zeros((qb, ov_ref.shape[1]), F32)
    for part in _split_hi_lo(pf_g):
        imp = imp + _dot(part, ov_all)
    for part in _split_hi_lo(pn_g):
        imp = imp + _dot(part, ov_near)
    nblk = imp.shape[1]
    jj = lax.broadcasted_iota(jnp.int32, (qb, nblk), 1)
    tq = lax.broadcasted_iota(jnp.int32, (qb, nblk), 0) + i * qb
    cur = jnp.right_shift(tq, 6)
    forced = (jj == 0) | (jj == cur) | (jj == cur - 1)
    score = jnp.where(jj <= cur, imp + jnp.where(forced, FORCE_BONUS, 0.0), NEG)
    sc = score.T
    jrow = lax.broadcasted_iota(jnp.int32, sc.shape, 0).astype(F32)
    sel_t = jnp.zeros(sc.shape, F32)
    for _ in range(min(SLC_TOPK, nblk)):
        mx = jnp.max(sc, axis=0, keepdims=True)
        first = jnp.min(jnp.where(sc == mx, jrow, float(nblk)), axis=0, keepdims=True)
        pick = jrow == first
        sel_t = jnp.where(pick, 1.0, sel_t)
        sc = jnp.where(pick, -3e38, sc)
    sel = sel_t.T

    sel_neg = jnp.where(sel > 0.5, 0.0, MASKED).astype(BF16)
    q_aug = jnp.concatenate([qs, jnp.concatenate([sel_neg] * NSA_HPG, axis=0)], axis=1)

    def block_onehot(e_rows, valid):
        return jnp.where(valid, e_rows, jnp.ones_like(e_rows))

    prev_t = jnp.maximum(i - 1, 0)
    prev0 = pl.multiple_of(prev_t * qb, qb)
    diag0 = pl.multiple_of(i * qb, qb)
    k_near = jnp.concatenate([ks_ref[pl.ds(prev0, qb), :], ks_ref[pl.ds(diag0, qb), :]], axis=0)
    e_near = jnp.concatenate([e_ref[pl.ds(prev0, qb), :], e_ref[pl.ds(diag0, qb), :]], axis=0)
    row_near = lax.broadcasted_iota(jnp.int32, e_near.shape, 0)
    e_near = block_onehot(e_near, row_near >= jnp.where(i == 0, qb, 0))
    lt = _dot_nt(jnp.concatenate([k_near, e_near], axis=1), q_aug) + bs_ref[...]
    m0 = jnp.maximum(jnp.max(lt, axis=0, keepdims=True), -1e20)
    p0 = jnp.exp(lt - m0).astype(BF16)
    vt_near = jnp.concatenate([vs_ref[prev_t], vs_ref[i]], axis=1)
    m_ref[...] = m0
    acc_ref[...] = _dot(vt_near, p0)

    far_end = (i - 1) * qb
    tiles = 4
    chunk = tiles * qb

    def far_body(c, carry):
        c0 = pl.multiple_of(c * chunk, chunk)
        e_c = e_ref[pl.ds(c0, chunk), :]
        row_c = lax.broadcasted_iota(jnp.int32, e_c.shape, 0) + c0
        k_aug = jnp.concatenate([ks_ref[pl.ds(c0, chunk), :], block_onehot(e_c, row_c < far_end)], axis=1)
        lt_c = _dot_nt(k_aug, q_aug)
        m_old = m_ref[...]
        m_new = jnp.maximum(m_old, jnp.max(lt_c, axis=0, keepdims=True))
        alpha = jnp.exp(m_old - m_new)
        p_c = jnp.exp(lt_c - m_new).astype(BF16)
        vt_c = jnp.concatenate([vs_ref[c * tiles + u] for u in range(tiles)], axis=1)
        m_ref[...] = m_new
        acc_ref[...] = alpha * acc_ref[...] + _dot(vt_c, p_c)
        return carry

    lax.fori_loop(0, (i + 2) // 4, far_body, 0)
    acc_s = acc_ref[...]
    o_s_t = acc_s[0:hd] * (1.0 / jnp.maximum(acc_s[hd:hd + 1], 1e-30))

    nwin = WINDOW // qb + 1
    wtiles = [jnp.maximum(i - (nwin - 1 - tt), 0) for tt in range(nwin)]
    k_win = jnp.concatenate([kw_ref[pl.ds(pl.multiple_of(wt * qb, qb), qb), :] for wt in wtiles], axis=0)
    ltw = _dot_nt(k_win, qs) + bw_ref[...]
    wrow = lax.broadcasted_iota(jnp.int32, ltw.shape, 0)
    ltw = jnp.where(wrow >= (nwin - 1 - i) * qb, ltw, MASKED)
    mw = jnp.max(ltw, axis=0, keepdims=True)
    pw = jnp.exp(ltw - mw).astype(BF16)
    acc_w = _dot(jnp.concatenate([vw_ref[wt] for wt in wtiles], axis=1), pw)
    o_w_t = acc_w[0:hd] * (1.0 / jnp.maximum(acc_w[hd:hd + 1], 1e-30))

    gate = _sigmoid(ng_ref[...].astype(F32))
    gate_t = gate.T
    for r in range(NSA_HPG):
        rs = slice(r * qb, (r + 1) * qb)
        o_sw_t = gate_t[3 * r + 1:3 * r + 2, :] * o_s_t[:, rs] + gate_t[3 * r + 2:3 * r + 3, :] * o_w_t[:, rs]
        o = gate[:, 3 * r:3 * r + 1] * o_c[rs] + o_sw_t.T
        o_ref[:, r * hd:(r + 1) * hd] = o.astype(o_ref.dtype)


def nsa_call(p, qn, ksn, kwn, vst, vwt, kcp, vcp, bcn, bsl_t, bwn_t, ovl, e_tab, batch):
    t = p.shape[0]
    seq = t // batch
    nq = seq // Q_BLOCK
    g_ = NSA_KV
    kcp_rows = kcp.shape[1]
    rows = NSA_HPG * Q_BLOCK
    ng_blk = NG0 // LANES
    seq_spec = pl.BlockSpec((seq, HEAD_DIM), lambda b, g, i: (b, g))
    vt_spec = pl.BlockSpec((None, nq, VT_ROWS, Q_BLOCK), lambda b, g, i: (b * g_ + g, 0, 0, 0))
    cmp_spec = pl.BlockSpec((None, kcp_rows, HEAD_DIM), lambda b, g, i: (b * g_ + g, 0, 0))
    tbl_spec = lambda arr: pl.BlockSpec((None,) + arr.shape[1:], lambda b, g, i: (g, 0, 0))
    return pl.pallas_call(
        _nsa_kernel,
        grid=(batch, g_, nq),
        in_specs=[
            pl.BlockSpec((Q_BLOCK, rows), lambda b, g, i: (b * nq + i, g)),
            cmp_spec, cmp_spec,
            seq_spec, vt_spec, seq_spec, vt_spec,
            pl.BlockSpec((Q_BLOCK, LANES), lambda b, g, i: (b * nq + i, ng_blk + g)),
            tbl_spec(bcn), tbl_spec(bsl_t), tbl_spec(bwn_t),
            pl.BlockSpec(ovl.shape, lambda b, g, i: (0, 0)),
            pl.BlockSpec(e_tab.shape, lambda b, g, i: (0, 0)),
        ],
        out_specs=pl.BlockSpec((Q_BLOCK, rows), lambda b, g, i: (b * nq + i, g)),
        out_shape=jax.ShapeDtypeStruct((t, MIX_W), BF16),
        scratch_shapes=[pltpu.VMEM((1, rows), F32), pltpu.VMEM((VT_ROWS, rows), F32)],
        compiler_params=_cparams("parallel", "parallel", "arbitrary"),
        name="nsa_attention",
    )(qn, kcp, vcp, ksn, vst, kwn, vwt, p, bcn, bsl_t, bwn_t, ovl, e_tab)


def _t5_bucket_np(dist):
    n = np.maximum(dist, 0)
    max_exact = REL_BUCKETS // 2
    nf = np.maximum(n, 1).astype(np.float32)
    large = max_exact + (np.log(nf / max_exact) / math.log(REL_MAX_DIST / max_exact)
                         * (REL_BUCKETS - max_exact)).astype(np.int32)
    large = np.minimum(large, REL_BUCKETS - 1)
    return np.where(n < max_exact, n, large)


def _bias_tables(rel_bias, seq):
    qb = Q_BLOCK
    iq = np.arange(qb)[:, None]

    def tile(dist, visible):
        b = rel_bias[_t5_bucket_np(dist)] - rel_bias[REL_BUCKETS - 1][None, None, :]
        b = jnp.where(visible[:, :, None], b, MASKED)
        w = dist.shape[1]
        return b.transpose(2, 0, 1).reshape(NSA_KV, NSA_HPG * qb, w)

    jk = np.arange(qb)[None, :]
    d_diag = iq - jk
    d_prev = qb + iq - jk
    d_edge = WINDOW + iq - jk
    d_far = np.full((qb, qb), 2 * qb)
    d_sel = np.concatenate([d_prev, d_diag], axis=1)
    d_win = np.concatenate([d_edge] + [d_far] * (WINDOW // qb - 2) + [d_prev, d_diag], axis=1)
    d_cmp = iq - CMP_STRIDE * (jk - (KC_PAD - 8)) - (CMP_LEN - 1)
    bsl_t = tile(d_sel, d_sel >= 0).transpose(0, 2, 1)
    bwn_t = tile(d_win, (d_win >= 0) & (d_win < WINDOW)).transpose(0, 2, 1)
    bcn = tile(d_cmp, d_cmp >= 0)
    nc = seq // CMP_STRIDE
    n_slc = seq // SLC_LEN
    c0 = (np.arange(nc) * CMP_STRIDE)[:, None]
    s0 = (np.arange(n_slc) * SLC_LEN)[None, :]
    ov = ((c0 < s0 + SLC_LEN) & (c0 + CMP_LEN > s0) & (np.arange(nc)[:, None] < nc - 1)).astype(np.float32)
    ovl = np.concatenate([np.zeros((KC_PAD, n_slc), np.float32), ov], axis=0)
    e_tab = (np.arange(seq)[:, None] // SLC_LEN == np.arange(n_slc)[None, :]).astype(np.float32)
    return bcn, bsl_t, bwn_t, jnp.asarray(ovl), jnp.asarray(e_tab, dtype=BF16)


def _merge_kernel(g_ref, a_ref, b_ref, c_ref, d_ref, wgu_ref, wbr_ref, o_ref):
    gin = g_ref[...]
    acc = None
    for bi, br in enumerate((a_ref, b_ref, c_ref, d_ref)):
        gate = _sigmoid(_dot(gin, wgu_ref[bi]))
        term = gate * _dot(br[...], wbr_ref[bi])
        acc = term if acc is None else acc + term
    o_ref[...] = acc.astype(o_ref.dtype)


def merge_call(p, branches, wgu, wbr, tm=1024, tn=512):
    t = p.shape[0]
    d = wgu.shape[2]
    row = lambda w, cb: pl.BlockSpec((tm, w), lambda i, j: (i, cb))
    return pl.pallas_call(
        _merge_kernel,
        grid=(t // tm, d // tn),
        in_specs=[row(GATE_RANK, G0 // GATE_RANK)] + [row(MIX_W, 0)] * 4 + [
            pl.BlockSpec((4, GATE_RANK, tn), lambda i, j: (0, 0, j)),
            pl.BlockSpec((4, MIX_W, tn), lambda i, j: (0, 0, j)),
        ],
        out_specs=pl.BlockSpec((tm, tn), lambda i, j: (i, j)),
        out_shape=jax.ShapeDtypeStruct((t, d), BF16),
        compiler_params=_cparams("parallel", "arbitrary"),
        name="merge",
    )(p, *branches, wgu, wbr)


def _rearranged_w_in(w_in):
    o_a, o_q, o_kv, o_ng, o_c, o_d, o_g = 0, 1024, 2048, 3584, 3608, 5656, 7704
    l, d, _ = w_in.shape
    hpg3 = NSA_HPG * 3
    zpad = lambda n: jnp.zeros((l, d, n), w_in.dtype)
    parts = [
        w_in[:, :, o_c:o_c + 2048], w_in[:, :, o_d:o_d + 2048], w_in[:, :, o_a:o_a + 1024],
        w_in[:, :, o_q:o_q + 1024], w_in[:, :, o_kv:o_kv + 1536], w_in[:, :, o_g:o_g + 512],
        w_in[:, :, o_ng:o_ng + hpg3], zpad(LANES - hpg3),
        w_in[:, :, o_ng + hpg3:o_ng + 2 * hpg3], zpad(LANES - hpg3), zpad(PCOLS - NG0 - 2 * LANES),
    ]
    return jnp.concatenate(parts, axis=-1).astype(BF16)


def kernel(x, rel_bias, norm_mix_g, norm_ffn_g, w_in, pool_w, pool_scale, q_norm_g, k_norm_g, cmp_pos_k, cmp_w1_k, cmp_w2_k, cmp_pos_v, cmp_w1_v, cmp_w2_v, gmlp_ln_g, gmlp_ln_b, gmlp_ws, gmlp_bs, conv_w, conv_b, conv_ln_g, conv_ln_b, w_branch, w_gate_up, w_out, ffn_w_gate, ffn_w_up, ffn_w_down, moe_router, moe_router_b, moe_w_gate, moe_w_up, moe_w_down):
    batch, seq, d = x.shape
    t = batch * seq
    depth = w_in.shape[0]
    nc = seq // CMP_STRIDE
    half = CMP_STRIDE * HEAD_DIM
    xf = x.reshape(t, d)

    w_in_r = _rearranged_w_in(w_in)
    bcn, bsl_t, bwn_t, ovl, e_tab = _bias_tables(rel_bias, seq)
    row = lambda v: v.reshape(1, -1)

    for l in range(depth):
        h = rmsnorm_call(xf, row(norm_mix_g[l]))
        p = matmul_call(h, w_in_r[l], name="in_proj")
        o_a = pool_call(p, pool_w[l].astype(BF16), row(pool_scale[l]), batch)
        qn, ksn, kwn, vst, vwt = nsa_prep_call(p, row(q_norm_g[l]), row(k_norm_g[l]), batch)
        kvc = p[:, KV0:KV0 + 2 * NSA_KV * HEAD_DIM].reshape(batch, nc, CMP_STRIDE, 2, NSA_KV, HEAD_DIM)
        kvc = kvc.transpose(3, 0, 4, 1, 2, 5).reshape(2, batch * NSA_KV, nc, half)
        kcp, vcp = compress_call(
            kvc[0], kvc[1], cmp_pos_k[l].reshape(2, half), cmp_pos_v[l].reshape(2, half),
            cmp_w1_k[l].reshape(2, half, HEAD_DIM).astype(BF16), cmp_w2_k[l].astype(BF16),
            cmp_w1_v[l].reshape(2, half, HEAD_DIM).astype(BF16), cmp_w2_v[l].astype(BF16), row(k_norm_g[l]))
        o_b = nsa_call(p, qn, ksn, kwn, vst, vwt, kcp, vcp, bcn, bsl_t, bwn_t, ovl, e_tab, batch)
        bs_b = jnp.repeat(gmlp_bs[l].T, HEAD_DIM, axis=1)
        o_c = gmlp_call(p, row(gmlp_ln_g[l]), row(gmlp_ln_b[l]), gmlp_ws[l], bs_b)
        o_d = conv_call(p, conv_w[l], row(conv_b[l]), row(conv_ln_g[l]), row(conv_ln_b[l]), batch)
        mix = merge_call(p, (o_a, o_b, o_c, o_d), w_gate_up[l].astype(BF16), w_branch[l].astype(BF16))
        xf = matmul_call(mix, w_out[l].astype(BF16), res=xf, out_dtype=F32, name="out_proj")
        h = rmsnorm_call(xf, row(norm_ffn_g[l]))
        i = l // 2
        if l % 2 == 0:
            act = swiglu_call(h, ffn_w_gate[i][None].astype(BF16), ffn_w_up[i][None].astype(BF16))
            xf = matmul_call(act, ffn_w_down[i].astype(BF16), res=xf, out_dtype=F32, name="ffn_down")
        else:
            rw = jnp.pad(moe_router[i], ((0, 0), (0, LANES - N_EXPERTS))).astype(BF16)
            rb = jnp.pad(moe_router_b[i], (0, LANES - N_EXPERTS)).reshape(1, LANES)
            gate = router_call(h, rw, rb)
            act = swiglu_call(h, moe_w_gate[i].astype(BF16), moe_w_up[i].astype(BF16), gate=gate, tn=256)
            wd = moe_w_down[i].reshape(-1, d).astype(BF16)
            xf = matmul_call(act, wd, res=xf, out_dtype=F32, tk=3584, name="moe_down")
    return xf.reshape(batch, seq, d)
```

```python
import functools
import math

import jax
import jax.numpy as jnp
import numpy as np
from jax import lax
from jax.experimental import pallas as pl
from jax.experimental.pallas import tpu as pltpu

F32 = jnp.float32
BF16 = jnp.bfloat16

VMEM_LIMIT_BYTES = 56 * 1024 * 1024
LANES = 128

EPS = 1e-6
HEAD_DIM = 128
MIX_W = 1024
POOL_WINDOWS = (2, 4, 8, 16)
POOL_GW = MIX_W // len(POOL_WINDOWS)
NSA_HEADS = 8
NSA_KV = 2
NSA_HPG = NSA_HEADS // NSA_KV
CMP_LEN = 32
CMP_STRIDE = 16
SLC_LEN = 64
SLC_TOPK = 16
WINDOW = 512
Q_BLOCK = 128
FORCE_BONUS = 1e4
NEG = -1e9
MASKED = -1e30
GMLP_CHUNK = 128
CONV_W = 31
GATE_RANK = 512
REL_BUCKETS = 32
REL_MAX_DIST = 128
N_EXPERTS = 8
TOP_K = 2

C0 = 0
D0 = 2048
A0 = 4096
Q0 = 5120
KV0 = 6144
G0 = 7680
NG0 = 8192
PCOLS = 8704
KC_PAD = 128
VT_ROWS = HEAD_DIM + 16


def _cparams(*sem):
    return pltpu.CompilerParams(dimension_semantics=sem, vmem_limit_bytes=VMEM_LIMIT_BYTES)


def _dot(a, b):
    return jnp.dot(a, b, preferred_element_type=F32)


def _dot_nt(a, b):
    return lax.dot_general(a, b, (((1,), (1,)), ((), ())), preferred_element_type=F32)


def _gelu_tanh(x):
    return 0.5 * x * (1.0 + jnp.tanh(math.sqrt(2.0 / math.pi) * (x + 0.044715 * (x * x * x))))


def _sigmoid(x):
    return 1.0 / (1.0 + jnp.exp(-x))


def _rmsnorm_kernel(x_ref, g_ref, o_ref):
    xf = x_ref[...]
    r = lax.rsqrt(jnp.mean(xf * xf, axis=-1, keepdims=True) + EPS)
    o_ref[...] = ((xf * r) * g_ref[...]).astype(o_ref.dtype)


def rmsnorm_call(x, g, tr=256):
    t, d = x.shape
    return pl.pallas_call(
        _rmsnorm_kernel,
        grid=(t // tr,),
        in_specs=[pl.BlockSpec((tr, d), lambda i: (i, 0)), pl.BlockSpec((1, d), lambda i: (0, 0))],
        out_specs=pl.BlockSpec((tr, d), lambda i: (i, 0)),
        out_shape=jax.ShapeDtypeStruct((t, d), BF16),
        compiler_params=_cparams("parallel"),
        name="rmsnorm",
    )(x, g)


def _mm_kernel(*refs, nk, has_res):
    if has_res:
        a_ref, w_ref, r_ref, o_ref = refs[:4]
        scr = refs[4:]
    else:
        a_ref, w_ref, o_ref = refs[:3]
        r_ref = None
        scr = refs[3:]
    part = _dot(a_ref[...], w_ref[...])

    def finish(acc):
        if has_res:
            acc = acc + r_ref[...]
        o_ref[...] = acc.astype(o_ref.dtype)

    if nk == 1:
        finish(part)
    else:
        acc_ref = scr[0]
        k = pl.program_id(2)

        @pl.when(k == 0)
        def _():
            acc_ref[...] = part

        @pl.when(k > 0)
        def _():
            acc_ref[...] += part

        @pl.when(k == nk - 1)
        def _():
            finish(acc_ref[...])


def matmul_call(a, w, res=None, out_dtype=BF16, tm=1024, tn=512, tk=4096, name="matmul"):
    m, kdim = a.shape
    n = w.shape[1]
    tk = min(tk, kdim)
    nk = kdim // tk
    assert m % tm == 0 and n % tn == 0 and kdim % tk == 0
    in_specs = [pl.BlockSpec((tm, tk), lambda i, j, k: (i, k)), pl.BlockSpec((tk, tn), lambda i, j, k: (k, j))]
    args = [a, w]
    if res is not None:
        in_specs.append(pl.BlockSpec((tm, tn), lambda i, j, k: (i, j)))
        args.append(res)
    return pl.pallas_call(
        functools.partial(_mm_kernel, nk=nk, has_res=res is not None),
        grid=(m // tm, n // tn, nk),
        in_specs=in_specs,
        out_specs=pl.BlockSpec((tm, tn), lambda i, j, k: (i, j)),
        out_shape=jax.ShapeDtypeStruct((m, n), out_dtype),
        scratch_shapes=[pltpu.VMEM((tm, tn), F32)] if nk > 1 else [],
        compiler_params=_cparams("parallel", "parallel", "arbitrary"),
        name=name,
    )(*args)


def _swiglu_kernel(*refs, has_gate):
    if has_gate:
        h_ref, wg_ref, wu_ref, gate_ref, o_ref = refs
    else:
        h_ref, wg_ref, wu_ref, o_ref = refs
    h = h_ref[...]
    hg = _dot(h, wg_ref[...])
    hu = _dot(h, wu_ref[...])
    act = hg * _sigmoid(hg) * hu
    if has_gate:
        e = pl.program_id(1)
        gate = gate_ref[...]
        lane = lax.broadcasted_iota(jnp.int32, gate.shape, 1)
        act = act * jnp.sum(jnp.where(lane == e, gate, 0.0), axis=-1, keepdims=True)
    o_ref[...] = act.astype(o_ref.dtype)


def swiglu_call(h, wg, wu, gate=None, tm=1024, tn=512):
    t, d = h.shape
    ne, _, f = wg.shape
    assert f % tn == 0 and t % tm == 0
    nj = f // tn
    in_specs = [
        pl.BlockSpec((tm, d), lambda i, e, j: (i, 0)),
        pl.BlockSpec((None, d, tn), lambda i, e, j: (e, 0, j)),
        pl.BlockSpec((None, d, tn), lambda i, e, j: (e, 0, j)),
    ]
    args = [h, wg, wu]
    if gate is not None:
        in_specs.append(pl.BlockSpec((tm, LANES), lambda i, e, j: (i, 0)))
        args.append(gate)
    return pl.pallas_call(
        functools.partial(_swiglu_kernel, has_gate=gate is not None),
        grid=(t // tm, ne, nj),
        in_specs=in_specs,
        out_specs=pl.BlockSpec((tm, tn), lambda i, e, j: (i, e * nj + j)),
        out_shape=jax.ShapeDtypeStruct((t, ne * f), BF16),
        compiler_params=_cparams("parallel", "arbitrary", "arbitrary"),
        name="swiglu",
    )(*args)


def _router_kernel(h_ref, w_ref, b_ref, o_ref):
    logits = _dot(h_ref[...], w_ref[...]) + b_ref[...]
    lane = lax.broadcasted_iota(jnp.int32, logits.shape, 1)
    logits = jnp.where(lane < N_EXPERTS, logits, MASKED)
    v1 = jnp.max(logits, axis=-1, keepdims=True)
    i1 = jnp.min(jnp.where(logits == v1, lane, LANES), axis=-1, keepdims=True)
    rest = jnp.where(lane == i1, MASKED, logits)
    v2 = jnp.max(rest, axis=-1, keepdims=True)
    i2 = jnp.min(jnp.where(rest == v2, lane, LANES), axis=-1, keepdims=True)
    e2 = jnp.exp(v2 - v1)
    inv = 1.0 / (1.0 + e2)
    o_ref[...] = jnp.where(lane == i1, inv, jnp.where(lane == i2, e2 * inv, 0.0))


def router_call(h, w, b, tm=1024):
    t, d = h.shape
    return pl.pallas_call(
        _router_kernel,
        grid=(t // tm,),
        in_specs=[pl.BlockSpec((tm, d), lambda i: (i, 0)), pl.BlockSpec((d, LANES), lambda i: (0, 0)),
                  pl.BlockSpec((1, LANES), lambda i: (0, 0))],
        out_specs=pl.BlockSpec((tm, LANES), lambda i: (i, 0)),
        out_shape=jax.ShapeDtypeStruct((t, LANES), F32),
        compiler_params=_cparams("parallel"),
        name="router",
    )(h, w, b)


def _pool_kernel(cur_ref, halo_ref, w_ref, scale_ref, o_ref, xs_ref, *, ts, halo):
    i = pl.program_id(1)
    prev = halo_ref[...].astype(F32)
    xs_ref[0:halo, :] = jnp.where(i == 0, 0.0, prev)
    xs_ref[halo:halo + ts, :] = cur_ref[...].astype(F32)
    row = lax.broadcasted_iota(jnp.int32, (ts, POOL_GW), 0) + i * ts + 1
    for gi, w in enumerate(POOL_WINDOWS):
        c0 = gi * POOL_GW
        x = xs_ref[halo:halo + ts, c0:c0 + POOL_GW]
        wsum = x
        for k in range(1, w):
            wsum = wsum + xs_ref[halo - k:halo - k + ts, c0:c0 + POOL_GW]
        cnt = jnp.minimum(row, w).astype(F32)
        diff = (wsum / cnt - x).astype(BF16)
        y = _dot(diff, w_ref[gi]) * scale_ref[:, c0:c0 + POOL_GW]
        o_ref[:, c0:c0 + POOL_GW] = y.astype(o_ref.dtype)


def pool_call(p, w_pool, scale, batch, ts=512):
    t = p.shape[0]
    seq = t // batch
    nts = seq // ts
    halo = 16
    cb = A0 // MIX_W
    return pl.pallas_call(
        functools.partial(_pool_kernel, ts=ts, halo=halo),
        grid=(batch, nts),
        in_specs=[
            pl.BlockSpec((ts, MIX_W), lambda b, i: (b * nts + i, cb)),
            pl.BlockSpec((halo, MIX_W), lambda b, i: (jnp.maximum((b * nts + i) * (ts // halo) - 1, 0), cb)),
            pl.BlockSpec((len(POOL_WINDOWS), POOL_GW, POOL_GW), lambda b, i: (0, 0, 0)),
            pl.BlockSpec((1, MIX_W), lambda b, i: (0, 0)),
        ],
        out_specs=pl.BlockSpec((ts, MIX_W), lambda b, i: (b * nts + i, 0)),
        out_shape=jax.ShapeDtypeStruct((t, MIX_W), BF16),
        scratch_shapes=[pltpu.VMEM((halo + ts, MIX_W), F32)],
        compiler_params=_cparams("parallel", "arbitrary"),
        name="pool",
    )(p, p, w_pool, scale)


def _layernorm(v, g, b):
    mu = jnp.mean(v, axis=-1, keepdims=True)
    vc = v - mu
    var = jnp.mean(vc * vc, axis=-1, keepdims=True)
    return (vc * lax.rsqrt(var + EPS)) * g + b


def _gmlp_kernel(z_ref, g_ref, b_ref, ws_ref, bs_ref, o_ref, *, ts):
    z = _gelu_tanh(z_ref[...].astype(F32))
    u = z[:, :MIX_W]
    v = _layernorm(z[:, MIX_W:], g_ref[...], b_ref[...]).astype(BF16)
    ri = lax.broadcasted_iota(jnp.int32, (GMLP_CHUNK, GMLP_CHUNK), 0)
    ci = lax.broadcasted_iota(jnp.int32, (GMLP_CHUNK, GMLP_CHUNK), 1)
    tri = ci <= ri
    for h in range(MIX_W // HEAD_DIM):
        w = jnp.where(tri, ws_ref[h], 0.0).astype(BF16)
        bias = bs_ref[:, h * HEAD_DIM:(h + 1) * HEAD_DIM]
        for c in range(ts // GMLP_CHUNK):
            rows = slice(c * GMLP_CHUNK, (c + 1) * GMLP_CHUNK)
            cols = slice(h * HEAD_DIM, (h + 1) * HEAD_DIM)
            s = _dot(w, v[rows, cols]) + bias
            o_ref[rows, cols] = (u[rows, cols] * s).astype(o_ref.dtype)


def gmlp_call(p, ln_g, ln_b, ws, bs_b, ts=512):
    t = p.shape[0]
    return pl.pallas_call(
        functools.partial(_gmlp_kernel, ts=ts),
        grid=(t // ts,),
        in_specs=[
            pl.BlockSpec((ts, 2 * MIX_W), lambda i: (i, C0 // (2 * MIX_W))),
            pl.BlockSpec((1, MIX_W), lambda i: (0, 0)),
            pl.BlockSpec((1, MIX_W), lambda i: (0, 0)),
            pl.BlockSpec(ws.shape, lambda i: (0, 0, 0)),
            pl.BlockSpec(bs_b.shape, lambda i: (0, 0)),
        ],
        out_specs=pl.BlockSpec((ts, MIX_W), lambda i: (i, 0)),
        out_shape=jax.ShapeDtypeStruct((t, MIX_W), BF16),
        compiler_params=_cparams("parallel"),
        name="gmlp",
    )(p, ln_g, ln_b, ws, bs_b)


def _conv_kernel(cur_ref, halo_ref, w_ref, b_ref, g_ref, beta_ref, o_ref, hs_ref, *, ts, halo):
    i = pl.program_id(1)

    def glu(z):
        zf = z.astype(F32)
        return zf[:, :MIX_W] * _sigmoid(zf[:, MIX_W:])

    hs_ref[0:halo, :] = jnp.where(i == 0, 0.0, glu(halo_ref[...]))
    hs_ref[halo:halo + ts, :] = glu(cur_ref[...])
    off = halo - (CONV_W - 1)
    acc = jnp.zeros((ts, MIX_W), F32) + b_ref[...]
    for k in range(CONV_W):
        acc = acc + w_ref[k:k + 1, :] * hs_ref[off + k:off + k + ts, :]
    y = _layernorm(acc, g_ref[...], beta_ref[...])
    o_ref[...] = (y * _sigmoid(y)).astype(o_ref.dtype)


def conv_call(p, w, b, ln_g, ln_b, batch, ts=256):
    t = p.shape[0]
    seq = t // batch
    nts = seq // ts
    halo = 32
    cb = D0 // (2 * MIX_W)
    return pl.pallas_call(
        functools.partial(_conv_kernel, ts=ts, halo=halo),
        grid=(batch, nts),
        in_specs=[
            pl.BlockSpec((ts, 2 * MIX_W), lambda b_, i: (b_ * nts + i, cb)),
            pl.BlockSpec((halo, 2 * MIX_W), lambda b_, i: (jnp.maximum((b_ * nts + i) * (ts // halo) - 1, 0), cb)),
            pl.BlockSpec((CONV_W, MIX_W), lambda b_, i: (0, 0)),
            pl.BlockSpec((1, MIX_W), lambda b_, i: (0, 0)),
            pl.BlockSpec((1, MIX_W), lambda b_, i: (0, 0)),
            pl.BlockSpec((1, MIX_W), lambda b_, i: (0, 0)),
        ],
        out_specs=pl.BlockSpec((ts, MIX_W), lambda b_, i: (b_ * nts + i, 0)),
        out_shape=jax.ShapeDtypeStruct((t, MIX_W), BF16),
        scratch_shapes=[pltpu.VMEM((halo + ts, MIX_W), F32)],
        compiler_params=_cparams("parallel", "arbitrary"),
        name="conv",
    )(p, p, w, b, ln_g, ln_b)


def _head_rms(x, g):
    xf = x.astype(F32)
    r = lax.rsqrt(jnp.mean(xf * xf, axis=-1, keepdims=True) + EPS)
    return (xf * r) * g


def _nsa_prep_kernel(q_ref, ks_ref, vs_ref, kw_ref, vw_ref, qg_ref, kg_ref, qo_ref, kso_ref, kwo_ref, vso_ref, vwo_ref, *, tr):
    qg = qg_ref[...]
    kg = kg_ref[...]
    for h in range(NSA_HEADS):
        cols = slice(h * HEAD_DIM, (h + 1) * HEAD_DIM)
        qo_ref[:, cols] = (_head_rms(q_ref[:, cols], qg) * (HEAD_DIM ** -0.5)).astype(qo_ref.dtype)
    ones = jnp.ones((VT_ROWS - HEAD_DIM, Q_BLOCK), vso_ref.dtype)
    for g in range(NSA_KV):
        cols = slice(g * HEAD_DIM, (g + 1) * HEAD_DIM)
        kso_ref[:, cols] = _head_rms(ks_ref[:, cols], kg).astype(kso_ref.dtype)
        kwo_ref[:, cols] = _head_rms(kw_ref[:, cols], kg).astype(kwo_ref.dtype)
        for u in range(tr // Q_BLOCK):
            rows = slice(u * Q_BLOCK, (u + 1) * Q_BLOCK)
            for v_ref, vo_ref in ((vs_ref, vso_ref), (vw_ref, vwo_ref)):
                vo_ref[g, u, 0:HEAD_DIM, :] = v_ref[rows, cols].astype(F32).T.astype(vo_ref.dtype)
                vo_ref[g, u, HEAD_DIM:VT_ROWS, :] = ones


def nsa_prep_call(p, q_g, k_g, batch, tr=512):
    t = p.shape[0]
    seq = t // batch
    nst = seq // tr
    kvw = NSA_KV * HEAD_DIM
    kv_spec = lambda c: pl.BlockSpec((tr, kvw), lambda i: (i, (KV0 + c * kvw) // kvw))
    vt_sds = jax.ShapeDtypeStruct((batch * NSA_KV, seq // Q_BLOCK, VT_ROWS, Q_BLOCK), BF16)
    vt_spec = pl.BlockSpec((NSA_KV, tr // Q_BLOCK, VT_ROWS, Q_BLOCK), lambda i: (i // nst, i % nst, 0, 0))
    return pl.pallas_call(
        functools.partial(_nsa_prep_kernel, tr=tr),
        grid=(t // tr,),
        in_specs=[
            pl.BlockSpec((tr, MIX_W), lambda i: (i, Q0 // MIX_W)),
            kv_spec(2), kv_spec(3), kv_spec(4), kv_spec(5),
            pl.BlockSpec((1, HEAD_DIM), lambda i: (0, 0)),
            pl.BlockSpec((1, HEAD_DIM), lambda i: (0, 0)),
        ],
        out_specs=[
            pl.BlockSpec((tr, MIX_W), lambda i: (i, 0)),
            pl.BlockSpec((tr, kvw), lambda i: (i, 0)),
            pl.BlockSpec((tr, kvw), lambda i: (i, 0)),
            vt_spec, vt_spec,
        ],
        out_shape=[jax.ShapeDtypeStruct((t, MIX_W), BF16), jax.ShapeDtypeStruct((t, kvw), BF16),
                   jax.ShapeDtypeStruct((t, kvw), BF16), vt_sds, vt_sds],
        compiler_params=_cparams("parallel"),
        name="nsa_prep",
    )(p, p, p, p, p, q_g, k_g)


def _compress_kernel(ck_ref, cv_ref, pk_ref, pv_ref, w1k_ref, w2k_ref, w1v_ref, w2v_ref, kg_ref, ko_ref, vo_ref, *, nc):
    def comp(c_ref, p_ref, w1_ref, w2_ref):
        c = c_ref[...].astype(F32)
        xa = (c + p_ref[0:1, :]).astype(BF16)
        xb = (c + p_ref[1:2, :]).astype(BF16)
        first = _dot(xa, w1_ref[0])
        second = _dot(xb, w1_ref[1])
        hdn = _gelu_tanh(first + pltpu.roll(second, nc - 1, 0))
        return _dot(hdn.astype(BF16), w2_ref[...])

    row = lax.broadcasted_iota(jnp.int32, (nc, HEAD_DIM), 0)
    real = row < nc - 1
    kc = _head_rms(comp(ck_ref, pk_ref, w1k_ref, w2k_ref), kg_ref[...])
    vc = comp(cv_ref, pv_ref, w1v_ref, w2v_ref)
    zeros = jnp.zeros((KC_PAD, HEAD_DIM), F32)
    ko_ref[0:KC_PAD, :] = zeros
    vo_ref[0:KC_PAD, :] = zeros
    ko_ref[KC_PAD:KC_PAD + nc, :] = jnp.where(real, kc, 0.0)
    vo_ref[KC_PAD:KC_PAD + nc, :] = jnp.where(real, vc, 0.0)


def compress_call(ck, cv, pos_k, pos_v, w1k, w2k, w1v, w2v, k_g):
    bg, nc, half = ck.shape
    full2 = lambda shape: pl.BlockSpec(shape, lambda i: (0,) * len(shape))
    out_sds = jax.ShapeDtypeStruct((bg, KC_PAD + nc, HEAD_DIM), F32)
    return pl.pallas_call(
        functools.partial(_compress_kernel, nc=nc),
        grid=(bg,),
        in_specs=[
            pl.BlockSpec((None, nc, half), lambda i: (i, 0, 0)),
            pl.BlockSpec((None, nc, half), lambda i: (i, 0, 0)),
            full2((2, half)), full2((2, half)),
            full2((2, half, HEAD_DIM)), full2((HEAD_DIM, HEAD_DIM)),
            full2((2, half, HEAD_DIM)), full2((HEAD_DIM, HEAD_DIM)),
            full2((1, HEAD_DIM)),
        ],
        out_specs=[pl.BlockSpec((None, KC_PAD + nc, HEAD_DIM), lambda i: (i, 0, 0))] * 2,
        out_shape=[out_sds, out_sds],
        compiler_params=_cparams("parallel"),
        name="nsa_compress",
    )(ck, cv, pos_k, pos_v, w1k, w2k, w1v, w2v, k_g)


def _softmax_rows(parts):
    m = parts[0].max(axis=-1, keepdims=True)
    for x in parts[1:]:
        m = jnp.maximum(m, x.max(axis=-1, keepdims=True))
    m = jnp.maximum(m, -1e20)
    ps = [jnp.exp(x - m) for x in parts]
    s = ps[0].sum(axis=-1, keepdims=True)
    for p in ps[1:]:
        s = s + p.sum(axis=-1, keepdims=True)
    inv = 1.0 / jnp.maximum(s, 1e-30)
    return ps, inv


def _split_hi_lo(x):
    hi = x.astype(BF16)
    lo = (x - hi.astype(F32)).astype(BF16)
    return hi, lo


def _nsa_kernel(q_ref, kc_ref, vc_ref, ks_ref, vs_ref, kw_ref, vw_ref, ng_ref,
                bcn_ref, bs_ref, bw_ref, ov_ref, e_ref, o_ref, m_ref, acc_ref, ow_ref):
    i = pl.program_id(2)
    qb = Q_BLOCK
    rows = NSA_HPG * qb
    hd = HEAD_DIM
    qs = jnp.concatenate([q_ref[:, r * hd:(r + 1) * hd] for r in range(NSA_HPG)], axis=0)
    kcp = kc_ref.shape[0]

    nwin = WINDOW // qb + 1
    wtiles = [jnp.maximum(i - (nwin - 1 - tt), 0) for tt in range(nwin)]
    k_win = jnp.concatenate([kw_ref[pl.ds(pl.multiple_of(wt * qb, qb), qb), :] for wt in wtiles], axis=0)
    ltw = _dot_nt(k_win, qs) + bw_ref[...]
    wrow = lax.broadcasted_iota(jnp.int32, ltw.shape, 0)
    ltw = jnp.where(wrow >= (nwin - 1 - i) * qb, ltw, MASKED)
    mw = jnp.max(ltw, axis=0, keepdims=True)
    pw = jnp.exp(ltw - mw).astype(BF16)
    acc_w = _dot(jnp.concatenate([vw_ref[wt] for wt in wtiles], axis=1), pw)
    ow_ref[...] = acc_w[0:hd] * (1.0 / jnp.maximum(acc_w[hd:hd + 1], 1e-30))

    near0 = pl.multiple_of(8 * i + 8, 8)
    lf = _dot_nt(qs, kc_ref[...].astype(BF16))
    npad = lax.broadcasted_iota(jnp.int32, (rows, kcp), 1)
    lf = jnp.where((npad >= KC_PAD) & (npad < near0), lf, MASKED)
    ln = _dot_nt(qs, kc_ref[pl.ds(near0, qb), :].astype(BF16)) + bcn_ref[...]
    ncol = lax.broadcasted_iota(jnp.int32, (rows, qb), 1)
    ln = jnp.where(ncol >= KC_PAD - 8 - 8 * i, ln, MASKED)
    (pf, pn), inv = _softmax_rows([lf, ln])
    pf = pf * inv
    pn = pn * inv
    o_c = _dot(pf.astype(BF16), vc_ref[...].astype(BF16)) + _dot(pn.astype(BF16), vc_ref[pl.ds(near0, qb), :].astype(BF16))

    pf_g = pf[0:qb]
    pn_g = pn[0:qb]
    for r in range(1, NSA_HPG):
        pf_g = pf_g + pf[r * qb:(r + 1) * qb]
        pn_g = pn_g + pn[r * qb:(r + 1) * qb]
    ov_all = ov_ref[...].astype(BF16)
    ov_near = ov_ref[pl.ds(near0, qb), :].astype(BF16)
    imp = jnp.zeros((qb, ov_ref.shape[1]), F32)
    for part in _split_hi_lo(pf_g):
        imp = imp + _dot(part, ov_all)
    for part in _split_hi_lo(pn_g):
        imp = imp + _dot(part, ov_near)
    nblk = imp.shape[1]
    jj = lax.broadcasted_iota(jnp.int32, (qb, nblk), 1)
    tq = lax.broadcasted_iota(jnp.int32, (qb, nblk), 0) + i * qb
    cur = jnp.right_shift(tq, 6)
    forced = (jj == 0) | (jj == cur) | (jj == cur - 1)
    score = jnp.where(jj <= cur, imp + jnp.where(forced, FORCE_BONUS, 0.0), NEG)
    sc = score.T
    jrow = lax.broadcasted_iota(jnp.int32, sc.shape, 0).astype(F32)
    sel_t = jnp.zeros(sc.shape, F32)
    for _ in range(min(SLC_TOPK, nblk)):
        mx = jnp.max(sc, axis=0, keepdims=True)
        first = jnp.min(jnp.where(sc == mx, jrow, float(nblk)), axis=0, keepdims=True)
        pick = jrow == first
        sel_t = jnp.where(pick, 1.0, sel_t)
        sc = jnp.where(pick, -3e38, sc)
    sel = sel_t.T

    sel_neg = jnp.where(sel > 0.5, 0.0, MASKED).astype(BF16)
    q_aug = jnp.concatenate([qs, jnp.concatenate([sel_neg] * NSA_HPG, axis=0)], axis=1)

    def block_onehot(e_rows, valid):
        return jnp.where(valid, e_rows, jnp.ones_like(e_rows))

    prev_t = jnp.maximum(i - 1, 0)
    prev0 = pl.multiple_of(prev_t * qb, qb)
    diag0 = pl.multiple_of(i * qb, qb)
    k_near = jnp.concatenate([ks_ref[pl.ds(prev0, qb), :], ks_ref[pl.ds(diag0, qb), :]], axis=0)
    e_near = jnp.concatenate([e_ref[pl.ds(prev0, qb), :], e_ref[pl.ds(diag0, qb), :]], axis=0)
    row_near = lax.broadcasted_iota(jnp.int32, e_near.shape, 0)
    e_near = block_onehot(e_near, row_near >= jnp.where(i == 0, qb, 0))
    lt = _dot_nt(jnp.concatenate([k_near, e_near], axis=1), q_aug) + bs_ref[...]
    m0 = jnp.maximum(jnp.max(lt, axis=0, keepdims=True), -1e20)
    p0 = jnp.exp(lt - m0).astype(BF16)
    vt_near = jnp.concatenate([vs_ref[prev_t], vs_ref[i]], axis=1)
    m_ref[...] = m0
    acc_ref[...] = _dot(vt_near, p0)

    far_end = (i - 1) * qb
    tiles = 4
    chunk = tiles * qb
    halves = 2

    def far_body(c, carry):
        lts, vts = [], []
        for hf in range(halves):
            t0 = (c * halves + hf) * tiles
            c0 = pl.multiple_of(t0 * qb, chunk)
            e_c = e_ref[pl.ds(c0, chunk), :]
            row_c = lax.broadcasted_iota(jnp.int32, e_c.shape, 0) + c0
            k_aug = jnp.concatenate([ks_ref[pl.ds(c0, chunk), :], block_onehot(e_c, row_c < far_end)], axis=1)
            lts.append(_dot_nt(k_aug, q_aug))
            vts.append(jnp.concatenate([vs_ref[t0 + u] for u in range(tiles)], axis=1))
        m_old = m_ref[...]
        m_new = m_old
        for lt_c in lts:
            m_new = jnp.maximum(m_new, jnp.max(lt_c, axis=0, keepdims=True))
        acc = jnp.exp(m_old - m_new) * acc_ref[...]
        for lt_c, vt_c in zip(lts, vts):
            acc = acc + _dot(vt_c, jnp.exp(lt_c - m_new).astype(BF16))
        m_ref[...] = m_new
        acc_ref[...] = acc
        return carry

    lax.fori_loop(0, (i - 1 + halves * tiles - 1) // (halves * tiles), far_body, 0)
    acc_s = acc_ref[...]
    o_s_t = acc_s[0:hd] * (1.0 / jnp.maximum(acc_s[hd:hd + 1], 1e-30))
    o_w_t = ow_ref[...]

    gate = _sigmoid(ng_ref[...].astype(F32))
    gate_t = gate.T
    for r in range(NSA_HPG):
        rs = slice(r * qb, (r + 1) * qb)
        o_sw_t = gate_t[3 * r + 1:3 * r + 2, :] * o_s_t[:, rs] + gate_t[3 * r + 2:3 * r + 3, :] * o_w_t[:, rs]
        o = gate[:, 3 * r:3 * r + 1] * o_c[rs] + o_sw_t.T
        o_ref[:, r * hd:(r + 1) * hd] = o.astype(o_ref.dtype)


def nsa_call(p, qn, ksn, kwn, vst, vwt, kcp, vcp, bcn, bsl_t, bwn_t, ovl, e_tab, batch):
    t = p.shape[0]
    seq = t // batch
    nq = seq // Q_BLOCK
    assert nq % 8 == 0
    g_ = NSA_KV
    kcp_rows = kcp.shape[1]
    rows = NSA_HPG * Q_BLOCK
    ng_blk = NG0 // LANES
    seq_spec = pl.BlockSpec((seq, HEAD_DIM), lambda b, g, i: (b, g))
    vt_spec = pl.BlockSpec((None, nq, VT_ROWS, Q_BLOCK), lambda b, g, i: (b * g_ + g, 0, 0, 0))
    cmp_spec = pl.BlockSpec((None, kcp_rows, HEAD_DIM), lambda b, g, i: (b * g_ + g, 0, 0))
    tbl_spec = lambda arr: pl.BlockSpec((None,) + arr.shape[1:], lambda b, g, i: (g, 0, 0))
    return pl.pallas_call(
        _nsa_kernel,
        grid=(batch, g_, nq),
        in_specs=[
            pl.BlockSpec((Q_BLOCK, rows), lambda b, g, i: (b * nq + i, g)),
            cmp_spec, cmp_spec,
            seq_spec, vt_spec, seq_spec, vt_spec,
            pl.BlockSpec((Q_BLOCK, LANES), lambda b, g, i: (b * nq + i, ng_blk + g)),
            tbl_spec(bcn), tbl_spec(bsl_t), tbl_spec(bwn_t),
            pl.BlockSpec(ovl.shape, lambda b, g, i: (0, 0)),
            pl.BlockSpec(e_tab.shape, lambda b, g, i: (0, 0)),
        ],
        out_specs=pl.BlockSpec((Q_BLOCK, rows), lambda b, g, i: (b * nq + i, g)),
        out_shape=jax.ShapeDtypeStruct((t, MIX_W), BF16),
        scratch_shapes=[pltpu.VMEM((1, rows), F32), pltpu.VMEM((VT_ROWS, rows), F32), pltpu.VMEM((HEAD_DIM, rows), F32)],
        compiler_params=_cparams("parallel", "parallel", "arbitrary"),
        name="nsa_attention",
    )(qn, kcp, vcp, ksn, vst, kwn, vwt, p, bcn, bsl_t, bwn_t, ovl, e_tab)


def _t5_bucket_np(dist):
    n = np.maximum(dist, 0)
    max_exact = REL_BUCKETS // 2
    nf = np.maximum(n, 1).astype(np.float32)
    large = max_exact + (np.log(nf / max_exact) / math.log(REL_MAX_DIST / max_exact)
                         * (REL_BUCKETS - max_exact)).astype(np.int32)
    large = np.minimum(large, REL_BUCKETS - 1)
    return np.where(n < max_exact, n, large)


def _bias_tables(rel_bias, seq):
    qb = Q_BLOCK
    iq = np.arange(qb)[:, None]

    def tile(dist, visible):
        b = rel_bias[_t5_bucket_np(dist)] - rel_bias[REL_BUCKETS - 1][None, None, :]
        b = jnp.where(visible[:, :, None], b, MASKED)
        w = dist.shape[1]
        return b.transpose(2, 0, 1).reshape(NSA_KV, NSA_HPG * qb, w)

    jk = np.arange(qb)[None, :]
    d_diag = iq - jk
    d_prev = qb + iq - jk
    d_edge = WINDOW + iq - jk
    d_far = np.full((qb, qb), 2 * qb)
    d_sel = np.concatenate([d_prev, d_diag], axis=1)
    d_win = np.concatenate([d_edge] + [d_far] * (WINDOW // qb - 2) + [d_prev, d_diag], axis=1)
    d_cmp = iq - CMP_STRIDE * (jk - (KC_PAD - 8)) - (CMP_LEN - 1)
    bsl_t = tile(d_sel, d_sel >= 0).transpose(0, 2, 1)
    bwn_t = tile(d_win, (d_win >= 0) & (d_win < WINDOW)).transpose(0, 2, 1)
    bcn = tile(d_cmp, d_cmp >= 0)
    nc = seq // CMP_STRIDE
    n_slc = seq // SLC_LEN
    c0 = (np.arange(nc) * CMP_STRIDE)[:, None]
    s0 = (np.arange(n_slc) * SLC_LEN)[None, :]
    ov = ((c0 < s0 + SLC_LEN) & (c0 + CMP_LEN > s0) & (np.arange(nc)[:, None] < nc - 1)).astype(np.float32)
    ovl = np.concatenate([np.zeros((KC_PAD, n_slc), np.float32), ov], axis=0)
    e_tab = (np.arange(seq)[:, None] // SLC_LEN == np.arange(n_slc)[None, :]).astype(np.float32)
    return bcn, bsl_t, bwn_t, jnp.asarray(ovl), jnp.asarray(e_tab, dtype=BF16)


def _merge_kernel(g_ref, a_ref, b_ref, c_ref, d_ref, wgu_ref, wbr_ref, o_ref):
    gin = g_ref[...]
    acc = None
    for bi, br in enumerate((a_ref, b_ref, c_ref, d_ref)):
        gate = _sigmoid(_dot(gin, wgu_ref[bi]))
        term = gate * _dot(br[...], wbr_ref[bi])
        acc = term if acc is None else acc + term
    o_ref[...] = acc.astype(o_ref.dtype)


def merge_call(p, branches, wgu, wbr, tm=1024, tn=512):
    t = p.shape[0]
    d = wgu.shape[2]
    row = lambda w, cb: pl.BlockSpec((tm, w), lambda i, j: (i, cb))
    return pl.pallas_call(
        _merge_kernel,
        grid=(t // tm, d // tn),
        in_specs=[row(GATE_RANK, G0 // GATE_RANK)] + [row(MIX_W, 0)] * 4 + [
            pl.BlockSpec((4, GATE_RANK, tn), lambda i, j: (0, 0, j)),
            pl.BlockSpec((4, MIX_W, tn), lambda i, j: (0, 0, j)),
        ],
        out_specs=pl.BlockSpec((tm, tn), lambda i, j: (i, j)),
        out_shape=jax.ShapeDtypeStruct((t, d), BF16),
        compiler_params=_cparams("parallel", "arbitrary"),
        name="merge",
    )(p, *branches, wgu, wbr)


def _rearranged_w_in(w_in):
    o_a, o_q, o_kv, o_ng, o_c, o_d, o_g = 0, 1024, 2048, 3584, 3608, 5656, 7704
    l, d, _ = w_in.shape
    hpg3 = NSA_HPG * 3
    zpad = lambda n: jnp.zeros((l, d, n), w_in.dtype)
    parts = [
        w_in[:, :, o_c:o_c + 2048], w_in[:, :, o_d:o_d + 2048], w_in[:, :, o_a:o_a + 1024],
        w_in[:, :, o_q:o_q + 1024], w_in[:, :, o_kv:o_kv + 1536], w_in[:, :, o_g:o_g + 512],
        w_in[:, :, o_ng:o_ng + hpg3], zpad(LANES - hpg3),
        w_in[:, :, o_ng + hpg3:o_ng + 2 * hpg3], zpad(LANES - hpg3), zpad(PCOLS - NG0 - 2 * LANES),
    ]
    return jnp.concatenate(parts, axis=-1).astype(BF16)


def kernel(x, rel_bias, norm_mix_g, norm_ffn_g, w_in, pool_w, pool_scale, q_norm_g, k_norm_g, cmp_pos_k, cmp_w1_k, cmp_w2_k, cmp_pos_v, cmp_w1_v, cmp_w2_v, gmlp_ln_g, gmlp_ln_b, gmlp_ws, gmlp_bs, conv_w, conv_b, conv_ln_g, conv_ln_b, w_branch, w_gate_up, w_out, ffn_w_gate, ffn_w_up, ffn_w_down, moe_router, moe_router_b, moe_w_gate, moe_w_up, moe_w_down):
    batch, seq, d = x.shape
    t = batch * seq
    depth = w_in.shape[0]
    nc = seq // CMP_STRIDE
    half = CMP_STRIDE * HEAD_DIM
    xf = x.reshape(t, d)

    w_in_r = _rearranged_w_in(w_in)
    bcn, bsl_t, bwn_t, ovl, e_tab = _bias_tables(rel_bias, seq)
    row = lambda v: v.reshape(1, -1)

    for l in range(depth):
        h = rmsnorm_call(xf, row(norm_mix_g[l]))
        p = matmul_call(h, w_in_r[l], name="in_proj")
        o_a = pool_call(p, pool_w[l].astype(BF16), row(pool_scale[l]), batch)
        qn, ksn, kwn, vst, vwt = nsa_prep_call(p, row(q_norm_g[l]), row(k_norm_g[l]), batch)
        kvc = p[:, KV0:KV0 + 2 * NSA_KV * HEAD_DIM].reshape(batch, nc, CMP_STRIDE, 2, NSA_KV, HEAD_DIM)
        kvc = kvc.transpose(3, 0, 4, 1, 2, 5).reshape(2, batch * NSA_KV, nc, half)
        kcp, vcp = compress_call(
            kvc[0], kvc[1], cmp_pos_k[l].reshape(2, half), cmp_pos_v[l].reshape(2, half),
            cmp_w1_k[l].reshape(2, half, HEAD_DIM).astype(BF16), cmp_w2_k[l].astype(BF16),
            cmp_w1_v[l].reshape(2, half, HEAD_DIM).astype(BF16), cmp_w2_v[l].astype(BF16), row(k_norm_g[l]))
        o_b = nsa_call(p, qn, ksn, kwn, vst, vwt, kcp, vcp, bcn, bsl_t, bwn_t, ovl, e_tab, batch)
        bs_b = jnp.repeat(gmlp_bs[l].T, HEAD_DIM, axis=1)
        o_c = gmlp_call(p, row(gmlp_ln_g[l]), row(gmlp_ln_b[l]), gmlp_ws[l], bs_b)
        o_d = conv_call(p, conv_w[l], row(conv_b[l]), row(conv_ln_g[l]), row(conv_ln_b[l]), batch)
        mix = merge_call(p, (o_a, o_b, o_c, o_d), w_gate_up[l].astype(BF16), w_branch[l].astype(BF16))
        xf = matmul_call(mix, w_out[l].astype(BF16), res=xf, out_dtype=F32, name="out_proj")
        h = rmsnorm_call(xf, row(norm_ffn_g[l]))
        i = l // 2
        if l % 2 == 0:
            act = swiglu_call(h, ffn_w_gate[i][None].astype(BF16), ffn_w_up[i][None].astype(BF16))
            xf = matmul_call(act, ffn_w_down[i].astype(BF16), res=xf, out_dtype=F32, name="ffn_down")
        else:
            rw = jnp.pad(moe_router[i], ((0, 0), (0, LANES - N_EXPERTS))).astype(BF16)
            rb = jnp.pad(moe_router_b[i], (0, LANES - N_EXPERTS)).reshape(1, LANES)
            gate = router_call(h, rw, rb)
            act = swiglu_call(h, moe_w_gate[i].astype(BF16), moe_w_up[i].astype(BF16), gate=gate, tn=256)
            wd = moe_w_down[i].reshape(-1, d).astype(BF16)
            xf = matmul_call(act, wd, res=xf, out_dtype=F32, tk=3584, name="moe_down")
    return xf.reshape(batch, seq, d)
```

```python
import functools
import math

import jax
import jax.numpy as jnp
import numpy as np
from jax import lax
from jax.experimental import pallas as pl
from jax.experimental.pallas import tpu as pltpu

F32 = jnp.float32
BF16 = jnp.bfloat16

VMEM_LIMIT_BYTES = 56 * 1024 * 1024
LANES = 128

EPS = 1e-6
HEAD_DIM = 128
MIX_W = 1024
POOL_WINDOWS = (2, 4, 8, 16)
POOL_GW = MIX_W // len(POOL_WINDOWS)
NSA_HEADS = 8
NSA_KV = 2
NSA_HPG = NSA_HEADS // NSA_KV
CMP_LEN = 32
CMP_STRIDE = 16
SLC_LEN = 64
SLC_TOPK = 16
WINDOW = 512
Q_BLOCK = 128
FORCE_BONUS = 1e4
NEG = -1e9
MASKED = -1e30
GMLP_CHUNK = 128
CONV_W = 31
GATE_RANK = 512
REL_BUCKETS = 32
REL_MAX_DIST = 128
N_EXPERTS = 8
TOP_K = 2

C0 = 0
D0 = 2048
A0 = 4096
Q0 = 5120
KV0 = 6144
G0 = 7680
NG0 = 8192
PCOLS = 8704
KC_PAD = 128
VT_ROWS = HEAD_DIM + 16


def _cparams(*sem):
    return pltpu.CompilerParams(dimension_semantics=sem, vmem_limit_bytes=VMEM_LIMIT_BYTES)


def _dot(a, b):
    return jnp.dot(a, b, preferred_element_type=F32)


def _dot_nt(a, b):
    return lax.dot_general(a, b, (((1,), (1,)), ((), ())), preferred_element_type=F32)


def _gelu_tanh(x):
    return 0.5 * x * (1.0 + jnp.tanh(math.sqrt(2.0 / math.pi) * (x + 0.044715 * (x * x * x))))


def _sigmoid(x):
    return 1.0 / (1.0 + jnp.exp(-x))


def _rmsnorm_kernel(x_ref, g_ref, o_ref, *packed_ref):
    xf = x_ref[...]
    r = lax.rsqrt(jnp.mean(xf * xf, axis=-1, keepdims=True) + EPS)
    y = (xf * r) * g_ref[...]
    o_ref[...] = y.astype(o_ref.dtype)
    if packed_ref:
        half = y.shape[1] // 2
        packed_ref[0][...] = _pack_bf16_pairs(y[:, :half], y[:, half:])


def rmsnorm_call(x, g, tr=256, packed=False):
    t, d = x.shape
    out_specs = [pl.BlockSpec((tr, d), lambda i: (i, 0))]
    out_shape = [jax.ShapeDtypeStruct((t, d), BF16)]
    if packed:
        out_specs.append(pl.BlockSpec((tr, d // 2), lambda i: (i, 0)))
        out_shape.append(jax.ShapeDtypeStruct((t, d // 2), jnp.uint32))
    out = pl.pallas_call(
        _rmsnorm_kernel,
        grid=(t // tr,),
        in_specs=[pl.BlockSpec((tr, d), lambda i: (i, 0)), pl.BlockSpec((1, d), lambda i: (0, 0))],
        out_specs=out_specs,
        out_shape=out_shape,
        compiler_params=_cparams("parallel"),
        name="rmsnorm",
    )(x, g)
    return out if packed else out[0]


def _mm_kernel(*refs, nk, has_res):
    if has_res:
        a_ref, w_ref, r_ref, o_ref = refs[:4]
        scr = refs[4:]
    else:
        a_ref, w_ref, o_ref = refs[:3]
        r_ref = None
        scr = refs[3:]
    part = _dot(a_ref[...], w_ref[...])

    def finish(acc):
        if has_res:
            acc = acc + r_ref[...]
        o_ref[...] = acc.astype(o_ref.dtype)

    if nk == 1:
        finish(part)
    else:
        acc_ref = scr[0]
        k = pl.program_id(2)

        @pl.when(k == 0)
        def _():
            acc_ref[...] = part

        @pl.when(k > 0)
        def _():
            acc_ref[...] += part

        @pl.when(k == nk - 1)
        def _():
            finish(acc_ref[...])


def matmul_call(a, w, res=None, out_dtype=BF16, tm=1024, tn=512, tk=4096, name="matmul"):
    m, kdim = a.shape
    n = w.shape[1]
    tk = min(tk, kdim)
    nk = kdim // tk
    assert m % tm == 0 and n % tn == 0 and kdim % tk == 0
    in_specs = [pl.BlockSpec((tm, tk), lambda i, j, k: (i, k)), pl.BlockSpec((tk, tn), lambda i, j, k: (k, j))]
    args = [a, w]
    if res is not None:
        in_specs.append(pl.BlockSpec((tm, tn), lambda i, j, k: (i, j)))
        args.append(res)
    return pl.pallas_call(
        functools.partial(_mm_kernel, nk=nk, has_res=res is not None),
        grid=(m // tm, n // tn, nk),
        in_specs=in_specs,
        out_specs=pl.BlockSpec((tm, tn), lambda i, j, k: (i, j)),
        out_shape=jax.ShapeDtypeStruct((m, n), out_dtype),
        scratch_shapes=[pltpu.VMEM((tm, tn), F32)] if nk > 1 else [],
        compiler_params=_cparams("parallel", "parallel", "arbitrary"),
        name=name,
    )(*args)


def _swiglu_kernel(h_ref, wg_ref, wu_ref, o_ref):
    h = h_ref[...]
    hg = _dot(h, wg_ref[...])
    hu = _dot(h, wu_ref[...])
    o_ref[...] = (hg * _sigmoid(hg) * hu).astype(o_ref.dtype)


def swiglu_call(h, wg, wu, tm=1024, tn=512):
    t, d = h.shape
    f = wg.shape[1]
    assert f % tn == 0 and t % tm == 0
    return pl.pallas_call(
        _swiglu_kernel,
        grid=(t // tm, f // tn),
        in_specs=[
            pl.BlockSpec((tm, d), lambda i, j: (i, 0)),
            pl.BlockSpec((d, tn), lambda i, j: (0, j)),
            pl.BlockSpec((d, tn), lambda i, j: (0, j)),
        ],
        out_specs=pl.BlockSpec((tm, tn), lambda i, j: (i, j)),
        out_shape=jax.ShapeDtypeStruct((t, f), BF16),
        compiler_params=_cparams("parallel", "arbitrary"),
        name="swiglu",
    )(h, wg, wu)


def _router_kernel(h_ref, w_ref, b_ref, info_ref, cnt_ref, run_ref, *, tm):
    @pl.when(pl.program_id(0) == 0)
    def _():
        run_ref[...] = jnp.zeros_like(run_ref)

    logits = _dot(h_ref[...], w_ref[...]) + b_ref[...]
    lane = lax.broadcasted_iota(jnp.int32, logits.shape, 1)
    logits = jnp.where(lane < N_EXPERTS, logits, MASKED)
    v1 = jnp.max(logits, axis=-1, keepdims=True)
    i1 = jnp.min(jnp.where(logits == v1, lane, LANES), axis=-1, keepdims=True)
    rest = jnp.where(lane == i1, MASKED, logits)
    v2 = jnp.max(rest, axis=-1, keepdims=True)
    i2 = jnp.min(jnp.where(rest == v2, lane, LANES), axis=-1, keepdims=True)
    e2 = jnp.exp(v2 - v1)
    inv = 1.0 / (1.0 + e2)
    chosen = jnp.where((lane == i1) | (lane == i2), 1.0, 0.0)
    earlier = lax.broadcasted_iota(jnp.int32, (tm, tm), 1) < lax.broadcasted_iota(jnp.int32, (tm, tm), 0)
    rank = _dot(jnp.where(earlier, 1.0, 0.0).astype(BF16), chosen.astype(BF16)) + run_ref[...]
    run_ref[...] += jnp.sum(chosen, axis=0, keepdims=True)
    cnt_ref[...] = run_ref[...]
    r1 = jnp.sum(jnp.where(lane == i1, rank, 0.0), axis=-1, keepdims=True)
    r2 = jnp.sum(jnp.where(lane == i2, rank, 0.0), axis=-1, keepdims=True)
    cols = (i1.astype(F32), i2.astype(F32), inv, e2 * inv, r1, r2)
    info = jnp.zeros(logits.shape, F32)
    for c, col in enumerate(cols):
        info = jnp.where(lane == c, col, info)
    info_ref[...] = info


def router_call(h, w, b, tm=1024):
    t, d = h.shape
    return pl.pallas_call(
        functools.partial(_router_kernel, tm=tm),
        grid=(t // tm,),
        in_specs=[pl.BlockSpec((tm, d), lambda i: (i, 0)), pl.BlockSpec((d, LANES), lambda i: (0, 0)),
                  pl.BlockSpec((1, LANES), lambda i: (0, 0))],
        out_specs=[pl.BlockSpec((tm, LANES), lambda i: (i, 0)), pl.BlockSpec((1, LANES), lambda i: (0, 0))],
        out_shape=[jax.ShapeDtypeStruct((t, LANES), F32), jax.ShapeDtypeStruct((1, LANES), F32)],
        scratch_shapes=[pltpu.VMEM((1, LANES), F32)],
        compiler_params=_cparams("arbitrary"),
        name="router",
    )(h, w, b)


def _pack_bf16_pairs(lo, hi):
    lo_bits = pltpu.bitcast(lo.astype(BF16).astype(F32), jnp.uint32)
    hi_bits = pltpu.bitcast(hi.astype(BF16).astype(F32), jnp.uint32)
    return jnp.right_shift(lo_bits, jnp.uint32(16)) | (hi_bits & jnp.uint32(0xFFFF0000))


def _unpack_bf16_pairs(u):
    lo = pltpu.bitcast(jnp.left_shift(u, jnp.uint32(16)), F32)
    hi = pltpu.bitcast(u & jnp.uint32(0xFFFF0000), F32)
    return lo, hi


def _row_copy(src_ref, src_row, dst_ref, dst_row, sem):
    return pltpu.make_async_copy(src_ref.at[pl.ds(src_row, 1), :], dst_ref.at[pl.ds(dst_row, 1), :], sem)


def _dispatch_kernel(dest_ref, hp_ref, xs_in_ref, xs_ref, sem, *, tt):
    del xs_in_ref
    i = pl.program_id(0)

    def start(t, c):
        tok = i * tt + t
        for k in range(TOP_K):
            _row_copy(hp_ref, tok, xs_ref, dest_ref[TOP_K * tok + k], sem).start()
        return c

    def wait(t, c):
        for k in range(TOP_K):
            _row_copy(hp_ref, 0, xs_ref, 0, sem).wait()
        return c

    lax.fori_loop(0, tt, start, 0)
    lax.fori_loop(0, tt, wait, 0)


def moe_dispatch_call(dest, hp, n_rows, tt=256):
    t, half = hp.shape
    zeros = jnp.zeros((n_rows, half), jnp.uint32)
    return pl.pallas_call(
        functools.partial(_dispatch_kernel, tt=tt),
        grid_spec=pltpu.PrefetchScalarGridSpec(
            num_scalar_prefetch=1,
            grid=(t // tt,),
            in_specs=[pl.BlockSpec(memory_space=pl.ANY), pl.BlockSpec(memory_space=pl.ANY)],
            out_specs=pl.BlockSpec(memory_space=pl.ANY),
            scratch_shapes=[pltpu.SemaphoreType.DMA(())],
        ),
        out_shape=jax.ShapeDtypeStruct((n_rows, half), jnp.uint32),
        input_output_aliases={2: 0},
        compiler_params=pltpu.CompilerParams(dimension_semantics=("arbitrary",), vmem_limit_bytes=VMEM_LIMIT_BYTES,
                                             has_side_effects=True),
        name="moe_dispatch",
    )(dest, hp, zeros)


def _moe_ffn_kernel(te_ref, nv_ref, xs_ref, wg_ref, wu_ref, wd_ref, y_ref, xb_ref, acc_ref, *, nj, half):
    r = pl.program_id(0)
    j = pl.program_id(1)
    valid = r < nv_ref[0]
    last = j == nj - 1

    @pl.when(valid & (j == 0))
    def _():
        lo, hi = _unpack_bf16_pairs(xs_ref[...])
        xb_ref[:, :half] = lo.astype(BF16)
        xb_ref[:, half:] = hi.astype(BF16)

    @pl.when(valid)
    def _():
        xb = xb_ref[...]
        hg = _dot(xb, wg_ref[...])
        hu = _dot(xb, wu_ref[...])
        part = _dot((hg * _sigmoid(hg) * hu).astype(BF16), wd_ref[...])

        @pl.when(j == 0)
        def _():
            acc_ref[...] = part

        @pl.when(j > 0)
        def _():
            acc_ref[...] += part

    @pl.when(valid & last)
    def _():
        y_ref[...] = _pack_bf16_pairs(acc_ref[:, :half], acc_ref[:, half:])

    @pl.when(jnp.logical_not(valid) & last)
    def _():
        y_ref[...] = jnp.zeros_like(y_ref)


def moe_ffn_call(tile_expert, n_valid, xs, wg, wu, wd, tm, tf=256):
    n_rows, half = xs.shape
    ne, d, f = wg.shape
    nj = f // tf
    assert f % tf == 0 and n_rows % tm == 0 and d == 2 * half

    def clamp(r, nv):
        return jnp.minimum(r, nv[0] - 1)

    def jj(r, j, nv):
        return jnp.where(r < nv[0], j, nj - 1)

    return pl.pallas_call(
        functools.partial(_moe_ffn_kernel, nj=nj, half=half),
        grid_spec=pltpu.PrefetchScalarGridSpec(
            num_scalar_prefetch=2,
            grid=(n_rows // tm, nj),
            in_specs=[
                pl.BlockSpec((tm, half), lambda r, j, te, nv: (clamp(r, nv), 0)),
                pl.BlockSpec((None, d, tf), lambda r, j, te, nv: (te[clamp(r, nv)], 0, jj(r, j, nv))),
                pl.BlockSpec((None, d, tf), lambda r, j, te, nv: (te[clamp(r, nv)], 0, jj(r, j, nv))),
                pl.BlockSpec((None, tf, d), lambda r, j, te, nv: (te[clamp(r, nv)], jj(r, j, nv), 0)),
            ],
            out_specs=pl.BlockSpec((tm, half), lambda r, j, te, nv: (r, 0)),
            scratch_shapes=[pltpu.VMEM((tm, d), BF16), pltpu.VMEM((tm, d), F32)],
        ),
        out_shape=jax.ShapeDtypeStruct((n_rows, half), jnp.uint32),
        compiler_params=_cparams("arbitrary", "arbitrary"),
        name="moe_ffn",
    )(tile_expert, n_valid, xs, wg, wu, wd)


def _combine_kernel(dest_ref, y_ref, x_ref, info_ref, o_ref, buf_ref, sem, *, tt, half):
    i = pl.program_id(0)

    def start(t, c):
        tok = i * tt + t
        for k in range(TOP_K):
            _row_copy(y_ref, dest_ref[TOP_K * tok + k], buf_ref.at[k], t, sem).start()
        return c

    def wait(t, c):
        for k in range(TOP_K):
            _row_copy(y_ref, 0, buf_ref.at[k], 0, sem).wait()
        return c

    lax.fori_loop(0, tt, start, 0)
    lax.fori_loop(0, tt, wait, 0)
    info = info_ref[...]
    lo_sum = x_ref[:, :half]
    hi_sum = x_ref[:, half:]
    for k in range(TOP_K):
        wk = info[:, 2 + k:3 + k]
        lo, hi = _unpack_bf16_pairs(buf_ref[k])
        lo_sum = lo_sum + wk * lo
        hi_sum = hi_sum + wk * hi
    o_ref[:, :half] = lo_sum
    o_ref[:, half:] = hi_sum


def moe_combine_call(dest, y, x, info, tt=256):
    t, d = x.shape
    half = d // 2
    return pl.pallas_call(
        functools.partial(_combine_kernel, tt=tt, half=half),
        grid_spec=pltpu.PrefetchScalarGridSpec(
            num_scalar_prefetch=1,
            grid=(t // tt,),
            in_specs=[pl.BlockSpec(memory_space=pl.ANY),
                      pl.BlockSpec((tt, d), lambda i, dest_: (i, 0)),
                      pl.BlockSpec((tt, LANES), lambda i, dest_: (i, 0))],
            out_specs=pl.BlockSpec((tt, d), lambda i, dest_: (i, 0)),
            scratch_shapes=[pltpu.VMEM((TOP_K, tt, half), jnp.uint32), pltpu.SemaphoreType.DMA(())],
        ),
        out_shape=jax.ShapeDtypeStruct((t, d), F32),
        compiler_params=_cparams("arbitrary"),
        name="moe_combine",
    )(dest, y, x, info)


def moe_layer(xf, h, hp, router_w, router_b, wg, wu, wd, tm=512):
    t, d = xf.shape
    ne = wg.shape[0]
    info, cnt = router_call(h, router_w, router_b)
    experts = info[:, 0:TOP_K].astype(jnp.int32)
    ranks = info[:, 4:4 + TOP_K].astype(jnp.int32)
    counts = cnt[0, :ne].astype(jnp.int32)
    padded = ((counts + tm - 1) // tm) * tm
    ends = jnp.cumsum(padded)
    dest = ((ends - padded)[experts] + ranks).reshape(-1)
    n_tiles = (t * TOP_K) // tm + ne
    tile_expert = jnp.minimum(jnp.sum(jnp.arange(n_tiles)[:, None] * tm >= ends[None, :], axis=-1), ne - 1).astype(jnp.int32)
    n_valid = (ends[-1:] // tm).astype(jnp.int32)
    xs = moe_dispatch_call(dest, hp, n_tiles * tm)
    y = moe_ffn_call(tile_expert, n_valid, xs, wg, wu, wd, tm)
    return moe_combine_call(dest, y, xf, info)


def _pool_kernel(cur_ref, halo_ref, w_ref, scale_ref, o_ref, xs_ref, *, ts, halo):
    i = pl.program_id(1)
    prev = halo_ref[...].astype(F32)
    xs_ref[0:halo, :] = jnp.where(i == 0, 0.0, prev)
    xs_ref[halo:halo + ts, :] = cur_ref[...].astype(F32)
    row = lax.broadcasted_iota(jnp.int32, (ts, POOL_GW), 0) + i * ts + 1
    for gi, w in enumerate(POOL_WINDOWS):
        c0 = gi * POOL_GW
        x = xs_ref[halo:halo + ts, c0:c0 + POOL_GW]
        wsum = x
        for k in range(1, w):
            wsum = wsum + xs_ref[halo - k:halo - k + ts, c0:c0 + POOL_GW]
        cnt = jnp.minimum(row, w).astype(F32)
        diff = (wsum / cnt - x).astype(BF16)
        y = _dot(diff, w_ref[gi]) * scale_ref[:, c0:c0 + POOL_GW]
        o_ref[:, c0:c0 + POOL_GW] = y.astype(o_ref.dtype)


def pool_call(p, w_pool, scale, batch, ts=512):
    t = p.shape[0]
    seq = t // batch
    nts = seq // ts
    halo = 16
    cb = A0 // MIX_W
    return pl.pallas_call(
        functools.partial(_pool_kernel, ts=ts, halo=halo),
        grid=(batch, nts),
        in_specs=[
            pl.BlockSpec((ts, MIX_W), lambda b, i: (b * nts + i, cb)),
            pl.BlockSpec((halo, MIX_W), lambda b, i: (jnp.maximum((b * nts + i) * (ts // halo) - 1, 0), cb)),
            pl.BlockSpec((len(POOL_WINDOWS), POOL_GW, POOL_GW), lambda b, i: (0, 0, 0)),
            pl.BlockSpec((1, MIX_W), lambda b, i: (0, 0)),
        ],
        out_specs=pl.BlockSpec((ts, MIX_W), lambda b, i: (b * nts + i, 0)),
        out_shape=jax.ShapeDtypeStruct((t, MIX_W), BF16),
        scratch_shapes=[pltpu.VMEM((halo + ts, MIX_W), F32)],
        compiler_params=_cparams("parallel", "arbitrary"),
        name="pool",
    )(p, p, w_pool, scale)


def _layernorm(v, g, b):
    mu = jnp.mean(v, axis=-1, keepdims=True)
    vc = v - mu
    var = jnp.mean(vc * vc, axis=-1, keepdims=True)
    return (vc * lax.rsqrt(var + EPS)) * g + b


def _gmlp_kernel(z_ref, g_ref, b_ref, ws_ref, bs_ref, o_ref, *, ts):
    z = _gelu_tanh(z_ref[...].astype(F32))
    u = z[:, :MIX_W]
    v = _layernorm(z[:, MIX_W:], g_ref[...], b_ref[...]).astype(BF16)
    ri = lax.broadcasted_iota(jnp.int32, (GMLP_CHUNK, GMLP_CHUNK), 0)
    ci = lax.broadcasted_iota(jnp.int32, (GMLP_CHUNK, GMLP_CHUNK), 1)
    tri = ci <= ri
    for h in range(MIX_W // HEAD_DIM):
        w = jnp.where(tri, ws_ref[h], 0.0).astype(BF16)
        bias = bs_ref[:, h * HEAD_DIM:(h + 1) * HEAD_DIM]
        for c in range(ts // GMLP_CHUNK):
            rows = slice(c * GMLP_CHUNK, (c + 1) * GMLP_CHUNK)
            cols = slice(h * HEAD_DIM, (h + 1) * HEAD_DIM)
            s = _dot(w, v[rows, cols]) + bias
            o_ref[rows, cols] = (u[rows, cols] * s).astype(o_ref.dtype)


def gmlp_call(p, ln_g, ln_b, ws, bs_b, ts=512):
    t = p.shape[0]
    return pl.pallas_call(
        functools.partial(_gmlp_kernel, ts=ts),
        grid=(t // ts,),
        in_specs=[
            pl.BlockSpec((ts, 2 * MIX_W), lambda i: (i, C0 // (2 * MIX_W))),
            pl.BlockSpec((1, MIX_W), lambda i: (0, 0)),
            pl.BlockSpec((1, MIX_W), lambda i: (0, 0)),
            pl.BlockSpec(ws.shape, lambda i: (0, 0, 0)),
            pl.BlockSpec(bs_b.shape, lambda i: (0, 0)),
        ],
        out_specs=pl.BlockSpec((ts, MIX_W), lambda i: (i, 0)),
        out_shape=jax.ShapeDtypeStruct((t, MIX_W), BF16),
        compiler_params=_cparams("parallel"),
        name="gmlp",
    )(p, ln_g, ln_b, ws, bs_b)


def _conv_kernel(cur_ref, halo_ref, w_ref, b_ref, g_ref, beta_ref, o_ref, hs_ref, *, ts, halo):
    i = pl.program_id(1)

    def glu(z):
        zf = z.astype(F32)
        return zf[:, :MIX_W] * _sigmoid(zf[:, MIX_W:])

    hs_ref[0:halo, :] = jnp.where(i == 0, 0.0, glu(halo_ref[...]))
    hs_ref[halo:halo + ts, :] = glu(cur_ref[...])
    off = halo - (CONV_W - 1)
    acc = jnp.zeros((ts, MIX_W), F32) + b_ref[...]
    for k in range(CONV_W):
        acc = acc + w_ref[k:k + 1, :] * hs_ref[off + k:off + k + ts, :]
    y = _layernorm(acc, g_ref[...], beta_ref[...])
    o_ref[...] = (y * _sigmoid(y)).astype(o_ref.dtype)


def conv_call(p, w, b, ln_g, ln_b, batch, ts=256):
    t = p.shape[0]
    seq = t // batch
    nts = seq // ts
    halo = 32
    cb = D0 // (2 * MIX_W)
    return pl.pallas_call(
        functools.partial(_conv_kernel, ts=ts, halo=halo),
        grid=(batch, nts),
        in_specs=[
            pl.BlockSpec((ts, 2 * MIX_W), lambda b_, i: (b_ * nts + i, cb)),
            pl.BlockSpec((halo, 2 * MIX_W), lambda b_, i: (jnp.maximum((b_ * nts + i) * (ts // halo) - 1, 0), cb)),
            pl.BlockSpec((CONV_W, MIX_W), lambda b_, i: (0, 0)),
            pl.BlockSpec((1, MIX_W), lambda b_, i: (0, 0)),
            pl.BlockSpec((1, MIX_W), lambda b_, i: (0, 0)),
            pl.BlockSpec((1, MIX_W), lambda b_, i: (0, 0)),
        ],
        out_specs=pl.BlockSpec((ts, MIX_W), lambda b_, i: (b_ * nts + i, 0)),
        out_shape=jax.ShapeDtypeStruct((t, MIX_W), BF16),
        scratch_shapes=[pltpu.VMEM((halo + ts, MIX_W), F32)],
        compiler_params=_cparams("parallel", "arbitrary"),
        name="conv",
    )(p, p, w, b, ln_g, ln_b)


def _head_rms(x, g):
    xf = x.astype(F32)
    r = lax.rsqrt(jnp.mean(xf * xf, axis=-1, keepdims=True) + EPS)
    return (xf * r) * g


def _nsa_prep_kernel(q_ref, ks_ref, vs_ref, kw_ref, vw_ref, qg_ref, kg_ref, qo_ref, kso_ref, kwo_ref, vso_ref, vwo_ref, *, tr):
    qg = qg_ref[...]
    kg = kg_ref[...]
    for h in range(NSA_HEADS):
        cols = slice(h * HEAD_DIM, (h + 1) * HEAD_DIM)
        qo_ref[:, cols] = (_head_rms(q_ref[:, cols], qg) * (HEAD_DIM ** -0.5)).astype(qo_ref.dtype)
    ones = jnp.ones((VT_ROWS - HEAD_DIM, Q_BLOCK), vso_ref.dtype)
    for g in range(NSA_KV):
        cols = slice(g * HEAD_DIM, (g + 1) * HEAD_DIM)
        kso_ref[:, cols] = _head_rms(ks_ref[:, cols], kg).astype(kso_ref.dtype)
        kwo_ref[:, cols] = _head_rms(kw_ref[:, cols], kg).astype(kwo_ref.dtype)
        for u in range(tr // Q_BLOCK):
            rows = slice(u * Q_BLOCK, (u + 1) * Q_BLOCK)
            for v_ref, vo_ref in ((vs_ref, vso_ref), (vw_ref, vwo_ref)):
                vo_ref[g, u, 0:HEAD_DIM, :] = v_ref[rows, cols].astype(F32).T.astype(vo_ref.dtype)
                vo_ref[g, u, HEAD_DIM:VT_ROWS, :] = ones


def nsa_prep_call(p, q_g, k_g, batch, tr=512):
    t = p.shape[0]
    seq = t // batch
    nst = seq // tr
    kvw = NSA_KV * HEAD_DIM
    kv_spec = lambda c: pl.BlockSpec((tr, kvw), lambda i: (i, (KV0 + c * kvw) // kvw))
    vt_sds = jax.ShapeDtypeStruct((batch * NSA_KV, seq // Q_BLOCK, VT_ROWS, Q_BLOCK), BF16)
    vt_spec = pl.BlockSpec((NSA_KV, tr // Q_BLOCK, VT_ROWS, Q_BLOCK), lambda i: (i // nst, i % nst, 0, 0))
    return pl.pallas_call(
        functools.partial(_nsa_prep_kernel, tr=tr),
        grid=(t // tr,),
        in_specs=[
            pl.BlockSpec((tr, MIX_W), lambda i: (i, Q0 // MIX_W)),
            kv_spec(2), kv_spec(3), kv_spec(4), kv_spec(5),
            pl.BlockSpec((1, HEAD_DIM), lambda i: (0, 0)),
            pl.BlockSpec((1, HEAD_DIM), lambda i: (0, 0)),
        ],
        out_specs=[
            pl.BlockSpec((tr, MIX_W), lambda i: (i, 0)),
            pl.BlockSpec((tr, kvw), lambda i: (i, 0)),
            pl.BlockSpec((tr, kvw), lambda i: (i, 0)),
            vt_spec, vt_spec,
        ],
        out_shape=[jax.ShapeDtypeStruct((t, MIX_W), BF16), jax.ShapeDtypeStruct((t, kvw), BF16),
                   jax.ShapeDtypeStruct((t, kvw), BF16), vt_sds, vt_sds],
        compiler_params=_cparams("parallel"),
        name="nsa_prep",
    )(p, p, p, p, p, q_g, k_g)


def _compress_kernel(ck_ref, cv_ref, pk_ref, pv_ref, w1k_ref, w2k_ref, w1v_ref, w2v_ref, kg_ref, ko_ref, vo_ref, *, nc):
    def comp(c_ref, p_ref, w1_ref, w2_ref):
        c = c_ref[...].astype(F32)
        xa = (c + p_ref[0:1, :]).astype(BF16)
        xb = (c + p_ref[1:2, :]).astype(BF16)
        first = _dot(xa, w1_ref[0])
        second = _dot(xb, w1_ref[1])
        hdn = _gelu_tanh(first + pltpu.roll(second, nc - 1, 0))
        return _dot(hdn.astype(BF16), w2_ref[...])

    row = lax.broadcasted_iota(jnp.int32, (nc, HEAD_DIM), 0)
    real = row < nc - 1
    kc = _head_rms(comp(ck_ref, pk_ref, w1k_ref, w2k_ref), kg_ref[...])
    vc = comp(cv_ref, pv_ref, w1v_ref, w2v_ref)
    zeros = jnp.zeros((KC_PAD, HEAD_DIM), F32)
    ko_ref[0:KC_PAD, :] = zeros
    vo_ref[0:KC_PAD, :] = zeros
    ko_ref[KC_PAD:KC_PAD + nc, :] = jnp.where(real, kc, 0.0)
    vo_ref[KC_PAD:KC_PAD + nc, :] = jnp.where(real, vc, 0.0)


def compress_call(ck, cv, pos_k, pos_v, w1k, w2k, w1v, w2v, k_g):
    bg, nc, half = ck.shape
    full2 = lambda shape: pl.BlockSpec(shape, lambda i: (0,) * len(shape))
    out_sds = jax.ShapeDtypeStruct((bg, KC_PAD + nc, HEAD_DIM), F32)
    return pl.pallas_call(
        functools.partial(_compress_kernel, nc=nc),
        grid=(bg,),
        in_specs=[
            pl.BlockSpec((None, nc, half), lambda i: (i, 0, 0)),
            pl.BlockSpec((None, nc, half), lambda i: (i, 0, 0)),
            full2((2, half)), full2((2, half)),
            full2((2, half, HEAD_DIM)), full2((HEAD_DIM, HEAD_DIM)),
            full2((2, half, HEAD_DIM)), full2((HEAD_DIM, HEAD_DIM)),
            full2((1, HEAD_DIM)),
        ],
        out_specs=[pl.BlockSpec((None, KC_PAD + nc, HEAD_DIM), lambda i: (i, 0, 0))] * 2,
        out_shape=[out_sds, out_sds],
        compiler_params=_cparams("parallel"),
        name="nsa_compress",
    )(ck, cv, pos_k, pos_v, w1k, w2k, w1v, w2v, k_g)


def _softmax_rows(parts):
    m = parts[0].max(axis=-1, keepdims=True)
    for x in parts[1:]:
        m = jnp.maximum(m, x.max(axis=-1, keepdims=True))
    m = jnp.maximum(m, -1e20)
    ps = [jnp.exp(x - m) for x in parts]
    s = ps[0].sum(axis=-1, keepdims=True)
    for p in ps[1:]:
        s = s + p.sum(axis=-1, keepdims=True)
    inv = 1.0 / jnp.maximum(s, 1e-30)
    return ps, inv


def _split_hi_lo(x):
    hi = x.astype(BF16)
    lo = (x - hi.astype(F32)).astype(BF16)
    return hi, lo


def _nsa_kernel(q_ref, kc_ref, vc_ref, ks_ref, vs_ref, kw_ref, vw_ref, ng_ref,
                bcn_ref, bs_ref, bw_ref, ov_ref, e_ref, o_ref, m_ref, acc_ref, ow_ref):
    i = pl.program_id(2)
    qb = Q_BLOCK
    rows = NSA_HPG * qb
    hd = HEAD_DIM
    qs = jnp.concatenate([q_ref[:, r * hd:(r + 1) * hd] for r in range(NSA_HPG)], axis=0)
    kcp = kc_ref.shape[0]

    nwin = WINDOW // qb + 1
    wtiles = [jnp.maximum(i - (nwin - 1 - tt), 0) for tt in range(nwin)]
    k_win = jnp.concatenate([kw_ref[pl.ds(pl.multiple_of(wt * qb, qb), qb), :] for wt in wtiles], axis=0)
    ltw = _dot_nt(k_win, qs) + bw_ref[...]
    wrow = lax.broadcasted_iota(jnp.int32, ltw.shape, 0)
    ltw = jnp.where(wrow >= (nwin - 1 - i) * qb, ltw, MASKED)
    mw = jnp.max(ltw, axis=0, keepdims=True)
    pw = jnp.exp(ltw - mw).astype(BF16)
    acc_w = _dot(jnp.concatenate([vw_ref[wt] for wt in wtiles], axis=1), pw)
    ow_ref[...] = acc_w[0:hd] * (1.0 / jnp.maximum(acc_w[hd:hd + 1], 1e-30))

    near0 = pl.multiple_of(8 * i + 8, 8)
    lf = _dot_nt(qs, kc_ref[...].astype(BF16))
    npad = lax.broadcasted_iota(jnp.int32, (rows, kcp), 1)
    lf = jnp.where((npad >= KC_PAD) & (npad < near0), lf, MASKED)
    ln = _dot_nt(qs, kc_ref[pl.ds(near0, qb), :].astype(BF16)) + bcn_ref[...]
    ncol = lax.broadcasted_iota(jnp.int32, (rows, qb), 1)
    ln = jnp.where(ncol >= KC_PAD - 8 - 8 * i, ln, MASKED)
    (pf, pn), inv = _softmax_rows([lf, ln])
    pf = pf * inv
    pn = pn * inv
    o_c = _dot(pf.astype(BF16), vc_ref[...].astype(BF16)) + _dot(pn.astype(BF16), vc_ref[pl.ds(near0, qb), :].astype(BF16))

    pf_g = pf[0:qb]
    pn_g = pn[0:qb]
    for r in range(1, NSA_HPG):
        pf_g = pf_g + pf[r * qb:(r + 1) * qb]
        pn_g = pn_g + pn[r * qb:(r + 1) * qb]
    ov_all = ov_ref[...].astype(BF16)
    ov_near = ov_ref[pl.ds(near0, qb), :].astype(BF16)
    imp = jnp.zeros((qb, ov_ref.shape[1]), F32)
    for part in _split_hi_lo(pf_g):
        imp = imp + _dot(part, ov_all)
    for part in _split_hi_lo(pn_g):
        imp = imp + _dot(part, ov_near)
    nblk = imp.shape[1]
    jj = lax.broadcasted_iota(jnp.int32, (qb, nblk), 1)
    tq = lax.broadcasted_iota(jnp.int32, (qb, nblk), 0) + i * qb
    cur = jnp.right_shift(tq, 6)
    forced = (jj == 0) | (jj == cur) | (jj == cur - 1)
    score = jnp.where(jj <= cur, imp + jnp.where(forced, FORCE_BONUS, 0.0), NEG)
    sc = score.T
    jrow = lax.broadcasted_iota(jnp.int32, sc.shape, 0).astype(F32)
    sel_t = jnp.zeros(sc.shape, F32)
    for _ in range(min(SLC_TOPK, nblk)):
        mx = jnp.max(sc, axis=0, keepdims=True)
        first = jnp.min(jnp.where(sc == mx, jrow, float(nblk)), axis=0, keepdims=True)
        pick = jrow == first
        sel_t = jnp.where(pick, 1.0, sel_t)
        sc = jnp.where(pick, -3e38, sc)
    sel = sel_t.T

    sel_neg = jnp.where(sel > 0.5, 0.0, MASKED).astype(BF16)
    q_aug = jnp.concatenate([qs, jnp.concatenate([sel_neg] * NSA_HPG, axis=0)], axis=1)

    def block_onehot(e_rows, valid):
        return jnp.where(valid, e_rows, jnp.ones_like(e_rows))

    prev_t = jnp.maximum(i - 1, 0)
    prev0 = pl.multiple_of(prev_t * qb, qb)
    diag0 = pl.multiple_of(i * qb, qb)
    k_near = jnp.concatenate([ks_ref[pl.ds(prev0, qb), :], ks_ref[pl.ds(diag0, qb), :]], axis=0)
    e_near = jnp.concatenate([e_ref[pl.ds(prev0, qb), :], e_ref[pl.ds(diag0, qb), :]], axis=0)
    row_near = lax.broadcasted_iota(jnp.int32, e_near.shape, 0)
    e_near = block_onehot(e_near, row_near >= jnp.where(i == 0, qb, 0))
    lt = _dot_nt(jnp.concatenate([k_near, e_near], axis=1), q_aug) + bs_ref[...]
    m0 = jnp.maximum(jnp.max(lt, axis=0, keepdims=True), -1e20)
    p0 = jnp.exp(lt - m0).astype(BF16)
    vt_near = jnp.concatenate([vs_ref[prev_t], vs_ref[i]], axis=1)
    m_ref[...] = m0
    acc_ref[...] = _dot(vt_near, p0)

    far_end = (i - 1) * qb
    tiles = 4
    chunk = tiles * qb
    halves = 2

    def far_body(c, carry):
        lts, vts = [], []
        for hf in range(halves):
            t0 = (c * halves + hf) * tiles
            c0 = pl.multiple_of(t0 * qb, chunk)
            e_c = e_ref[pl.ds(c0, chunk), :]
            row_c = lax.broadcasted_iota(jnp.int32, e_c.shape, 0) + c0
            k_aug = jnp.concatenate([ks_ref[pl.ds(c0, chunk), :], block_onehot(e_c, row_c < far_end)], axis=1)
            lts.append(_dot_nt(k_aug, q_aug))
            vts.append(jnp.concatenate([vs_ref[t0 + u] for u in range(tiles)], axis=1))
        m_old = m_ref[...]
        m_new = m_old
        for lt_c in lts:
            m_new = jnp.maximum(m_new, jnp.max(lt_c, axis=0, keepdims=True))
        acc = jnp.exp(m_old - m_new) * acc_ref[...]
        for lt_c, vt_c in zip(lts, vts):
            acc = acc + _dot(vt_c, jnp.exp(lt_c - m_new).astype(BF16))
        m_ref[...] = m_new
        acc_ref[...] = acc
        return carry

    lax.fori_loop(0, (i - 1 + halves * tiles - 1) // (halves * tiles), far_body, 0)
    acc_s = acc_ref[...]
    o_s_t = acc_s[0:hd] * (1.0 / jnp.maximum(acc_s[hd:hd + 1], 1e-30))
    o_w_t = ow_ref[...]

    gate = _sigmoid(ng_ref[...].astype(F32))
    gate_t = gate.T
    for r in range(NSA_HPG):
        rs = slice(r * qb, (r + 1) * qb)
        o_sw_t = gate_t[3 * r + 1:3 * r + 2, :] * o_s_t[:, rs] + gate_t[3 * r + 2:3 * r + 3, :] * o_w_t[:, rs]
        o = gate[:, 3 * r:3 * r + 1] * o_c[rs] + o_sw_t.T
        o_ref[:, r * hd:(r + 1) * hd] = o.astype(o_ref.dtype)


def nsa_call(p, qn, ksn, kwn, vst, vwt, kcp, vcp, bcn, bsl_t, bwn_t, ovl, e_tab, batch):
    t = p.shape[0]
    seq = t // batch
    nq = seq // Q_BLOCK
    assert nq % 8 == 0
    g_ = NSA_KV
    kcp_rows = kcp.shape[1]
    rows = NSA_HPG * Q_BLOCK
    ng_blk = NG0 // LANES
    seq_spec = pl.BlockSpec((seq, HEAD_DIM), lambda b, g, i: (b, g))
    vt_spec = pl.BlockSpec((None, nq, VT_ROWS, Q_BLOCK), lambda b, g, i: (b * g_ + g, 0, 0, 0))
    cmp_spec = pl.BlockSpec((None, kcp_rows, HEAD_DIM), lambda b, g, i: (b * g_ + g, 0, 0))
    tbl_spec = lambda arr: pl.BlockSpec((None,) + arr.shape[1:], lambda b, g, i: (g, 0, 0))
    return pl.pallas_call(
        _nsa_kernel,
        grid=(batch, g_, nq),
        in_specs=[
            pl.BlockSpec((Q_BLOCK, rows), lambda b, g, i: (b * nq + i, g)),
            cmp_spec, cmp_spec,
            seq_spec, vt_spec, seq_spec, vt_spec,
            pl.BlockSpec((Q_BLOCK, LANES), lambda b, g, i: (b * nq + i, ng_blk + g)),
            tbl_spec(bcn), tbl_spec(bsl_t), tbl_spec(bwn_t),
            pl.BlockSpec(ovl.shape, lambda b, g, i: (0, 0)),
            pl.BlockSpec(e_tab.shape, lambda b, g, i: (0, 0)),
        ],
        out_specs=pl.BlockSpec((Q_BLOCK, rows), lambda b, g, i: (b * nq + i, g)),
        out_shape=jax.ShapeDtypeStruct((t, MIX_W), BF16),
        scratch_shapes=[pltpu.VMEM((1, rows), F32), pltpu.VMEM((VT_ROWS, rows), F32), pltpu.VMEM((HEAD_DIM, rows), F32)],
        compiler_params=_cparams("parallel", "parallel", "arbitrary"),
        name="nsa_attention",
    )(qn, kcp, vcp, ksn, vst, kwn, vwt, p, bcn, bsl_t, bwn_t, ovl, e_tab)


def _t5_bucket_np(dist):
    n = np.maximum(dist, 0)
    max_exact = REL_BUCKETS // 2
    nf = np.maximum(n, 1).astype(np.float32)
    large = max_exact + (np.log(nf / max_exact) / math.log(REL_MAX_DIST / max_exact)
                         * (REL_BUCKETS - max_exact)).astype(np.int32)
    large = np.minimum(large, REL_BUCKETS - 1)
    return np.where(n < max_exact, n, large)


def _bias_tables(rel_bias, seq):
    qb = Q_BLOCK
    iq = np.arange(qb)[:, None]

    def tile(dist, visible):
        b = rel_bias[_t5_bucket_np(dist)] - rel_bias[REL_BUCKETS - 1][None, None, :]
        b = jnp.where(visible[:, :, None], b, MASKED)
        w = dist.shape[1]
        return b.transpose(2, 0, 1).reshape(NSA_KV, NSA_HPG * qb, w)

    jk = np.arange(qb)[None, :]
    d_diag = iq - jk
    d_prev = qb + iq - jk
    d_edge = WINDOW + iq - jk
    d_far = np.full((qb, qb), 2 * qb)
    d_sel = np.concatenate([d_prev, d_diag], axis=1)
    d_win = np.concatenate([d_edge] + [d_far] * (WINDOW // qb - 2) + [d_prev, d_diag], axis=1)
    d_cmp = iq - CMP_STRIDE * (jk - (KC_PAD - 8)) - (CMP_LEN - 1)
    bsl_t = tile(d_sel, d_sel >= 0).transpose(0, 2, 1)
    bwn_t = tile(d_win, (d_win >= 0) & (d_win < WINDOW)).transpose(0, 2, 1)
    bcn = tile(d_cmp, d_cmp >= 0)
    nc = seq // CMP_STRIDE
    n_slc = seq // SLC_LEN
    c0 = (np.arange(nc) * CMP_STRIDE)[:, None]
    s0 = (np.arange(n_slc) * SLC_LEN)[None, :]
    ov = ((c0 < s0 + SLC_LEN) & (c0 + CMP_LEN > s0) & (np.arange(nc)[:, None] < nc - 1)).astype(np.float32)
    ovl = np.concatenate([np.zeros((KC_PAD, n_slc), np.float32), ov], axis=0)
    e_tab = (np.arange(seq)[:, None] // SLC_LEN == np.arange(n_slc)[None, :]).astype(np.float32)
    return bcn, bsl_t, bwn_t, jnp.asarray(ovl), jnp.asarray(e_tab, dtype=BF16)


def _merge_kernel(g_ref, a_ref, b_ref, c_ref, d_ref, wgu_ref, wbr_ref, o_ref):
    gin = g_ref[...]
    acc = None
    for bi, br in enumerate((a_ref, b_ref, c_ref, d_ref)):
        gate = _sigmoid(_dot(gin, wgu_ref[bi]))
        term = gate * _dot(br[...], wbr_ref[bi])
        acc = term if acc is None else acc + term
    o_ref[...] = acc.astype(o_ref.dtype)


def merge_call(p, branches, wgu, wbr, tm=1024, tn=512):
    t = p.shape[0]
    d = wgu.shape[2]
    row = lambda w, cb: pl.BlockSpec((tm, w), lambda i, j: (i, cb))
    return pl.pallas_call(
        _merge_kernel,
        grid=(t // tm, d // tn),
        in_specs=[row(GATE_RANK, G0 // GATE_RANK)] + [row(MIX_W, 0)] * 4 + [
            pl.BlockSpec((4, GATE_RANK, tn), lambda i, j: (0, 0, j)),
            pl.BlockSpec((4, MIX_W, tn), lambda i, j: (0, 0, j)),
        ],
        out_specs=pl.BlockSpec((tm, tn), lambda i, j: (i, j)),
        out_shape=jax.ShapeDtypeStruct((t, d), BF16),
        compiler_params=_cparams("parallel", "arbitrary"),
        name="merge",
    )(p, *branches, wgu, wbr)


def _rearranged_w_in(w):
    o_a, o_q, o_kv, o_ng, o_c, o_d, o_g = 0, 1024, 2048, 3584, 3608, 5656, 7704
    d = w.shape[0]
    hpg3 = NSA_HPG * 3
    zpad = lambda n: jnp.zeros((d, n), BF16)
    parts = [
        w[:, o_c:o_c + 2048], w[:, o_d:o_d + 2048], w[:, o_a:o_a + 1024],
        w[:, o_q:o_q + 1024], w[:, o_kv:o_kv + 1536], w[:, o_g:o_g + 512],
        w[:, o_ng:o_ng + hpg3], zpad(LANES - hpg3),
        w[:, o_ng + hpg3:o_ng + 2 * hpg3], zpad(LANES - hpg3), zpad(PCOLS - NG0 - 2 * LANES),
    ]
    return jnp.concatenate([part.astype(BF16) for part in parts], axis=-1)


def kernel(x, rel_bias, norm_mix_g, norm_ffn_g, w_in, pool_w, pool_scale, q_norm_g, k_norm_g, cmp_pos_k, cmp_w1_k, cmp_w2_k, cmp_pos_v, cmp_w1_v, cmp_w2_v, gmlp_ln_g, gmlp_ln_b, gmlp_ws, gmlp_bs, conv_w, conv_b, conv_ln_g, conv_ln_b, w_branch, w_gate_up, w_out, ffn_w_gate, ffn_w_up, ffn_w_down, moe_router, moe_router_b, moe_w_gate, moe_w_up, moe_w_down):
    batch, seq, d = x.shape
    t = batch * seq
    depth = w_in.shape[0]
    nc = seq // CMP_STRIDE
    half = CMP_STRIDE * HEAD_DIM
    xf = x.reshape(t, d)

    bcn, bsl_t, bwn_t, ovl, e_tab = _bias_tables(rel_bias, seq)
    row = lambda v: v.reshape(1, -1)

    for l in range(depth):
        h = rmsnorm_call(xf, row(norm_mix_g[l]))
        p = matmul_call(h, _rearranged_w_in(w_in[l]), name="in_proj")
        o_a = pool_call(p, pool_w[l].astype(BF16), row(pool_scale[l]), batch)
        qn, ksn, kwn, vst, vwt = nsa_prep_call(p, row(q_norm_g[l]), row(k_norm_g[l]), batch)
        kvc = p[:, KV0:KV0 + 2 * NSA_KV * HEAD_DIM].reshape(batch, nc, CMP_STRIDE, 2, NSA_KV, HEAD_DIM)
        kvc = kvc.transpose(3, 0, 4, 1, 2, 5).reshape(2, batch * NSA_KV, nc, half)
        kcp, vcp = compress_call(
            kvc[0], kvc[1], cmp_pos_k[l].reshape(2, half), cmp_pos_v[l].reshape(2, half),
            cmp_w1_k[l].reshape(2, half, HEAD_DIM).astype(BF16), cmp_w2_k[l].astype(BF16),
            cmp_w1_v[l].reshape(2, half, HEAD_DIM).astype(BF16), cmp_w2_v[l].astype(BF16), row(k_norm_g[l]))
        o_b = nsa_call(p, qn, ksn, kwn, vst, vwt, kcp, vcp, bcn, bsl_t, bwn_t, ovl, e_tab, batch)
        bs_b = jnp.repeat(gmlp_bs[l].T, HEAD_DIM, axis=1)
        o_c = gmlp_call(p, row(gmlp_ln_g[l]), row(gmlp_ln_b[l]), gmlp_ws[l], bs_b)
        o_d = conv_call(p, conv_w[l], row(conv_b[l]), row(conv_ln_g[l]), row(conv_ln_b[l]), batch)
        mix = merge_call(p, (o_a, o_b, o_c, o_d), w_gate_up[l].astype(BF16), w_branch[l].astype(BF16))
        xf = matmul_call(mix, w_out[l].astype(BF16), res=xf, out_dtype=F32, name="out_proj")
        i = l // 2
        if l % 2 == 0:
            h = rmsnorm_call(xf, row(norm_ffn_g[l]))
            act = swiglu_call(h, ffn_w_gate[i].astype(BF16), ffn_w_up[i].astype(BF16))
            xf = matmul_call(act, ffn_w_down[i].astype(BF16), res=xf, out_dtype=F32, name="ffn_down")
        else:
            h, hp = rmsnorm_call(xf, row(norm_ffn_g[l]), packed=True)
            rw = jnp.pad(moe_router[i], ((0, 0), (0, LANES - N_EXPERTS))).astype(BF16)
            rb = jnp.pad(moe_router_b[i], (0, LANES - N_EXPERTS)).reshape(1, LANES)
            xf = moe_layer(xf, h, hp, rw, rb, moe_w_gate[i].astype(BF16), moe_w_up[i].astype(BF16),
                           moe_w_down[i].astype(BF16))
    return xf.reshape(batch, seq, d)
```

```python
import functools
import math

import jax
import jax.numpy as jnp
import numpy as np
from jax import lax
from jax.experimental import pallas as pl
from jax.experimental.pallas import tpu as pltpu

F32 = jnp.float32
BF16 = jnp.bfloat16

VMEM_LIMIT_BYTES = 56 * 1024 * 1024
LANES = 128

EPS = 1e-6
HEAD_DIM = 128
MIX_W = 1024
POOL_WINDOWS = (2, 4, 8, 16)
POOL_GW = MIX_W // len(POOL_WINDOWS)
NSA_HEADS = 8
NSA_KV = 2
NSA_HPG = NSA_HEADS // NSA_KV
CMP_LEN = 32
CMP_STRIDE = 16
SLC_LEN = 64
SLC_TOPK = 16
WINDOW = 512
Q_BLOCK = 128
FORCE_BONUS = 1e4
NEG = -1e9
MASKED = -1e30
GMLP_CHUNK = 128
CONV_W = 31
GATE_RANK = 512
REL_BUCKETS = 32
REL_MAX_DIST = 128
N_EXPERTS = 8
TOP_K = 2

C0 = 0
D0 = 2048
A0 = 4096
Q0 = 5120
KV0 = 6144
G0 = 7680
NG0 = 8192
PCOLS = 8704
KC_PAD = 128
VT_ROWS = HEAD_DIM + 16


def _cparams(*sem):
    return pltpu.CompilerParams(dimension_semantics=sem, vmem_limit_bytes=VMEM_LIMIT_BYTES)


def _dot(a, b):
    return jnp.dot(a, b, preferred_element_type=F32)


def _dot_nt(a, b):
    return lax.dot_general(a, b, (((1,), (1,)), ((), ())), preferred_element_type=F32)


def _gelu_tanh(x):
    return 0.5 * x * (1.0 + jnp.tanh(math.sqrt(2.0 / math.pi) * (x + 0.044715 * (x * x * x))))


def _sigmoid(x):
    return 1.0 / (1.0 + jnp.exp(-x))


def _rmsnorm_kernel(x_ref, g_ref, o_ref, *packed_ref):
    xf = x_ref[...]
    r = lax.rsqrt(jnp.mean(xf * xf, axis=-1, keepdims=True) + EPS)
    y = (xf * r) * g_ref[...]
    o_ref[...] = y.astype(o_ref.dtype)
    if packed_ref:
        half = y.shape[1] // 2
        packed_ref[0][...] = _pack_bf16_pairs(y[:, :half], y[:, half:])


def rmsnorm_call(x, g, tr=256, packed=False):
    t, d = x.shape
    out_specs = [pl.BlockSpec((tr, d), lambda i: (i, 0))]
    out_shape = [jax.ShapeDtypeStruct((t, d), BF16)]
    if packed:
        out_specs.append(pl.BlockSpec((tr, d // 2), lambda i: (i, 0)))
        out_shape.append(jax.ShapeDtypeStruct((t, d // 2), jnp.uint32))
    out = pl.pallas_call(
        _rmsnorm_kernel,
        grid=(t // tr,),
        in_specs=[pl.BlockSpec((tr, d), lambda i: (i, 0)), pl.BlockSpec((1, d), lambda i: (0, 0))],
        out_specs=out_specs,
        out_shape=out_shape,
        compiler_params=_cparams("parallel"),
        name="rmsnorm",
    )(x, g)
    return out if packed else out[0]


def _mm_kernel(*refs, nk, has_res):
    if has_res:
        a_ref, w_ref, r_ref, o_ref = refs[:4]
        scr = refs[4:]
    else:
        a_ref, w_ref, o_ref = refs[:3]
        r_ref = None
        scr = refs[3:]
    part = _dot(a_ref[...], w_ref[...])

    def finish(acc):
        if has_res:
            acc = acc + r_ref[...]
        o_ref[...] = acc.astype(o_ref.dtype)

    if nk == 1:
        finish(part)
    else:
        acc_ref = scr[0]
        k = pl.program_id(2)

        @pl.when(k == 0)
        def _():
            acc_ref[...] = part

        @pl.when(k > 0)
        def _():
            acc_ref[...] += part

        @pl.when(k == nk - 1)
        def _():
            finish(acc_ref[...])


def matmul_call(a, w, layer=None, res=None, out_dtype=BF16, tm=1024, tn=512, tk=4096, name="matmul"):
    m, kdim = a.shape
    n = w.shape[-1]
    tk = min(tk, kdim)
    nk = kdim // tk
    assert m % tm == 0 and n % tn == 0 and kdim % tk == 0
    if layer is None:
        w_spec = pl.BlockSpec((tk, tn), lambda i, j, k: (k, j))
    else:
        w_spec = pl.BlockSpec((None, tk, tn), lambda i, j, k: (layer, k, j))
    in_specs = [pl.BlockSpec((tm, tk), lambda i, j, k: (i, k)), w_spec]
    args = [a, w]
    if res is not None:
        in_specs.append(pl.BlockSpec((tm, tn), lambda i, j, k: (i, j)))
        args.append(res)
    return pl.pallas_call(
        functools.partial(_mm_kernel, nk=nk, has_res=res is not None),
        grid=(m // tm, n // tn, nk),
        in_specs=in_specs,
        out_specs=pl.BlockSpec((tm, tn), lambda i, j, k: (i, j)),
        out_shape=jax.ShapeDtypeStruct((m, n), out_dtype),
        scratch_shapes=[pltpu.VMEM((tm, tn), F32)] if nk > 1 else [],
        compiler_params=_cparams("parallel", "parallel", "arbitrary"),
        name=name,
    )(*args)


def _swiglu_kernel(h_ref, wg_ref, wu_ref, o_ref):
    h = h_ref[...]
    hg = _dot(h, wg_ref[...])
    hu = _dot(h, wu_ref[...])
    o_ref[...] = (hg * _sigmoid(hg) * hu).astype(o_ref.dtype)


def swiglu_call(h, wg, wu, layer, tm=1024, tn=512):
    t, d = h.shape
    f = wg.shape[-1]
    assert f % tn == 0 and t % tm == 0
    return pl.pallas_call(
        _swiglu_kernel,
        grid=(t // tm, f // tn),
        in_specs=[
            pl.BlockSpec((tm, d), lambda i, j: (i, 0)),
            pl.BlockSpec((None, d, tn), lambda i, j: (layer, 0, j)),
            pl.BlockSpec((None, d, tn), lambda i, j: (layer, 0, j)),
        ],
        out_specs=pl.BlockSpec((tm, tn), lambda i, j: (i, j)),
        out_shape=jax.ShapeDtypeStruct((t, f), BF16),
        compiler_params=_cparams("parallel", "arbitrary"),
        name="swiglu",
    )(h, wg, wu)


def _router_kernel(h_ref, w_ref, b_ref, info_ref, cnt_ref, run_ref, *, tm):
    @pl.when(pl.program_id(0) == 0)
    def _():
        run_ref[...] = jnp.zeros_like(run_ref)

    logits = _dot(h_ref[...], w_ref[...]) + b_ref[...]
    lane = lax.broadcasted_iota(jnp.int32, logits.shape, 1)
    logits = jnp.where(lane < N_EXPERTS, logits, MASKED)
    v1 = jnp.max(logits, axis=-1, keepdims=True)
    i1 = jnp.min(jnp.where(logits == v1, lane, LANES), axis=-1, keepdims=True)
    rest = jnp.where(lane == i1, MASKED, logits)
    v2 = jnp.max(rest, axis=-1, keepdims=True)
    i2 = jnp.min(jnp.where(rest == v2, lane, LANES), axis=-1, keepdims=True)
    e2 = jnp.exp(v2 - v1)
    inv = 1.0 / (1.0 + e2)
    chosen = jnp.where((lane == i1) | (lane == i2), 1.0, 0.0)
    earlier = lax.broadcasted_iota(jnp.int32, (tm, tm), 1) < lax.broadcasted_iota(jnp.int32, (tm, tm), 0)
    rank = _dot(jnp.where(earlier, 1.0, 0.0).astype(BF16), chosen.astype(BF16)) + run_ref[...]
    run_ref[...] += jnp.sum(chosen, axis=0, keepdims=True)
    cnt_ref[...] = run_ref[...]
    r1 = jnp.sum(jnp.where(lane == i1, rank, 0.0), axis=-1, keepdims=True)
    r2 = jnp.sum(jnp.where(lane == i2, rank, 0.0), axis=-1, keepdims=True)
    cols = (i1.astype(F32), i2.astype(F32), inv, e2 * inv, r1, r2)
    info = jnp.zeros(logits.shape, F32)
    for c, col in enumerate(cols):
        info = jnp.where(lane == c, col, info)
    info_ref[...] = info


def router_call(h, w, b, tm=1024):
    t, d = h.shape
    return pl.pallas_call(
        functools.partial(_router_kernel, tm=tm),
        grid=(t // tm,),
        in_specs=[pl.BlockSpec((tm, d), lambda i: (i, 0)), pl.BlockSpec((d, LANES), lambda i: (0, 0)),
                  pl.BlockSpec((1, LANES), lambda i: (0, 0))],
        out_specs=[pl.BlockSpec((tm, LANES), lambda i: (i, 0)), pl.BlockSpec((1, LANES), lambda i: (0, 0))],
        out_shape=[jax.ShapeDtypeStruct((t, LANES), F32), jax.ShapeDtypeStruct((1, LANES), F32)],
        scratch_shapes=[pltpu.VMEM((1, LANES), F32)],
        compiler_params=_cparams("arbitrary"),
        name="router",
    )(h, w, b)


def _pack_bf16_pairs(lo, hi):
    lo_bits = pltpu.bitcast(lo.astype(BF16).astype(F32), jnp.uint32)
    hi_bits = pltpu.bitcast(hi.astype(BF16).astype(F32), jnp.uint32)
    return jnp.right_shift(lo_bits, jnp.uint32(16)) | (hi_bits & jnp.uint32(0xFFFF0000))


def _unpack_bf16_pairs(u):
    lo = pltpu.bitcast(jnp.left_shift(u, jnp.uint32(16)), F32)
    hi = pltpu.bitcast(u & jnp.uint32(0xFFFF0000), F32)
    return lo, hi


def _row_copy(src_ref, src_row, dst_ref, dst_row, sem):
    return pltpu.make_async_copy(src_ref.at[pl.ds(src_row, 1), :], dst_ref.at[pl.ds(dst_row, 1), :], sem)


def _dispatch_kernel(row_token_ref, hp_ref, xs_ref, sem, *, tt):
    i = pl.program_id(0)

    def start(t, c):
        _row_copy(hp_ref, row_token_ref[i * tt + t], xs_ref, t, sem).start()
        return c

    def wait(t, c):
        _row_copy(hp_ref, 0, xs_ref, 0, sem).wait()
        return c

    lax.fori_loop(0, tt, start, 0)
    lax.fori_loop(0, tt, wait, 0)


def moe_dispatch_call(row_token, hp, tt=256):
    n_rows = row_token.shape[0]
    half = hp.shape[1]
    return pl.pallas_call(
        functools.partial(_dispatch_kernel, tt=tt),
        grid_spec=pltpu.PrefetchScalarGridSpec(
            num_scalar_prefetch=1,
            grid=(n_rows // tt,),
            in_specs=[pl.BlockSpec(memory_space=pl.ANY)],
            out_specs=pl.BlockSpec((tt, half), lambda i, rt: (i, 0)),
            scratch_shapes=[pltpu.SemaphoreType.DMA(())],
        ),
        out_shape=jax.ShapeDtypeStruct((n_rows, half), jnp.uint32),
        compiler_params=_cparams("arbitrary"),
        name="moe_dispatch",
    )(row_token, hp)


def _moe_ffn_kernel(te_ref, nv_ref, xs_ref, wg_ref, wu_ref, wd_ref, y_ref, xb_ref, acc_ref, *, nj, half):
    r = pl.program_id(0)
    j = pl.program_id(1)
    valid = r < nv_ref[0]
    last = j == nj - 1

    @pl.when(valid & (j == 0))
    def _():
        lo, hi = _unpack_bf16_pairs(xs_ref[...])
        xb_ref[:, :half] = lo.astype(BF16)
        xb_ref[:, half:] = hi.astype(BF16)

    @pl.when(valid)
    def _():
        xb = xb_ref[...]
        hg = _dot(xb, wg_ref[...])
        hu = _dot(xb, wu_ref[...])
        part = _dot((hg * _sigmoid(hg) * hu).astype(BF16), wd_ref[...])

        @pl.when(j == 0)
        def _():
            acc_ref[...] = part

        @pl.when(j > 0)
        def _():
            acc_ref[...] += part

    @pl.when(valid & last)
    def _():
        y_ref[...] = _pack_bf16_pairs(acc_ref[:, :half], acc_ref[:, half:])

    @pl.when(jnp.logical_not(valid) & last)
    def _():
        y_ref[...] = jnp.zeros_like(y_ref)


def moe_ffn_call(tile_expert, n_valid, xs, wg, wu, wd, layer, tm, tf=256):
    n_rows, half = xs.shape
    d, f = wg.shape[-2:]
    nj = f // tf
    assert f % tf == 0 and n_rows % tm == 0 and d == 2 * half

    def clamp(r, nv):
        return jnp.minimum(r, nv[0] - 1)

    def jj(r, j, nv):
        return jnp.where(r < nv[0], j, nj - 1)

    return pl.pallas_call(
        functools.partial(_moe_ffn_kernel, nj=nj, half=half),
        grid_spec=pltpu.PrefetchScalarGridSpec(
            num_scalar_prefetch=2,
            grid=(n_rows // tm, nj),
            in_specs=[
                pl.BlockSpec((tm, half), lambda r, j, te, nv: (clamp(r, nv), 0)),
                pl.BlockSpec((None, None, d, tf), lambda r, j, te, nv: (layer, te[clamp(r, nv)], 0, jj(r, j, nv))),
                pl.BlockSpec((None, None, d, tf), lambda r, j, te, nv: (layer, te[clamp(r, nv)], 0, jj(r, j, nv))),
                pl.BlockSpec((None, None, tf, d), lambda r, j, te, nv: (layer, te[clamp(r, nv)], jj(r, j, nv), 0)),
            ],
            out_specs=pl.BlockSpec((tm, half), lambda r, j, te, nv: (r, 0)),
            scratch_shapes=[pltpu.VMEM((tm, d), BF16), pltpu.VMEM((tm, d), F32)],
        ),
        out_shape=jax.ShapeDtypeStruct((n_rows, half), jnp.uint32),
        compiler_params=_cparams("arbitrary", "arbitrary"),
        name="moe_ffn",
    )(tile_expert, n_valid, xs, wg, wu, wd)


def _combine_kernel(dest_ref, y_ref, x_ref, info_ref, o_ref, buf_ref, sem, *, tt, half):
    i = pl.program_id(0)

    def start(t, c):
        tok = i * tt + t
        for k in range(TOP_K):
            _row_copy(y_ref, dest_ref[TOP_K * tok + k], buf_ref.at[k], t, sem).start()
        return c

    def wait(t, c):
        for k in range(TOP_K):
            _row_copy(y_ref, 0, buf_ref.at[k], 0, sem).wait()
        return c

    lax.fori_loop(0, tt, start, 0)
    lax.fori_loop(0, tt, wait, 0)
    info = info_ref[...]
    lo_sum = x_ref[:, :half]
    hi_sum = x_ref[:, half:]
    for k in range(TOP_K):
        wk = info[:, 2 + k:3 + k]
        lo, hi = _unpack_bf16_pairs(buf_ref[k])
        lo_sum = lo_sum + wk * lo
        hi_sum = hi_sum + wk * hi
    o_ref[:, :half] = lo_sum
    o_ref[:, half:] = hi_sum


def moe_combine_call(dest, y, x, info, tt=256):
    t, d = x.shape
    half = d // 2
    return pl.pallas_call(
        functools.partial(_combine_kernel, tt=tt, half=half),
        grid_spec=pltpu.PrefetchScalarGridSpec(
            num_scalar_prefetch=1,
            grid=(t // tt,),
            in_specs=[pl.BlockSpec(memory_space=pl.ANY),
                      pl.BlockSpec((tt, d), lambda i, dest_: (i, 0)),
                      pl.BlockSpec((tt, LANES), lambda i, dest_: (i, 0))],
            out_specs=pl.BlockSpec((tt, d), lambda i, dest_: (i, 0)),
            scratch_shapes=[pltpu.VMEM((TOP_K, tt, half), jnp.uint32), pltpu.SemaphoreType.DMA(())],
        ),
        out_shape=jax.ShapeDtypeStruct((t, d), F32),
        compiler_params=_cparams("arbitrary"),
        name="moe_combine",
    )(dest, y, x, info)


def moe_layer(xf, h, hp, router_w, router_b, wg, wu, wd, layer, tm=512):
    t, d = xf.shape
    ne = wg.shape[1]
    info, cnt = router_call(h, router_w, router_b)
    experts = info[:, 0:TOP_K].astype(jnp.int32)
    ranks = info[:, 4:4 + TOP_K].astype(jnp.int32)
    counts = cnt[0, :ne].astype(jnp.int32)
    padded = ((counts + tm - 1) // tm) * tm
    ends = jnp.cumsum(padded)
    dest = ((ends - padded)[experts] + ranks).reshape(-1)
    n_tiles = (t * TOP_K) // tm + ne
    tile_expert = jnp.minimum(jnp.sum(jnp.arange(n_tiles)[:, None] * tm >= ends[None, :], axis=-1), ne - 1).astype(jnp.int32)
    n_valid = (ends[-1:] // tm).astype(jnp.int32)
    slot_token = jnp.repeat(jnp.arange(t, dtype=jnp.int32), TOP_K)
    row_token = jnp.zeros((n_tiles * tm,), jnp.int32).at[dest].set(slot_token, unique_indices=True)
    xs = moe_dispatch_call(row_token, hp)
    y = moe_ffn_call(tile_expert, n_valid, xs, wg, wu, wd, layer, tm)
    return moe_combine_call(dest, y, xf, info)


def _pool_kernel(cur_ref, halo_ref, w_ref, scale_ref, o_ref, xs_ref, *, ts, halo):
    i = pl.program_id(1)
    prev = halo_ref[...].astype(F32)
    xs_ref[0:halo, :] = jnp.where(i == 0, 0.0, prev)
    xs_ref[halo:halo + ts, :] = cur_ref[...].astype(F32)
    row = lax.broadcasted_iota(jnp.int32, (ts, POOL_GW), 0) + i * ts + 1
    for gi, w in enumerate(POOL_WINDOWS):
        c0 = gi * POOL_GW
        x = xs_ref[halo:halo + ts, c0:c0 + POOL_GW]
        wsum = x
        for k in range(1, w):
            wsum = wsum + xs_ref[halo - k:halo - k + ts, c0:c0 + POOL_GW]
        cnt = jnp.minimum(row, w).astype(F32)
        diff = (wsum / cnt - x).astype(BF16)
        y = _dot(diff, w_ref[gi]) * scale_ref[:, c0:c0 + POOL_GW]
        o_ref[:, c0:c0 + POOL_GW] = y.astype(o_ref.dtype)


def pool_call(p, w_pool, scale, batch, ts=512):
    t = p.shape[0]
    seq = t // batch
    nts = seq // ts
    halo = 16
    cb = A0 // MIX_W
    return pl.pallas_call(
        functools.partial(_pool_kernel, ts=ts, halo=halo),
        grid=(batch, nts),
        in_specs=[
            pl.BlockSpec((ts, MIX_W), lambda b, i: (b * nts + i, cb)),
            pl.BlockSpec((halo, MIX_W), lambda b, i: (jnp.maximum((b * nts + i) * (ts // halo) - 1, 0), cb)),
            pl.BlockSpec((len(POOL_WINDOWS), POOL_GW, POOL_GW), lambda b, i: (0, 0, 0)),
            pl.BlockSpec((1, MIX_W), lambda b, i: (0, 0)),
        ],
        out_specs=pl.BlockSpec((ts, MIX_W), lambda b, i: (b * nts + i, 0)),
        out_shape=jax.ShapeDtypeStruct((t, MIX_W), BF16),
        scratch_shapes=[pltpu.VMEM((halo + ts, MIX_W), F32)],
        compiler_params=_cparams("parallel", "arbitrary"),
        name="pool",
    )(p, p, w_pool, scale)


def _layernorm(v, g, b):
    mu = jnp.mean(v, axis=-1, keepdims=True)
    vc = v - mu
    var = jnp.mean(vc * vc, axis=-1, keepdims=True)
    return (vc * lax.rsqrt(var + EPS)) * g + b


def _gmlp_kernel(z_ref, g_ref, b_ref, ws_ref, bs_ref, o_ref, *, ts):
    z = _gelu_tanh(z_ref[...].astype(F32))
    u = z[:, :MIX_W]
    v = _layernorm(z[:, MIX_W:], g_ref[...], b_ref[...]).astype(BF16)
    ri = lax.broadcasted_iota(jnp.int32, (GMLP_CHUNK, GMLP_CHUNK), 0)
    ci = lax.broadcasted_iota(jnp.int32, (GMLP_CHUNK, GMLP_CHUNK), 1)
    tri = ci <= ri
    for h in range(MIX_W // HEAD_DIM):
        w = jnp.where(tri, ws_ref[h], 0.0).astype(BF16)
        bias = bs_ref[:, h * HEAD_DIM:(h + 1) * HEAD_DIM]
        for c in range(ts // GMLP_CHUNK):
            rows = slice(c * GMLP_CHUNK, (c + 1) * GMLP_CHUNK)
            cols = slice(h * HEAD_DIM, (h + 1) * HEAD_DIM)
            s = _dot(w, v[rows, cols]) + bias
            o_ref[rows, cols] = (u[rows, cols] * s).astype(o_ref.dtype)


def gmlp_call(p, ln_g, ln_b, ws, bs_b, ts=512):
    t = p.shape[0]
    return pl.pallas_call(
        functools.partial(_gmlp_kernel, ts=ts),
        grid=(t // ts,),
        in_specs=[
            pl.BlockSpec((ts, 2 * MIX_W), lambda i: (i, C0 // (2 * MIX_W))),
            pl.BlockSpec((1, MIX_W), lambda i: (0, 0)),
            pl.BlockSpec((1, MIX_W), lambda i: (0, 0)),
            pl.BlockSpec(ws.shape, lambda i: (0, 0, 0)),
            pl.BlockSpec(bs_b.shape, lambda i: (0, 0)),
        ],
        out_specs=pl.BlockSpec((ts, MIX_W), lambda i: (i, 0)),
        out_shape=jax.ShapeDtypeStruct((t, MIX_W), BF16),
        compiler_params=_cparams("parallel"),
        name="gmlp",
    )(p, ln_g, ln_b, ws, bs_b)


def _conv_kernel(cur_ref, halo_ref, w_ref, b_ref, g_ref, beta_ref, o_ref, hs_ref, sh_ref, *, ts, halo):
    i = pl.program_id(1)

    def glu(z):
        zf = z.astype(F32)
        return zf[:, :MIX_W] * _sigmoid(zf[:, MIX_W:])

    hs_ref[0:halo, :] = jnp.where(i == 0, 0.0, glu(halo_ref[...]))
    hs_ref[halo:halo + ts, :] = glu(cur_ref[...])
    sub = 8
    shift_rows = sh_ref.shape[1]
    for s in range(1, sub):
        sh_ref[s - 1] = hs_ref[s:s + shift_rows, :]
    off = halo - (CONV_W - 1)
    acc = jnp.zeros((ts, MIX_W), F32) + b_ref[...]
    for k in range(CONV_W):
        s = (off + k) % sub
        base = off + k - s
        rows = hs_ref[base:base + ts, :] if s == 0 else sh_ref[s - 1, base:base + ts, :]
        acc = acc + w_ref[k:k + 1, :] * rows
    y = _layernorm(acc, g_ref[...], beta_ref[...])
    o_ref[...] = (y * _sigmoid(y)).astype(o_ref.dtype)


def conv_call(p, w, b, ln_g, ln_b, batch, ts=256):
    t = p.shape[0]
    seq = t // batch
    nts = seq // ts
    halo = 32
    cb = D0 // (2 * MIX_W)
    return pl.pallas_call(
        functools.partial(_conv_kernel, ts=ts, halo=halo),
        grid=(batch, nts),
        in_specs=[
            pl.BlockSpec((ts, 2 * MIX_W), lambda b_, i: (b_ * nts + i, cb)),
            pl.BlockSpec((halo, 2 * MIX_W), lambda b_, i: (jnp.maximum((b_ * nts + i) * (ts // halo) - 1, 0), cb)),
            pl.BlockSpec((CONV_W, MIX_W), lambda b_, i: (0, 0)),
            pl.BlockSpec((1, MIX_W), lambda b_, i: (0, 0)),
            pl.BlockSpec((1, MIX_W), lambda b_, i: (0, 0)),
            pl.BlockSpec((1, MIX_W), lambda b_, i: (0, 0)),
        ],
        out_specs=pl.BlockSpec((ts, MIX_W), lambda b_, i: (b_ * nts + i, 0)),
        out_shape=jax.ShapeDtypeStruct((t, MIX_W), BF16),
        scratch_shapes=[pltpu.VMEM((halo + ts, MIX_W), F32), pltpu.VMEM((7, halo + ts - 8, MIX_W), F32)],
        compiler_params=_cparams("parallel", "arbitrary"),
        name="conv",
    )(p, p, w, b, ln_g, ln_b)


def _head_rms(x, g):
    xf = x.astype(F32)
    r = lax.rsqrt(jnp.mean(xf * xf, axis=-1, keepdims=True) + EPS)
    return (xf * r) * g


def _nsa_prep_kernel(q_ref, ks_ref, vs_ref, kw_ref, vw_ref, qg_ref, kg_ref, qo_ref, kso_ref, kwo_ref, vso_ref, vwo_ref, *, tr):
    qg = qg_ref[...]
    kg = kg_ref[...]
    for h in range(NSA_HEADS):
        cols = slice(h * HEAD_DIM, (h + 1) * HEAD_DIM)
        qo_ref[:, cols] = (_head_rms(q_ref[:, cols], qg) * (HEAD_DIM ** -0.5)).astype(qo_ref.dtype)
    ones = jnp.ones((VT_ROWS - HEAD_DIM, Q_BLOCK), vso_ref.dtype)
    for g in range(NSA_KV):
        cols = slice(g * HEAD_DIM, (g + 1) * HEAD_DIM)
        kso_ref[:, cols] = _head_rms(ks_ref[:, cols], kg).astype(kso_ref.dtype)
        kwo_ref[:, cols] = _head_rms(kw_ref[:, cols], kg).astype(kwo_ref.dtype)
        for u in range(tr // Q_BLOCK):
            rows = slice(u * Q_BLOCK, (u + 1) * Q_BLOCK)
            for v_ref, vo_ref in ((vs_ref, vso_ref), (vw_ref, vwo_ref)):
                vo_ref[g, u, 0:HEAD_DIM, :] = v_ref[rows, cols].astype(F32).T.astype(vo_ref.dtype)
                vo_ref[g, u, HEAD_DIM:VT_ROWS, :] = ones


def nsa_prep_call(p, q_g, k_g, batch, tr=512):
    t = p.shape[0]
    seq = t // batch
    nst = seq // tr
    kvw = NSA_KV * HEAD_DIM
    kv_spec = lambda c: pl.BlockSpec((tr, kvw), lambda i: (i, (KV0 + c * kvw) // kvw))
    vt_sds = jax.ShapeDtypeStruct((batch * NSA_KV, seq // Q_BLOCK, VT_ROWS, Q_BLOCK), BF16)
    vt_spec = pl.BlockSpec((NSA_KV, tr // Q_BLOCK, VT_ROWS, Q_BLOCK), lambda i: (i // nst, i % nst, 0, 0))
    return pl.pallas_call(
        functools.partial(_nsa_prep_kernel, tr=tr),
        grid=(t // tr,),
        in_specs=[
            pl.BlockSpec((tr, MIX_W), lambda i: (i, Q0 // MIX_W)),
            kv_spec(2), kv_spec(3), kv_spec(4), kv_spec(5),
            pl.BlockSpec((1, HEAD_DIM), lambda i: (0, 0)),
            pl.BlockSpec((1, HEAD_DIM), lambda i: (0, 0)),
        ],
        out_specs=[
            pl.BlockSpec((tr, MIX_W), lambda i: (i, 0)),
            pl.BlockSpec((tr, kvw), lambda i: (i, 0)),
            pl.BlockSpec((tr, kvw), lambda i: (i, 0)),
            vt_spec, vt_spec,
        ],
        out_shape=[jax.ShapeDtypeStruct((t, MIX_W), BF16), jax.ShapeDtypeStruct((t, kvw), BF16),
                   jax.ShapeDtypeStruct((t, kvw), BF16), vt_sds, vt_sds],
        compiler_params=_cparams("parallel"),
        name="nsa_prep",
    )(p, p, p, p, p, q_g, k_g)


def _compress_kernel(ck_ref, cv_ref, pk_ref, pv_ref, w1k_ref, w2k_ref, w1v_ref, w2v_ref, kg_ref, ko_ref, vo_ref, *, nc):
    def comp(c_ref, p_ref, w1_ref, w2_ref):
        c = c_ref[...].astype(F32)
        xa = (c + p_ref[0:1, :]).astype(BF16)
        xb = (c + p_ref[1:2, :]).astype(BF16)
        first = _dot(xa, w1_ref[0])
        second = _dot(xb, w1_ref[1])
        hdn = _gelu_tanh(first + pltpu.roll(second, nc - 1, 0))
        return _dot(hdn.astype(BF16), w2_ref[...])

    row = lax.broadcasted_iota(jnp.int32, (nc, HEAD_DIM), 0)
    real = row < nc - 1
    kc = _head_rms(comp(ck_ref, pk_ref, w1k_ref, w2k_ref), kg_ref[...])
    vc = comp(cv_ref, pv_ref, w1v_ref, w2v_ref)
    zeros = jnp.zeros((KC_PAD, HEAD_DIM), F32)
    ko_ref[0:KC_PAD, :] = zeros
    vo_ref[0:KC_PAD, :] = zeros
    ko_ref[KC_PAD:KC_PAD + nc, :] = jnp.where(real, kc, 0.0)
    vo_ref[KC_PAD:KC_PAD + nc, :] = jnp.where(real, vc, 0.0)


def compress_call(ck, cv, pos_k, pos_v, w1k, w2k, w1v, w2v, k_g):
    bg, nc, half = ck.shape
    full2 = lambda shape: pl.BlockSpec(shape, lambda i: (0,) * len(shape))
    out_sds = jax.ShapeDtypeStruct((bg, KC_PAD + nc, HEAD_DIM), F32)
    return pl.pallas_call(
        functools.partial(_compress_kernel, nc=nc),
        grid=(bg,),
        in_specs=[
            pl.BlockSpec((None, nc, half), lambda i: (i, 0, 0)),
            pl.BlockSpec((None, nc, half), lambda i: (i, 0, 0)),
            full2((2, half)), full2((2, half)),
            full2((2, half, HEAD_DIM)), full2((HEAD_DIM, HEAD_DIM)),
            full2((2, half, HEAD_DIM)), full2((HEAD_DIM, HEAD_DIM)),
            full2((1, HEAD_DIM)),
        ],
        out_specs=[pl.BlockSpec((None, KC_PAD + nc, HEAD_DIM), lambda i: (i, 0, 0))] * 2,
        out_shape=[out_sds, out_sds],
        compiler_params=_cparams("parallel"),
        name="nsa_compress",
    )(ck, cv, pos_k, pos_v, w1k, w2k, w1v, w2v, k_g)


def _softmax_rows(parts):
    m = parts[0].max(axis=-1, keepdims=True)
    for x in parts[1:]:
        m = jnp.maximum(m, x.max(axis=-1, keepdims=True))
    m = jnp.maximum(m, -1e20)
    ps = [jnp.exp(x - m) for x in parts]
    s = ps[0].sum(axis=-1, keepdims=True)
    for p in ps[1:]:
        s = s + p.sum(axis=-1, keepdims=True)
    inv = 1.0 / jnp.maximum(s, 1e-30)
    return ps, inv


def _split_hi_lo(x):
    hi = x.astype(BF16)
    lo = (x - hi.astype(F32)).astype(BF16)
    return hi, lo


def _nsa_kernel(q_ref, kc_ref, vc_ref, ks_ref, vs_ref, kw_ref, vw_ref, ng_ref,
                bcn_ref, bs_ref, bw_ref, ov_ref, e_ref, o_ref, m_ref, acc_ref, ow_ref):
    i = pl.program_id(2)
    qb = Q_BLOCK
    rows = NSA_HPG * qb
    hd = HEAD_DIM
    qs = jnp.concatenate([q_ref[:, r * hd:(r + 1) * hd] for r in range(NSA_HPG)], axis=0)
    kcp = kc_ref.shape[0]

    nwin = WINDOW // qb + 1
    wtiles = [jnp.maximum(i - (nwin - 1 - tt), 0) for tt in range(nwin)]
    k_win = jnp.concatenate([kw_ref[pl.ds(pl.multiple_of(wt * qb, qb), qb), :] for wt in wtiles], axis=0)
    ltw = _dot_nt(k_win, qs) + bw_ref[...]
    wrow = lax.broadcasted_iota(jnp.int32, ltw.shape, 0)
    ltw = jnp.where(wrow >= (nwin - 1 - i) * qb, ltw, MASKED)
    mw = jnp.max(ltw, axis=0, keepdims=True)
    pw = jnp.exp(ltw - mw).astype(BF16)
    acc_w = _dot(jnp.concatenate([vw_ref[wt] for wt in wtiles], axis=1), pw)
    ow_ref[...] = acc_w[0:hd] * (1.0 / jnp.maximum(acc_w[hd:hd + 1], 1e-30))

    near0 = pl.multiple_of(8 * i + 8, 8)
    lf = _dot_nt(qs, kc_ref[...].astype(BF16))
    npad = lax.broadcasted_iota(jnp.int32, (rows, kcp), 1)
    lf = jnp.where((npad >= KC_PAD) & (npad < near0), lf, MASKED)
    ln = _dot_nt(qs, kc_ref[pl.ds(near0, qb), :].astype(BF16)) + bcn_ref[...]
    ncol = lax.broadcasted_iota(jnp.int32, (rows, qb), 1)
    ln = jnp.where(ncol >= KC_PAD - 8 - 8 * i, ln, MASKED)
    (pf, pn), inv = _softmax_rows([lf, ln])
    pf = pf * inv
    pn = pn * inv
    o_c = _dot(pf.astype(BF16), vc_ref[...].astype(BF16)) + _dot(pn.astype(BF16), vc_ref[pl.ds(near0, qb), :].astype(BF16))

    pf_g = pf[0:qb]
    pn_g = pn[0:qb]
    for r in range(1, NSA_HPG):
        pf_g = pf_g + pf[r * qb:(r + 1) * qb]
        pn_g = pn_g + pn[r * qb:(r + 1) * qb]
    ov_all = ov_ref[...].astype(BF16)
    ov_near = ov_ref[pl.ds(near0, qb), :].astype(BF16)
    imp = jnp.zeros((qb, ov_ref.shape[1]), F32)
    for part in _split_hi_lo(pf_g):
        imp = imp + _dot(part, ov_all)
    for part in _split_hi_lo(pn_g):
        imp = imp + _dot(part, ov_near)
    nblk = imp.shape[1]
    jj = lax.broadcasted_iota(jnp.int32, (qb, nblk), 1)
    tq = lax.broadcasted_iota(jnp.int32, (qb, nblk), 0) + i * qb
    cur = jnp.right_shift(tq, 6)
    forced = (jj == 0) | (jj == cur) | (jj == cur - 1)
    score = jnp.where(jj <= cur, imp + jnp.where(forced, FORCE_BONUS, 0.0), NEG)
    sc = score.T
    jrow = lax.broadcasted_iota(jnp.int32, sc.shape, 0).astype(F32)
    sel_t = jnp.zeros(sc.shape, F32)
    for _ in range(min(SLC_TOPK, nblk)):
        mx = jnp.max(sc, axis=0, keepdims=True)
        first = jnp.min(jnp.where(sc == mx, jrow, float(nblk)), axis=0, keepdims=True)
        pick = jrow == first
        sel_t = jnp.where(pick, 1.0, sel_t)
        sc = jnp.where(pick, -3e38, sc)
    sel = sel_t.T

    sel_neg = jnp.where(sel > 0.5, 0.0, MASKED).astype(BF16)
    q_aug = jnp.concatenate([qs, jnp.concatenate([sel_neg] * NSA_HPG, axis=0)], axis=1)

    def block_onehot(e_rows, valid):
        return jnp.where(valid, e_rows, jnp.ones_like(e_rows))

    prev_t = jnp.maximum(i - 1, 0)
    prev0 = pl.multiple_of(prev_t * qb, qb)
    diag0 = pl.multiple_of(i * qb, qb)
    k_near = jnp.concatenate([ks_ref[pl.ds(prev0, qb), :], ks_ref[pl.ds(diag0, qb), :]], axis=0)
    e_near = jnp.concatenate([e_ref[pl.ds(prev0, qb), :], e_ref[pl.ds(diag0, qb), :]], axis=0)
    row_near = lax.broadcasted_iota(jnp.int32, e_near.shape, 0)
    e_near = block_onehot(e_near, row_near >= jnp.where(i == 0, qb, 0))
    lt = _dot_nt(jnp.concatenate([k_near, e_near], axis=1), q_aug) + bs_ref[...]
    m0 = jnp.maximum(jnp.max(lt, axis=0, keepdims=True), -1e20)
    p0 = jnp.exp(lt - m0).astype(BF16)
    vt_near = jnp.concatenate([vs_ref[prev_t], vs_ref[i]], axis=1)
    m_ref[...] = m0
    acc_ref[...] = _dot(vt_near, p0)

    far_end = (i - 1) * qb
    tiles = 4
    chunk = tiles * qb
    halves = 2

    def far_body(c, carry):
        lts, vts = [], []
        for hf in range(halves):
            t0 = (c * halves + hf) * tiles
            c0 = pl.multiple_of(t0 * qb, chunk)
            e_c = e_ref[pl.ds(c0, chunk), :]
            row_c = lax.broadcasted_iota(jnp.int32, e_c.shape, 0) + c0
            k_aug = jnp.concatenate([ks_ref[pl.ds(c0, chunk), :], block_onehot(e_c, row_c < far_end)], axis=1)
            lts.append(_dot_nt(k_aug, q_aug))
            vts.append(jnp.concatenate([vs_ref[t0 + u] for u in range(tiles)], axis=1))
        m_old = m_ref[...]
        m_new = m_old
        for lt_c in lts:
            m_new = jnp.maximum(m_new, jnp.max(lt_c, axis=0, keepdims=True))
        acc = jnp.exp(m_old - m_new) * acc_ref[...]
        for lt_c, vt_c in zip(lts, vts):
            acc = acc + _dot(vt_c, jnp.exp(lt_c - m_new).astype(BF16))
        m_ref[...] = m_new
        acc_ref[...] = acc
        return carry

    lax.fori_loop(0, (i - 1 + halves * tiles - 1) // (halves * tiles), far_body, 0)
    acc_s = acc_ref[...]
    o_s_t = acc_s[0:hd] * (1.0 / jnp.maximum(acc_s[hd:hd + 1], 1e-30))
    o_w_t = ow_ref[...]

    gate = _sigmoid(ng_ref[...].astype(F32))
    gate_t = gate.T
    for r in range(NSA_HPG):
        rs = slice(r * qb, (r + 1) * qb)
        o_sw_t = gate_t[3 * r + 1:3 * r + 2, :] * o_s_t[:, rs] + gate_t[3 * r + 2:3 * r + 3, :] * o_w_t[:, rs]
        o = gate[:, 3 * r:3 * r + 1] * o_c[rs] + o_sw_t.T
        o_ref[:, r * hd:(r + 1) * hd] = o.astype(o_ref.dtype)


def nsa_call(p, qn, ksn, kwn, vst, vwt, kcp, vcp, bcn, bsl_t, bwn_t, ovl, e_tab, batch):
    t = p.shape[0]
    seq = t // batch
    nq = seq // Q_BLOCK
    assert nq % 8 == 0
    g_ = NSA_KV
    kcp_rows = kcp.shape[1]
    rows = NSA_HPG * Q_BLOCK
    ng_blk = NG0 // LANES
    seq_spec = pl.BlockSpec((seq, HEAD_DIM), lambda b, g, i: (b, g))
    vt_spec = pl.BlockSpec((None, nq, VT_ROWS, Q_BLOCK), lambda b, g, i: (b * g_ + g, 0, 0, 0))
    cmp_spec = pl.BlockSpec((None, kcp_rows, HEAD_DIM), lambda b, g, i: (b * g_ + g, 0, 0))
    tbl_spec = lambda arr: pl.BlockSpec((None,) + arr.shape[1:], lambda b, g, i: (g, 0, 0))
    return pl.pallas_call(
        _nsa_kernel,
        grid=(batch, g_, nq),
        in_specs=[
            pl.BlockSpec((Q_BLOCK, rows), lambda b, g, i: (b * nq + i, g)),
            cmp_spec, cmp_spec,
            seq_spec, vt_spec, seq_spec, vt_spec,
            pl.BlockSpec((Q_BLOCK, LANES), lambda b, g, i: (b * nq + i, ng_blk + g)),
            tbl_spec(bcn), tbl_spec(bsl_t), tbl_spec(bwn_t),
            pl.BlockSpec(ovl.shape, lambda b, g, i: (0, 0)),
            pl.BlockSpec(e_tab.shape, lambda b, g, i: (0, 0)),
        ],
        out_specs=pl.BlockSpec((Q_BLOCK, rows), lambda b, g, i: (b * nq + i, g)),
        out_shape=jax.ShapeDtypeStruct((t, MIX_W), BF16),
        scratch_shapes=[pltpu.VMEM((1, rows), F32), pltpu.VMEM((VT_ROWS, rows), F32), pltpu.VMEM((HEAD_DIM, rows), F32)],
        compiler_params=_cparams("parallel", "parallel", "arbitrary"),
        name="nsa_attention",
    )(qn, kcp, vcp, ksn, vst, kwn, vwt, p, bcn, bsl_t, bwn_t, ovl, e_tab)


def _t5_bucket_np(dist):
    n = np.maximum(dist, 0)
    max_exact = REL_BUCKETS // 2
    nf = np.maximum(n, 1).astype(np.float32)
    large = max_exact + (np.log(nf / max_exact) / math.log(REL_MAX_DIST / max_exact)
                         * (REL_BUCKETS - max_exact)).astype(np.int32)
    large = np.minimum(large, REL_BUCKETS - 1)
    return np.where(n < max_exact, n, large)


def _bias_tables(rel_bias, seq):
    qb = Q_BLOCK
    iq = np.arange(qb)[:, None]

    def tile(dist, visible):
        b = rel_bias[_t5_bucket_np(dist)] - rel_bias[REL_BUCKETS - 1][None, None, :]
        b = jnp.where(visible[:, :, None], b, MASKED)
        w = dist.shape[1]
        return b.transpose(2, 0, 1).reshape(NSA_KV, NSA_HPG * qb, w)

    jk = np.arange(qb)[None, :]
    d_diag = iq - jk
    d_prev = qb + iq - jk
    d_edge = WINDOW + iq - jk
    d_far = np.full((qb, qb), 2 * qb)
    d_sel = np.concatenate([d_prev, d_diag], axis=1)
    d_win = np.concatenate([d_edge] + [d_far] * (WINDOW // qb - 2) + [d_prev, d_diag], axis=1)
    d_cmp = iq - CMP_STRIDE * (jk - (KC_PAD - 8)) - (CMP_LEN - 1)
    bsl_t = tile(d_sel, d_sel >= 0).transpose(0, 2, 1)
    bwn_t = tile(d_win, (d_win >= 0) & (d_win < WINDOW)).transpose(0, 2, 1)
    bcn = tile(d_cmp, d_cmp >= 0)
    nc = seq // CMP_STRIDE
    n_slc = seq // SLC_LEN
    c0 = (np.arange(nc) * CMP_STRIDE)[:, None]
    s0 = (np.arange(n_slc) * SLC_LEN)[None, :]
    ov = ((c0 < s0 + SLC_LEN) & (c0 + CMP_LEN > s0) & (np.arange(nc)[:, None] < nc - 1)).astype(np.float32)
    ovl = np.concatenate([np.zeros((KC_PAD, n_slc), np.float32), ov], axis=0)
    e_tab = (np.arange(seq)[:, None] // SLC_LEN == np.arange(n_slc)[None, :]).astype(np.float32)
    return bcn, bsl_t, bwn_t, jnp.asarray(ovl), jnp.asarray(e_tab, dtype=BF16)


def _merge_kernel(g_ref, a_ref, b_ref, c_ref, d_ref, wgu_ref, wbr_ref, o_ref):
    gin = g_ref[...]
    acc = None
    for bi, br in enumerate((a_ref, b_ref, c_ref, d_ref)):
        gate = _sigmoid(_dot(gin, wgu_ref[bi]))
        term = gate * _dot(br[...], wbr_ref[bi])
        acc = term if acc is None else acc + term
    o_ref[...] = acc.astype(o_ref.dtype)


def merge_call(p, branches, wgu, wbr, layer, tm=1024, tn=512):
    t = p.shape[0]
    d = wgu.shape[-1]
    row = lambda w, cb: pl.BlockSpec((tm, w), lambda i, j: (i, cb))
    return pl.pallas_call(
        _merge_kernel,
        grid=(t // tm, d // tn),
        in_specs=[row(GATE_RANK, G0 // GATE_RANK)] + [row(MIX_W, 0)] * 4 + [
            pl.BlockSpec((None, 4, GATE_RANK, tn), lambda i, j: (layer, 0, 0, j)),
            pl.BlockSpec((None, 4, MIX_W, tn), lambda i, j: (layer, 0, 0, j)),
        ],
        out_specs=pl.BlockSpec((tm, tn), lambda i, j: (i, j)),
        out_shape=jax.ShapeDtypeStruct((t, d), BF16),
        compiler_params=_cparams("parallel", "arbitrary"),
        name="merge",
    )(p, *branches, wgu, wbr)


def _rearranged_w_in(w):
    o_a, o_q, o_kv, o_ng, o_c, o_d, o_g = 0, 1024, 2048, 3584, 3608, 5656, 7704
    d = w.shape[0]
    hpg3 = NSA_HPG * 3
    zpad = lambda n: jnp.zeros((d, n), BF16)
    parts = [
        w[:, o_c:o_c + 2048], w[:, o_d:o_d + 2048], w[:, o_a:o_a + 1024],
        w[:, o_q:o_q + 1024], w[:, o_kv:o_kv + 1536], w[:, o_g:o_g + 512],
        w[:, o_ng:o_ng + hpg3], zpad(LANES - hpg3),
        w[:, o_ng + hpg3:o_ng + 2 * hpg3], zpad(LANES - hpg3), zpad(PCOLS - NG0 - 2 * LANES),
    ]
    return jnp.concatenate([part.astype(BF16) for part in parts], axis=-1)


def kernel(x, rel_bias, norm_mix_g, norm_ffn_g, w_in, pool_w, pool_scale, q_norm_g, k_norm_g, cmp_pos_k, cmp_w1_k, cmp_w2_k, cmp_pos_v, cmp_w1_v, cmp_w2_v, gmlp_ln_g, gmlp_ln_b, gmlp_ws, gmlp_bs, conv_w, conv_b, conv_ln_g, conv_ln_b, w_branch, w_gate_up, w_out, ffn_w_gate, ffn_w_up, ffn_w_down, moe_router, moe_router_b, moe_w_gate, moe_w_up, moe_w_down):
    batch, seq, d = x.shape
    t = batch * seq
    depth = w_in.shape[0]
    nc = seq // CMP_STRIDE
    half = CMP_STRIDE * HEAD_DIM
    xf = x.reshape(t, d)

    bcn, bsl_t, bwn_t, ovl, e_tab = _bias_tables(rel_bias, seq)
    row = lambda v: v.reshape(1, -1)
    w_gate_up_b, w_branch_b, w_out_b = (w.astype(BF16) for w in (w_gate_up, w_branch, w_out))
    ffn_w_gate_b, ffn_w_up_b, ffn_w_down_b = (w.astype(BF16) for w in (ffn_w_gate, ffn_w_up, ffn_w_down))
    moe_w_gate_b, moe_w_up_b, moe_w_down_b = (w.astype(BF16) for w in (moe_w_gate, moe_w_up, moe_w_down))

    for l in range(depth):
        h = rmsnorm_call(xf, row(norm_mix_g[l]))
        p = matmul_call(h, _rearranged_w_in(w_in[l]), name="in_proj")
        o_a = pool_call(p, pool_w[l].astype(BF16), row(pool_scale[l]), batch)
        qn, ksn, kwn, vst, vwt = nsa_prep_call(p, row(q_norm_g[l]), row(k_norm_g[l]), batch)
        kvc = p[:, KV0:KV0 + 2 * NSA_KV * HEAD_DIM].reshape(batch, nc, CMP_STRIDE, 2, NSA_KV, HEAD_DIM)
        kvc = kvc.transpose(3, 0, 4, 1, 2, 5).reshape(2, batch * NSA_KV, nc, half)
        kcp, vcp = compress_call(
            kvc[0], kvc[1], cmp_pos_k[l].reshape(2, half), cmp_pos_v[l].reshape(2, half),
            cmp_w1_k[l].reshape(2, half, HEAD_DIM).astype(BF16), cmp_w2_k[l].astype(BF16),
            cmp_w1_v[l].reshape(2, half, HEAD_DIM).astype(BF16), cmp_w2_v[l].astype(BF16), row(k_norm_g[l]))
        o_b = nsa_call(p, qn, ksn, kwn, vst, vwt, kcp, vcp, bcn, bsl_t, bwn_t, ovl, e_tab, batch)
        bs_b = jnp.repeat(gmlp_bs[l].T, HEAD_DIM, axis=1)
        o_c = gmlp_call(p, row(gmlp_ln_g[l]), row(gmlp_ln_b[l]), gmlp_ws[l], bs_b)
        o_d = conv_call(p, conv_w[l], row(conv_b[l]), row(conv_ln_g[l]), row(conv_ln_b[l]), batch)
        mix = merge_call(p, (o_a, o_b, o_c, o_d), w_gate_up_b, w_branch_b, l)
        xf = matmul_call(mix, w_out_b, layer=l, res=xf, out_dtype=F32, name="out_proj")
        i = l // 2
        if l % 2 == 0:
            h = rmsnorm_call(xf, row(norm_ffn_g[l]))
            act = swiglu_call(h, ffn_w_gate_b, ffn_w_up_b, i)
            xf = matmul_call(act, ffn_w_down_b, layer=i, res=xf, out_dtype=F32, name="ffn_down")
        else:
            h, hp = rmsnorm_call(xf, row(norm_ffn_g[l]), packed=True)
            rw = jnp.pad(moe_router[i], ((0, 0), (0, LANES - N_EXPERTS))).astype(BF16)
            rb = jnp.pad(moe_router_b[i], (0, LANES - N_EXPERTS)).reshape(1, LANES)
            xf = moe_layer(xf, h, hp, rw, rb, moe_w_gate_b, moe_w_up_b, moe_w_down_b, i)
    return xf.reshape(batch, seq, d)
```

```python
import functools
import math

import jax
import jax.numpy as jnp
import numpy as np
from jax import lax
from jax.experimental import pallas as pl
from jax.experimental.pallas import tpu as pltpu

F32 = jnp.float32
BF16 = jnp.bfloat16

VMEM_LIMIT_BYTES = 56 * 1024 * 1024
LANES = 128

EPS = 1e-6
HEAD_DIM = 128
MIX_W = 1024
POOL_WINDOWS = (2, 4, 8, 16)
POOL_GW = MIX_W // len(POOL_WINDOWS)
NSA_HEADS = 8
NSA_KV = 2
NSA_HPG = NSA_HEADS // NSA_KV
CMP_LEN = 32
CMP_STRIDE = 16
SLC_LEN = 64
SLC_TOPK = 16
WINDOW = 512
Q_BLOCK = 128
FORCE_BONUS = 1e4
NEG = -1e9
MASKED = -1e30
GMLP_CHUNK = 128
CONV_W = 31
GATE_RANK = 512
REL_BUCKETS = 32
REL_MAX_DIST = 128
N_EXPERTS = 8
TOP_K = 2

C0 = 0
D0 = 2048
A0 = 4096
Q0 = 5120
KV0 = 6144
G0 = 7680
NG0 = 8192
PCOLS = 8704
KC_PAD = 128
VT_ROWS = HEAD_DIM + 16


def _cparams(*sem):
    return pltpu.CompilerParams(dimension_semantics=sem, vmem_limit_bytes=VMEM_LIMIT_BYTES)


def _dot(a, b):
    return jnp.dot(a, b, preferred_element_type=F32)


def _dot_nt(a, b):
    return lax.dot_general(a, b, (((1,), (1,)), ((), ())), preferred_element_type=F32)


def _gelu_tanh(x):
    return 0.5 * x * (1.0 + jnp.tanh(math.sqrt(2.0 / math.pi) * (x + 0.044715 * (x * x * x))))


def _sigmoid(x):
    return 1.0 / (1.0 + jnp.exp(-x))


def _exp_bf16(x):
    return jnp.exp(x.astype(BF16))


def _rmsnorm_kernel(x_ref, g_ref, o_ref, *packed_ref):
    xf = x_ref[...]
    r = lax.rsqrt(jnp.mean(xf * xf, axis=-1, keepdims=True) + EPS)
    y = (xf * r) * g_ref[...]
    o_ref[...] = y.astype(o_ref.dtype)
    if packed_ref:
        half = y.shape[1] // 2
        packed_ref[0][...] = _pack_bf16_pairs(y[:, :half], y[:, half:])


def rmsnorm_call(x, g, tr=256, packed=False):
    t, d = x.shape
    out_specs = [pl.BlockSpec((tr, d), lambda i: (i, 0))]
    out_shape = [jax.ShapeDtypeStruct((t, d), BF16)]
    if packed:
        out_specs.append(pl.BlockSpec((tr, d // 2), lambda i: (i, 0)))
        out_shape.append(jax.ShapeDtypeStruct((t, d // 2), jnp.uint32))
    out = pl.pallas_call(
        _rmsnorm_kernel,
        grid=(t // tr,),
        in_specs=[pl.BlockSpec((tr, d), lambda i: (i, 0)), pl.BlockSpec((1, d), lambda i: (0, 0))],
        out_specs=out_specs,
        out_shape=out_shape,
        compiler_params=_cparams("parallel"),
        name="rmsnorm",
    )(x, g)
    return out if packed else out[0]


def _mm_kernel(*refs, nk, has_res):
    if has_res:
        a_ref, w_ref, r_ref, o_ref = refs[:4]
        scr = refs[4:]
    else:
        a_ref, w_ref, o_ref = refs[:3]
        r_ref = None
        scr = refs[3:]
    part = _dot(a_ref[...], w_ref[...])

    def finish(acc):
        if has_res:
            acc = acc + r_ref[...]
        o_ref[...] = acc.astype(o_ref.dtype)

    if nk == 1:
        finish(part)
    else:
        acc_ref = scr[0]
        k = pl.program_id(2)

        @pl.when(k == 0)
        def _():
            acc_ref[...] = part

        @pl.when(k > 0)
        def _():
            acc_ref[...] += part

        @pl.when(k == nk - 1)
        def _():
            finish(acc_ref[...])


def matmul_call(a, w, layer=None, res=None, out_dtype=BF16, tm=1024, tn=512, tk=4096, name="matmul"):
    m, kdim = a.shape
    n = w.shape[-1]
    tk = min(tk, kdim)
    nk = kdim // tk
    assert m % tm == 0 and n % tn == 0 and kdim % tk == 0
    if layer is None:
        w_spec = pl.BlockSpec((tk, tn), lambda i, j, k: (k, j))
    else:
        w_spec = pl.BlockSpec((None, tk, tn), lambda i, j, k: (layer, k, j))
    in_specs = [pl.BlockSpec((tm, tk), lambda i, j, k: (i, k)), w_spec]
    args = [a, w]
    if res is not None:
        in_specs.append(pl.BlockSpec((tm, tn), lambda i, j, k: (i, j)))
        args.append(res)
    return pl.pallas_call(
        functools.partial(_mm_kernel, nk=nk, has_res=res is not None),
        grid=(m // tm, n // tn, nk),
        in_specs=in_specs,
        out_specs=pl.BlockSpec((tm, tn), lambda i, j, k: (i, j)),
        out_shape=jax.ShapeDtypeStruct((m, n), out_dtype),
        scratch_shapes=[pltpu.VMEM((tm, tn), F32)] if nk > 1 else [],
        compiler_params=_cparams("parallel", "parallel", "arbitrary"),
        name=name,
    )(*args)


def _swiglu_kernel(h_ref, wg_ref, wu_ref, o_ref):
    h = h_ref[...]
    hg = _dot(h, wg_ref[...])
    hu = _dot(h, wu_ref[...])
    o_ref[...] = (hg * _sigmoid(hg) * hu).astype(o_ref.dtype)


def swiglu_call(h, wg, wu, layer, tm=1024, tn=512):
    t, d = h.shape
    f = wg.shape[-1]
    assert f % tn == 0 and t % tm == 0
    return pl.pallas_call(
        _swiglu_kernel,
        grid=(t // tm, f // tn),
        in_specs=[
            pl.BlockSpec((tm, d), lambda i, j: (i, 0)),
            pl.BlockSpec((None, d, tn), lambda i, j: (layer, 0, j)),
            pl.BlockSpec((None, d, tn), lambda i, j: (layer, 0, j)),
        ],
        out_specs=pl.BlockSpec((tm, tn), lambda i, j: (i, j)),
        out_shape=jax.ShapeDtypeStruct((t, f), BF16),
        compiler_params=_cparams("parallel", "arbitrary"),
        name="swiglu",
    )(h, wg, wu)


def _router_kernel(h_ref, w_ref, b_ref, info_ref, cnt_ref, run_ref, *, tm):
    @pl.when(pl.program_id(0) == 0)
    def _():
        run_ref[...] = jnp.zeros_like(run_ref)

    logits = _dot(h_ref[...], w_ref[...]) + b_ref[...]
    lane = lax.broadcasted_iota(jnp.int32, logits.shape, 1)
    logits = jnp.where(lane < N_EXPERTS, logits, MASKED)
    v1 = jnp.max(logits, axis=-1, keepdims=True)
    i1 = jnp.min(jnp.where(logits == v1, lane, LANES), axis=-1, keepdims=True)
    rest = jnp.where(lane == i1, MASKED, logits)
    v2 = jnp.max(rest, axis=-1, keepdims=True)
    i2 = jnp.min(jnp.where(rest == v2, lane, LANES), axis=-1, keepdims=True)
    e2 = jnp.exp(v2 - v1)
    inv = 1.0 / (1.0 + e2)
    chosen = jnp.where((lane == i1) | (lane == i2), 1.0, 0.0)
    earlier = lax.broadcasted_iota(jnp.int32, (tm, tm), 1) < lax.broadcasted_iota(jnp.int32, (tm, tm), 0)
    rank = _dot(jnp.where(earlier, 1.0, 0.0).astype(BF16), chosen.astype(BF16)) + run_ref[...]
    run_ref[...] += jnp.sum(chosen, axis=0, keepdims=True)
    cnt_ref[...] = run_ref[...]
    r1 = jnp.sum(jnp.where(lane == i1, rank, 0.0), axis=-1, keepdims=True)
    r2 = jnp.sum(jnp.where(lane == i2, rank, 0.0), axis=-1, keepdims=True)
    cols = (i1.astype(F32), i2.astype(F32), inv, e2 * inv, r1, r2)
    info = jnp.zeros(logits.shape, F32)
    for c, col in enumerate(cols):
        info = jnp.where(lane == c, col, info)
    info_ref[...] = info


def router_call(h, w, b, tm=1024):
    t, d = h.shape
    return pl.pallas_call(
        functools.partial(_router_kernel, tm=tm),
        grid=(t // tm,),
        in_specs=[pl.BlockSpec((tm, d), lambda i: (i, 0)), pl.BlockSpec((d, LANES), lambda i: (0, 0)),
                  pl.BlockSpec((1, LANES), lambda i: (0, 0))],
        out_specs=[pl.BlockSpec((tm, LANES), lambda i: (i, 0)), pl.BlockSpec((1, LANES), lambda i: (0, 0))],
        out_shape=[jax.ShapeDtypeStruct((t, LANES), F32), jax.ShapeDtypeStruct((1, LANES), F32)],
        scratch_shapes=[pltpu.VMEM((1, LANES), F32)],
        compiler_params=_cparams("arbitrary"),
        name="router",
    )(h, w, b)


def _pack_bf16_pairs(lo, hi):
    lo_bits = pltpu.bitcast(lo.astype(BF16).astype(F32), jnp.uint32)
    hi_bits = pltpu.bitcast(hi.astype(BF16).astype(F32), jnp.uint32)
    return jnp.right_shift(lo_bits, jnp.uint32(16)) | (hi_bits & jnp.uint32(0xFFFF0000))


def _unpack_bf16_pairs(u):
    lo = pltpu.bitcast(jnp.left_shift(u, jnp.uint32(16)), F32)
    hi = pltpu.bitcast(u & jnp.uint32(0xFFFF0000), F32)
    return lo, hi


def _row_copy(src_ref, src_row, dst_ref, dst_row, sem):
    return pltpu.make_async_copy(src_ref.at[pl.ds(src_row, 1), :], dst_ref.at[pl.ds(dst_row, 1), :], sem)


def _dispatch_kernel(row_token_ref, hp_ref, xs_ref, sem, *, tt):
    i = pl.program_id(0)

    def start(t, c):
        _row_copy(hp_ref, row_token_ref[i * tt + t], xs_ref, t, sem).start()
        return c

    def wait(t, c):
        _row_copy(hp_ref, 0, xs_ref, 0, sem).wait()
        return c

    lax.fori_loop(0, tt, start, 0, unroll=8)
    lax.fori_loop(0, tt, wait, 0, unroll=8)


def moe_dispatch_call(row_token, hp, tt=256):
    n_rows = row_token.shape[0]
    half = hp.shape[1]
    return pl.pallas_call(
        functools.partial(_dispatch_kernel, tt=tt),
        grid_spec=pltpu.PrefetchScalarGridSpec(
            num_scalar_prefetch=1,
            grid=(n_rows // tt,),
            in_specs=[pl.BlockSpec(memory_space=pl.ANY)],
            out_specs=pl.BlockSpec((tt, half), lambda i, rt: (i, 0)),
            scratch_shapes=[pltpu.SemaphoreType.DMA(())],
        ),
        out_shape=jax.ShapeDtypeStruct((n_rows, half), jnp.uint32),
        compiler_params=_cparams("arbitrary"),
        name="moe_dispatch",
    )(row_token, hp)


def _moe_ffn_kernel(te_ref, nv_ref, xs_ref, wg_ref, wu_ref, wd_ref, y_ref, xb_ref, acc_ref, *, nj, half):
    r = pl.program_id(0)
    j = pl.program_id(1)
    valid = r < nv_ref[0]
    last = j == nj - 1

    @pl.when(valid & (j == 0))
    def _():
        lo, hi = _unpack_bf16_pairs(xs_ref[...])
        xb_ref[:, :half] = lo.astype(BF16)
        xb_ref[:, half:] = hi.astype(BF16)

    @pl.when(valid)
    def _():
        xb = xb_ref[...]
        hg = _dot(xb, wg_ref[...])
        hu = _dot(xb, wu_ref[...])
        part = _dot((hg * _sigmoid(hg) * hu).astype(BF16), wd_ref[...])

        @pl.when(j == 0)
        def _():
            acc_ref[...] = part

        @pl.when(j > 0)
        def _():
            acc_ref[...] += part

    @pl.when(valid & last)
    def _():
        y_ref[...] = _pack_bf16_pairs(acc_ref[:, :half], acc_ref[:, half:])

    @pl.when(jnp.logical_not(valid) & last)
    def _():
        y_ref[...] = jnp.zeros_like(y_ref)


def moe_ffn_call(tile_expert, n_valid, xs, wg, wu, wd, layer, tm, tf=256):
    n_rows, half = xs.shape
    d, f = wg.shape[-2:]
    nj = f // tf
    assert f % tf == 0 and n_rows % tm == 0 and d == 2 * half

    def clamp(r, nv):
        return jnp.minimum(r, nv[0] - 1)

    def jj(r, j, nv):
        return jnp.where(r < nv[0], j, nj - 1)

    return pl.pallas_call(
        functools.partial(_moe_ffn_kernel, nj=nj, half=half),
        grid_spec=pltpu.PrefetchScalarGridSpec(
            num_scalar_prefetch=2,
            grid=(n_rows // tm, nj),
            in_specs=[
                pl.BlockSpec((tm, half), lambda r, j, te, nv: (clamp(r, nv), 0)),
                pl.BlockSpec((None, None, d, tf), lambda r, j, te, nv: (layer, te[clamp(r, nv)], 0, jj(r, j, nv))),
                pl.BlockSpec((None, None, d, tf), lambda r, j, te, nv: (layer, te[clamp(r, nv)], 0, jj(r, j, nv))),
                pl.BlockSpec((None, None, tf, d), lambda r, j, te, nv: (layer, te[clamp(r, nv)], jj(r, j, nv), 0)),
            ],
            out_specs=pl.BlockSpec((tm, half), lambda r, j, te, nv: (r, 0)),
            scratch_shapes=[pltpu.VMEM((tm, d), BF16), pltpu.VMEM((tm, d), F32)],
        ),
        out_shape=jax.ShapeDtypeStruct((n_rows, half), jnp.uint32),
        compiler_params=_cparams("arbitrary", "arbitrary"),
        name="moe_ffn",
    )(tile_expert, n_valid, xs, wg, wu, wd)


def _combine_kernel(dest_ref, y_ref, x_ref, info_ref, o_ref, buf_ref, sem, *, tt, half):
    i = pl.program_id(0)

    def start(t, c):
        tok = i * tt + t
        for k in range(TOP_K):
            _row_copy(y_ref, dest_ref[TOP_K * tok + k], buf_ref.at[k], t, sem).start()
        return c

    def wait(t, c):
        for k in range(TOP_K):
            _row_copy(y_ref, 0, buf_ref.at[k], 0, sem).wait()
        return c

    lax.fori_loop(0, tt, start, 0, unroll=8)
    lax.fori_loop(0, tt, wait, 0, unroll=8)
    info = info_ref[...]
    lo_sum = x_ref[:, :half]
    hi_sum = x_ref[:, half:]
    for k in range(TOP_K):
        wk = info[:, 2 + k:3 + k]
        lo, hi = _unpack_bf16_pairs(buf_ref[k])
        lo_sum = lo_sum + wk * lo
        hi_sum = hi_sum + wk * hi
    o_ref[:, :half] = lo_sum
    o_ref[:, half:] = hi_sum


def moe_combine_call(dest, y, x, info, tt=256):
    t, d = x.shape
    half = d // 2
    return pl.pallas_call(
        functools.partial(_combine_kernel, tt=tt, half=half),
        grid_spec=pltpu.PrefetchScalarGridSpec(
            num_scalar_prefetch=1,
            grid=(t // tt,),
            in_specs=[pl.BlockSpec(memory_space=pl.ANY),
                      pl.BlockSpec((tt, d), lambda i, dest_: (i, 0)),
                      pl.BlockSpec((tt, LANES), lambda i, dest_: (i, 0))],
            out_specs=pl.BlockSpec((tt, d), lambda i, dest_: (i, 0)),
            scratch_shapes=[pltpu.VMEM((TOP_K, tt, half), jnp.uint32), pltpu.SemaphoreType.DMA(())],
        ),
        out_shape=jax.ShapeDtypeStruct((t, d), F32),
        compiler_params=_cparams("arbitrary"),
        name="moe_combine",
    )(dest, y, x, info)


def moe_layer(xf, h, hp, router_w, router_b, wg, wu, wd, layer, tm=512):
    t, d = xf.shape
    ne = wg.shape[1]
    info, cnt = router_call(h, router_w, router_b)
    experts = info[:, 0:TOP_K].astype(jnp.int32)
    ranks = info[:, 4:4 + TOP_K].astype(jnp.int32)
    counts = cnt[0, :ne].astype(jnp.int32)
    padded = ((counts + tm - 1) // tm) * tm
    ends = jnp.cumsum(padded)
    dest = ((ends - padded)[experts] + ranks).reshape(-1)
    n_tiles = (t * TOP_K) // tm + ne
    tile_expert = jnp.minimum(jnp.sum(jnp.arange(n_tiles)[:, None] * tm >= ends[None, :], axis=-1), ne - 1).astype(jnp.int32)
    n_valid = (ends[-1:] // tm).astype(jnp.int32)
    slot_token = jnp.repeat(jnp.arange(t, dtype=jnp.int32), TOP_K)
    row_token = jnp.zeros((n_tiles * tm,), jnp.int32).at[dest].set(slot_token, unique_indices=True)
    xs = moe_dispatch_call(row_token, hp)
    y = moe_ffn_call(tile_expert, n_valid, xs, wg, wu, wd, layer, tm)
    return moe_combine_call(dest, y, xf, info)


def _pool_kernel(cur_ref, halo_ref, w_ref, scale_ref, o_ref, xs_ref, *, ts, halo):
    i = pl.program_id(1)
    prev = halo_ref[...].astype(F32)
    xs_ref[0:halo, :] = jnp.where(i == 0, 0.0, prev)
    xs_ref[halo:halo + ts, :] = cur_ref[...].astype(F32)
    row = lax.broadcasted_iota(jnp.int32, (ts, POOL_GW), 0) + i * ts + 1
    for gi, w in enumerate(POOL_WINDOWS):
        c0 = gi * POOL_GW
        x = xs_ref[halo:halo + ts, c0:c0 + POOL_GW]
        wsum = x
        for k in range(1, w):
            wsum = wsum + xs_ref[halo - k:halo - k + ts, c0:c0 + POOL_GW]
        cnt = jnp.minimum(row, w).astype(F32)
        diff = (wsum / cnt - x).astype(BF16)
        y = _dot(diff, w_ref[gi]) * scale_ref[:, c0:c0 + POOL_GW]
        o_ref[:, c0:c0 + POOL_GW] = y.astype(o_ref.dtype)


def pool_call(p, w_pool, scale, batch, ts=512):
    t = p.shape[0]
    seq = t // batch
    nts = seq // ts
    halo = 16
    cb = A0 // MIX_W
    return pl.pallas_call(
        functools.partial(_pool_kernel, ts=ts, halo=halo),
        grid=(batch, nts),
        in_specs=[
            pl.BlockSpec((ts, MIX_W), lambda b, i: (b * nts + i, cb)),
            pl.BlockSpec((halo, MIX_W), lambda b, i: (jnp.maximum((b * nts + i) * (ts // halo) - 1, 0), cb)),
            pl.BlockSpec((len(POOL_WINDOWS), POOL_GW, POOL_GW), lambda b, i: (0, 0, 0)),
            pl.BlockSpec((1, MIX_W), lambda b, i: (0, 0)),
        ],
        out_specs=pl.BlockSpec((ts, MIX_W), lambda b, i: (b * nts + i, 0)),
        out_shape=jax.ShapeDtypeStruct((t, MIX_W), BF16),
        scratch_shapes=[pltpu.VMEM((halo + ts, MIX_W), F32)],
        compiler_params=_cparams("parallel", "arbitrary"),
        name="pool",
    )(p, p, w_pool, scale)


def _layernorm(v, g, b):
    mu = jnp.mean(v, axis=-1, keepdims=True)
    vc = v - mu
    var = jnp.mean(vc * vc, axis=-1, keepdims=True)
    return (vc * lax.rsqrt(var + EPS)) * g + b


def _gmlp_kernel(z_ref, g_ref, b_ref, ws_ref, bs_ref, o_ref, *, ts):
    z = _gelu_tanh(z_ref[...].astype(F32))
    u = z[:, :MIX_W]
    v = _layernorm(z[:, MIX_W:], g_ref[...], b_ref[...]).astype(BF16)
    ri = lax.broadcasted_iota(jnp.int32, (GMLP_CHUNK, GMLP_CHUNK), 0)
    ci = lax.broadcasted_iota(jnp.int32, (GMLP_CHUNK, GMLP_CHUNK), 1)
    tri = ci <= ri
    for h in range(MIX_W // HEAD_DIM):
        w = jnp.where(tri, ws_ref[h], 0.0).astype(BF16)
        bias = bs_ref[:, h * HEAD_DIM:(h + 1) * HEAD_DIM]
        for c in range(ts // GMLP_CHUNK):
            rows = slice(c * GMLP_CHUNK, (c + 1) * GMLP_CHUNK)
            cols = slice(h * HEAD_DIM, (h + 1) * HEAD_DIM)
            s = _dot(w, v[rows, cols]) + bias
            o_ref[rows, cols] = (u[rows, cols] * s).astype(o_ref.dtype)


def gmlp_call(p, ln_g, ln_b, ws, bs_b, ts=512):
    t = p.shape[0]
    return pl.pallas_call(
        functools.partial(_gmlp_kernel, ts=ts),
        grid=(t // ts,),
        in_specs=[
            pl.BlockSpec((ts, 2 * MIX_W), lambda i: (i, C0 // (2 * MIX_W))),
            pl.BlockSpec((1, MIX_W), lambda i: (0, 0)),
            pl.BlockSpec((1, MIX_W), lambda i: (0, 0)),
            pl.BlockSpec(ws.shape, lambda i: (0, 0, 0)),
            pl.BlockSpec(bs_b.shape, lambda i: (0, 0)),
        ],
        out_specs=pl.BlockSpec((ts, MIX_W), lambda i: (i, 0)),
        out_shape=jax.ShapeDtypeStruct((t, MIX_W), BF16),
        compiler_params=_cparams("parallel"),
        name="gmlp",
    )(p, ln_g, ln_b, ws, bs_b)


def _conv_kernel(cur_ref, halo_ref, w_ref, b_ref, g_ref, beta_ref, o_ref, hs_ref, sh_ref, *, ts, halo):
    i = pl.program_id(1)

    def glu(z):
        zf = z.astype(F32)
        return zf[:, :MIX_W] * _sigmoid(zf[:, MIX_W:])

    hs_ref[0:halo, :] = jnp.where(i == 0, 0.0, glu(halo_ref[...]))
    hs_ref[halo:halo + ts, :] = glu(cur_ref[...])
    sub = 8
    shift_rows = sh_ref.shape[1]
    for s in range(1, sub):
        sh_ref[s - 1] = hs_ref[s:s + shift_rows, :]
    off = halo - (CONV_W - 1)
    acc = jnp.zeros((ts, MIX_W), F32) + b_ref[...]
    for k in range(CONV_W):
        s = (off + k) % sub
        base = off + k - s
        rows = hs_ref[base:base + ts, :] if s == 0 else sh_ref[s - 1, base:base + ts, :]
        acc = acc + w_ref[k:k + 1, :] * rows
    y = _layernorm(acc, g_ref[...], beta_ref[...])
    o_ref[...] = (y * _sigmoid(y)).astype(o_ref.dtype)


def conv_call(p, w, b, ln_g, ln_b, batch, ts=256):
    t = p.shape[0]
    seq = t // batch
    nts = seq // ts
    halo = 32
    cb = D0 // (2 * MIX_W)
    return pl.pallas_call(
        functools.partial(_conv_kernel, ts=ts, halo=halo),
        grid=(batch, nts),
        in_specs=[
            pl.BlockSpec((ts, 2 * MIX_W), lambda b_, i: (b_ * nts + i, cb)),
            pl.BlockSpec((halo, 2 * MIX_W), lambda b_, i: (jnp.maximum((b_ * nts + i) * (ts // halo) - 1, 0), cb)),
            pl.BlockSpec((CONV_W, MIX_W), lambda b_, i: (0, 0)),
            pl.BlockSpec((1, MIX_W), lambda b_, i: (0, 0)),
            pl.BlockSpec((1, MIX_W), lambda b_, i: (0, 0)),
            pl.BlockSpec((1, MIX_W), lambda b_, i: (0, 0)),
        ],
        out_specs=pl.BlockSpec((ts, MIX_W), lambda b_, i: (b_ * nts + i, 0)),
        out_shape=jax.ShapeDtypeStruct((t, MIX_W), BF16),
        scratch_shapes=[pltpu.VMEM((halo + ts, MIX_W), F32), pltpu.VMEM((7, halo + ts - 8, MIX_W), F32)],
        compiler_params=_cparams("parallel", "arbitrary"),
        name="conv",
    )(p, p, w, b, ln_g, ln_b)


def _head_rms(x, g):
    xf = x.astype(F32)
    r = lax.rsqrt(jnp.mean(xf * xf, axis=-1, keepdims=True) + EPS)
    return (xf * r) * g


def _nsa_prep_kernel(q_ref, ks_ref, vs_ref, kw_ref, vw_ref, qg_ref, kg_ref, qo_ref, kso_ref, kwo_ref, vso_ref, vwo_ref, *, tr):
    qg = qg_ref[...]
    kg = kg_ref[...]
    for h in range(NSA_HEADS):
        cols = slice(h * HEAD_DIM, (h + 1) * HEAD_DIM)
        qo_ref[:, cols] = (_head_rms(q_ref[:, cols], qg) * (HEAD_DIM ** -0.5)).astype(qo_ref.dtype)
    ones = jnp.ones((VT_ROWS - HEAD_DIM, Q_BLOCK), vso_ref.dtype)
    for g in range(NSA_KV):
        cols = slice(g * HEAD_DIM, (g + 1) * HEAD_DIM)
        kso_ref[:, cols] = _head_rms(ks_ref[:, cols], kg).astype(kso_ref.dtype)
        kwo_ref[:, cols] = _head_rms(kw_ref[:, cols], kg).astype(kwo_ref.dtype)
        for u in range(tr // Q_BLOCK):
            rows = slice(u * Q_BLOCK, (u + 1) * Q_BLOCK)
            for v_ref, vo_ref in ((vs_ref, vso_ref), (vw_ref, vwo_ref)):
                vo_ref[g, u, 0:HEAD_DIM, :] = v_ref[rows, cols].astype(F32).T.astype(vo_ref.dtype)
                vo_ref[g, u, HEAD_DIM:VT_ROWS, :] = ones


def nsa_prep_call(p, q_g, k_g, batch, tr=512):
    t = p.shape[0]
    seq = t // batch
    nst = seq // tr
    kvw = NSA_KV * HEAD_DIM
    kv_spec = lambda c: pl.BlockSpec((tr, kvw), lambda i: (i, (KV0 + c * kvw) // kvw))
    vt_sds = jax.ShapeDtypeStruct((batch * NSA_KV, seq // Q_BLOCK, VT_ROWS, Q_BLOCK), BF16)
    vt_spec = pl.BlockSpec((NSA_KV, tr // Q_BLOCK, VT_ROWS, Q_BLOCK), lambda i: (i // nst, i % nst, 0, 0))
    return pl.pallas_call(
        functools.partial(_nsa_prep_kernel, tr=tr),
        grid=(t // tr,),
        in_specs=[
            pl.BlockSpec((tr, MIX_W), lambda i: (i, Q0 // MIX_W)),
            kv_spec(2), kv_spec(3), kv_spec(4), kv_spec(5),
            pl.BlockSpec((1, HEAD_DIM), lambda i: (0, 0)),
            pl.BlockSpec((1, HEAD_DIM), lambda i: (0, 0)),
        ],
        out_specs=[
            pl.BlockSpec((tr, MIX_W), lambda i: (i, 0)),
            pl.BlockSpec((tr, kvw), lambda i: (i, 0)),
            pl.BlockSpec((tr, kvw), lambda i: (i, 0)),
            vt_spec, vt_spec,
        ],
        out_shape=[jax.ShapeDtypeStruct((t, MIX_W), BF16), jax.ShapeDtypeStruct((t, kvw), BF16),
                   jax.ShapeDtypeStruct((t, kvw), BF16), vt_sds, vt_sds],
        compiler_params=_cparams("parallel"),
        name="nsa_prep",
    )(p, p, p, p, p, q_g, k_g)


def _compress_kernel(ck_ref, cv_ref, pk_ref, pv_ref, w1k_ref, w2k_ref, w1v_ref, w2v_ref, kg_ref, ko_ref, vo_ref, *, nc):
    def comp(c_ref, p_ref, w1_ref, w2_ref):
        c = c_ref[...].astype(F32)
        xa = (c + p_ref[0:1, :]).astype(BF16)
        xb = (c + p_ref[1:2, :]).astype(BF16)
        first = _dot(xa, w1_ref[0])
        second = _dot(xb, w1_ref[1])
        hdn = _gelu_tanh(first + pltpu.roll(second, nc - 1, 0))
        return _dot(hdn.astype(BF16), w2_ref[...])

    row = lax.broadcasted_iota(jnp.int32, (nc, HEAD_DIM), 0)
    real = row < nc - 1
    kc = _head_rms(comp(ck_ref, pk_ref, w1k_ref, w2k_ref), kg_ref[...])
    vc = comp(cv_ref, pv_ref, w1v_ref, w2v_ref)
    zeros = jnp.zeros((KC_PAD, HEAD_DIM), F32)
    ko_ref[0:KC_PAD, :] = zeros
    vo_ref[0:KC_PAD, :] = zeros
    ko_ref[KC_PAD:KC_PAD + nc, :] = jnp.where(real, kc, 0.0)
    vo_ref[KC_PAD:KC_PAD + nc, :] = jnp.where(real, vc, 0.0)


def compress_call(ck, cv, pos_k, pos_v, w1k, w2k, w1v, w2v, k_g):
    bg, nc, half = ck.shape
    full2 = lambda shape: pl.BlockSpec(shape, lambda i: (0,) * len(shape))
    out_sds = jax.ShapeDtypeStruct((bg, KC_PAD + nc, HEAD_DIM), F32)
    return pl.pallas_call(
        functools.partial(_compress_kernel, nc=nc),
        grid=(bg,),
        in_specs=[
            pl.BlockSpec((None, nc, half), lambda i: (i, 0, 0)),
            pl.BlockSpec((None, nc, half), lambda i: (i, 0, 0)),
            full2((2, half)), full2((2, half)),
            full2((2, half, HEAD_DIM)), full2((HEAD_DIM, HEAD_DIM)),
            full2((2, half, HEAD_DIM)), full2((HEAD_DIM, HEAD_DIM)),
            full2((1, HEAD_DIM)),
        ],
        out_specs=[pl.BlockSpec((None, KC_PAD + nc, HEAD_DIM), lambda i: (i, 0, 0))] * 2,
        out_shape=[out_sds, out_sds],
        compiler_params=_cparams("parallel"),
        name="nsa_compress",
    )(ck, cv, pos_k, pos_v, w1k, w2k, w1v, w2v, k_g)


def _softmax_rows(parts):
    m = parts[0].max(axis=-1, keepdims=True)
    for x in parts[1:]:
        m = jnp.maximum(m, x.max(axis=-1, keepdims=True))
    m = jnp.maximum(m, -1e20)
    ps = [jnp.exp(x - m) for x in parts]
    s = ps[0].sum(axis=-1, keepdims=True)
    for p in ps[1:]:
        s = s + p.sum(axis=-1, keepdims=True)
    inv = 1.0 / jnp.maximum(s, 1e-30)
    return ps, inv


def _split_hi_lo(x):
    hi = x.astype(BF16)
    lo = (x - hi.astype(F32)).astype(BF16)
    return hi, lo


def _nsa_kernel(q_ref, kc_ref, vc_ref, ks_ref, vs_ref, kw_ref, vw_ref, ng_ref,
                bcn_ref, bs_ref, bw_ref, ov_ref, e_ref, o_ref, m_ref, acc_ref, ow_ref):
    i = pl.program_id(2)
    qb = Q_BLOCK
    rows = NSA_HPG * qb
    hd = HEAD_DIM
    qs = jnp.concatenate([q_ref[:, r * hd:(r + 1) * hd] for r in range(NSA_HPG)], axis=0)
    kcp = kc_ref.shape[0]

    nwin = WINDOW // qb + 1
    wtiles = [jnp.maximum(i - (nwin - 1 - tt), 0) for tt in range(nwin)]
    k_win = jnp.concatenate([kw_ref[pl.ds(pl.multiple_of(wt * qb, qb), qb), :] for wt in wtiles], axis=0)
    ltw = _dot_nt(k_win, qs) + bw_ref[...]
    wrow = lax.broadcasted_iota(jnp.int32, ltw.shape, 0)
    ltw = jnp.where(wrow >= (nwin - 1 - i) * qb, ltw, MASKED)
    mw = jnp.max(ltw, axis=0, keepdims=True)
    pw = _exp_bf16(ltw - mw)
    acc_w = _dot(jnp.concatenate([vw_ref[wt] for wt in wtiles], axis=1), pw)
    ow_ref[...] = acc_w[0:hd] * (1.0 / jnp.maximum(acc_w[hd:hd + 1], 1e-30))

    near0 = pl.multiple_of(8 * i + 8, 8)
    lf = _dot_nt(qs, kc_ref[...].astype(BF16))
    npad = lax.broadcasted_iota(jnp.int32, (rows, kcp), 1)
    lf = jnp.where((npad >= KC_PAD) & (npad < near0), lf, MASKED)
    ln = _dot_nt(qs, kc_ref[pl.ds(near0, qb), :].astype(BF16)) + bcn_ref[...]
    ncol = lax.broadcasted_iota(jnp.int32, (rows, qb), 1)
    ln = jnp.where(ncol >= KC_PAD - 8 - 8 * i, ln, MASKED)
    (pf, pn), inv = _softmax_rows([lf, ln])
    pf = pf * inv
    pn = pn * inv
    o_c = _dot(pf.astype(BF16), vc_ref[...].astype(BF16)) + _dot(pn.astype(BF16), vc_ref[pl.ds(near0, qb), :].astype(BF16))

    pf_g = pf[0:qb]
    pn_g = pn[0:qb]
    for r in range(1, NSA_HPG):
        pf_g = pf_g + pf[r * qb:(r + 1) * qb]
        pn_g = pn_g + pn[r * qb:(r + 1) * qb]
    ov_all = ov_ref[...].astype(BF16)
    ov_near = ov_ref[pl.ds(near0, qb), :].astype(BF16)
    imp = jnp.zeros((qb, ov_ref.shape[1]), F32)
    for part in _split_hi_lo(pf_g):
        imp = imp + _dot(part, ov_all)
    for part in _split_hi_lo(pn_g):
        imp = imp + _dot(part, ov_near)
    nblk = imp.shape[1]
    imp_t = imp.T
    jrow_i = lax.broadcasted_iota(jnp.int32, imp_t.shape, 0)
    cur = jnp.right_shift(lax.broadcasted_iota(jnp.int32, imp_t.shape, 1) + i * qb, 6)
    forced = (jrow_i == 0) | (jrow_i == cur) | (jrow_i == cur - 1)
    sc = jnp.where((jrow_i >= 1) & (jrow_i <= cur - 2), imp_t, -3e38)
    jrow = jrow_i.astype(F32)
    picked = jnp.zeros(sc.shape, F32)
    for _ in range(SLC_TOPK - 3):
        mx = jnp.max(sc, axis=0, keepdims=True)
        first = jnp.min(jnp.where(sc == mx, jrow, float(nblk)), axis=0, keepdims=True)
        pick = jrow == first
        picked = jnp.where(pick, 1.0, picked)
        sc = jnp.where(pick, -3e38, sc)
    sel_t = jnp.where(cur < SLC_TOPK, jnp.where(jrow_i <= cur, 1.0, 0.0), jnp.where(forced, 1.0, picked))
    sel = sel_t.T

    sel_neg = jnp.where(sel > 0.5, 0.0, MASKED).astype(BF16)
    q_aug = jnp.concatenate([qs, jnp.concatenate([sel_neg] * NSA_HPG, axis=0)], axis=1)

    def block_onehot(e_rows, valid):
        return jnp.where(valid, e_rows, jnp.ones_like(e_rows))

    prev_t = jnp.maximum(i - 1, 0)
    prev0 = pl.multiple_of(prev_t * qb, qb)
    diag0 = pl.multiple_of(i * qb, qb)
    k_near = jnp.concatenate([ks_ref[pl.ds(prev0, qb), :], ks_ref[pl.ds(diag0, qb), :]], axis=0)
    e_near = jnp.concatenate([e_ref[pl.ds(prev0, qb), :], e_ref[pl.ds(diag0, qb), :]], axis=0)
    row_near = lax.broadcasted_iota(jnp.int32, e_near.shape, 0)
    e_near = block_onehot(e_near, row_near >= jnp.where(i == 0, qb, 0))
    lt = _dot_nt(jnp.concatenate([k_near, e_near], axis=1), q_aug) + bs_ref[...]
    m0 = jnp.maximum(jnp.max(lt, axis=0, keepdims=True), -1e20).astype(BF16).astype(F32)
    p0 = _exp_bf16(lt - m0)
    vt_near = jnp.concatenate([vs_ref[prev_t], vs_ref[i]], axis=1)
    m_ref[...] = m0
    acc_ref[...] = _dot(vt_near, p0)

    far_end = (i - 1) * qb
    tiles = 4
    chunk = tiles * qb
    halves = 2

    def far_body(c, carry):
        lts, vts = [], []
        for hf in range(halves):
            t0 = (c * halves + hf) * tiles
            c0 = pl.multiple_of(t0 * qb, chunk)
            e_c = e_ref[pl.ds(c0, chunk), :]
            row_c = lax.broadcasted_iota(jnp.int32, e_c.shape, 0) + c0
            k_aug = jnp.concatenate([ks_ref[pl.ds(c0, chunk), :], block_onehot(e_c, row_c < far_end)], axis=1)
            lts.append(_dot_nt(k_aug, q_aug).astype(BF16))
            vts.append(jnp.concatenate([vs_ref[t0 + u] for u in range(tiles)], axis=1))
        m_old = m_ref[...]
        m_new = m_old
        for lt_c in lts:
            m_new = jnp.maximum(m_new, jnp.max(lt_c, axis=0, keepdims=True).astype(F32))
        acc = jnp.exp(m_old - m_new) * acc_ref[...]
        m_new_b = m_new.astype(BF16)
        for lt_c, vt_c in zip(lts, vts):
            acc = acc + _dot(vt_c, jnp.exp(lt_c - m_new_b))
        m_ref[...] = m_new
        acc_ref[...] = acc
        return carry

    lax.fori_loop(0, (i - 1 + halves * tiles - 1) // (halves * tiles), far_body, 0)
    acc_s = acc_ref[...]
    o_s_t = acc_s[0:hd] * (1.0 / jnp.maximum(acc_s[hd:hd + 1], 1e-30))
    o_w_t = ow_ref[...]

    gate = _sigmoid(ng_ref[...].astype(F32))
    gate_t = gate.T
    for r in range(NSA_HPG):
        rs = slice(r * qb, (r + 1) * qb)
        o_sw_t = gate_t[3 * r + 1:3 * r + 2, :] * o_s_t[:, rs] + gate_t[3 * r + 2:3 * r + 3, :] * o_w_t[:, rs]
        o = gate[:, 3 * r:3 * r + 1] * o_c[rs] + o_sw_t.T
        o_ref[:, r * hd:(r + 1) * hd] = o.astype(o_ref.dtype)


def nsa_call(p, qn, ksn, kwn, vst, vwt, kcp, vcp, bcn, bsl_t, bwn_t, ovl, e_tab, batch):
    t = p.shape[0]
    seq = t // batch
    nq = seq // Q_BLOCK
    assert nq % 8 == 0
    g_ = NSA_KV
    kcp_rows = kcp.shape[1]
    rows = NSA_HPG * Q_BLOCK
    ng_blk = NG0 // LANES
    seq_spec = pl.BlockSpec((seq, HEAD_DIM), lambda b, g, i: (b, g))
    vt_spec = pl.BlockSpec((None, nq, VT_ROWS, Q_BLOCK), lambda b, g, i: (b * g_ + g, 0, 0, 0))
    cmp_spec = pl.BlockSpec((None, kcp_rows, HEAD_DIM), lambda b, g, i: (b * g_ + g, 0, 0))
    tbl_spec = lambda arr: pl.BlockSpec((None,) + arr.shape[1:], lambda b, g, i: (g, 0, 0))
    return pl.pallas_call(
        _nsa_kernel,
        grid=(batch, g_, nq),
        in_specs=[
            pl.BlockSpec((Q_BLOCK, rows), lambda b, g, i: (b * nq + i, g)),
            cmp_spec, cmp_spec,
            seq_spec, vt_spec, seq_spec, vt_spec,
            pl.BlockSpec((Q_BLOCK, LANES), lambda b, g, i: (b * nq + i, ng_blk + g)),
            tbl_spec(bcn), tbl_spec(bsl_t), tbl_spec(bwn_t),
            pl.BlockSpec(ovl.shape, lambda b, g, i: (0, 0)),
            pl.BlockSpec(e_tab.shape, lambda b, g, i: (0, 0)),
        ],
        out_specs=pl.BlockSpec((Q_BLOCK, rows), lambda b, g, i: (b * nq + i, g)),
        out_shape=jax.ShapeDtypeStruct((t, MIX_W), BF16),
        scratch_shapes=[pltpu.VMEM((1, rows), F32), pltpu.VMEM((VT_ROWS, rows), F32), pltpu.VMEM((HEAD_DIM, rows), F32)],
        compiler_params=_cparams("parallel", "parallel", "arbitrary"),
        name="nsa_attention",
    )(qn, kcp, vcp, ksn, vst, kwn, vwt, p, bcn, bsl_t, bwn_t, ovl, e_tab)


def _t5_bucket_np(dist):
    n = np.maximum(dist, 0)
    max_exact = REL_BUCKETS // 2
    nf = np.maximum(n, 1).astype(np.float32)
    large = max_exact + (np.log(nf / max_exact) / math.log(REL_MAX_DIST / max_exact)
                         * (REL_BUCKETS - max_exact)).astype(np.int32)
    large = np.minimum(large, REL_BUCKETS - 1)
    return np.where(n < max_exact, n, large)


def _bias_table_kernel(rb_ref, idx_ref, o_ref):
    h = pl.program_id(0)
    idx = idx_ref[...]
    far = rb_ref[REL_BUCKETS - 1, h]
    acc = jnp.zeros(idx.shape, F32)
    for b in range(REL_BUCKETS):
        acc = jnp.where(idx == b, rb_ref[b, h] - far, acc)
    o_ref[...] = jnp.where(idx < 0, MASKED, acc)


def bias_table_call(rel_bias, bucket):
    nh = rel_bias.shape[1]
    return pl.pallas_call(
        _bias_table_kernel,
        grid=(nh,),
        in_specs=[pl.BlockSpec(memory_space=pltpu.SMEM), pl.BlockSpec(bucket.shape, lambda h: (0, 0))],
        out_specs=pl.BlockSpec((None,) + bucket.shape, lambda h: (h, 0, 0)),
        out_shape=jax.ShapeDtypeStruct((nh,) + bucket.shape, F32),
        compiler_params=_cparams("arbitrary"),
        name="bias_tables",
    )(rel_bias, bucket)


def _bias_tables(rel_bias, seq):
    qb = Q_BLOCK
    iq = np.arange(qb)[:, None]
    jk = np.arange(qb)[None, :]
    d_diag = iq - jk
    d_prev = qb + iq - jk
    d_edge = WINDOW + iq - jk
    d_far = np.full((qb, qb), 2 * qb)
    d_sel = np.concatenate([d_prev, d_diag], axis=1)
    d_win = np.concatenate([d_edge] + [d_far] * (WINDOW // qb - 2) + [d_prev, d_diag], axis=1)
    d_cmp = iq - CMP_STRIDE * (jk - (KC_PAD - 8)) - (CMP_LEN - 1)
    parts = [(d_cmp, d_cmp >= 0), (d_sel, d_sel >= 0), (d_win, (d_win >= 0) & (d_win < WINDOW))]
    bucket = np.concatenate([np.where(vis, _t5_bucket_np(dist), -1) for dist, vis in parts], axis=1).astype(np.int32)
    tables = bias_table_call(rel_bias, jnp.asarray(bucket))
    tables = tables.reshape(NSA_KV, NSA_HPG * qb, bucket.shape[1])
    w_cmp, w_sel = d_cmp.shape[1], d_sel.shape[1]
    bcn = tables[:, :, :w_cmp]
    bsl_t = tables[:, :, w_cmp:w_cmp + w_sel].transpose(0, 2, 1)
    bwn_t = tables[:, :, w_cmp + w_sel:].transpose(0, 2, 1)
    nc = seq // CMP_STRIDE
    n_slc = seq // SLC_LEN
    c0 = (np.arange(nc) * CMP_STRIDE)[:, None]
    s0 = (np.arange(n_slc) * SLC_LEN)[None, :]
    ov = ((c0 < s0 + SLC_LEN) & (c0 + CMP_LEN > s0) & (np.arange(nc)[:, None] < nc - 1)).astype(np.float32)
    ovl = np.concatenate([np.zeros((KC_PAD, n_slc), np.float32), ov], axis=0)
    e_tab = (np.arange(seq)[:, None] // SLC_LEN == np.arange(n_slc)[None, :]).astype(np.float32)
    return bcn, bsl_t, bwn_t, jnp.asarray(ovl), jnp.asarray(e_tab, dtype=BF16)


def _merge_kernel(g_ref, a_ref, b_ref, c_ref, d_ref, wgu_ref, wbr_ref, o_ref):
    gin = g_ref[...]
    acc = None
    for bi, br in enumerate((a_ref, b_ref, c_ref, d_ref)):
        gate = _sigmoid(_dot(gin, wgu_ref[bi]))
        term = gate * _dot(br[...], wbr_ref[bi])
        acc = term if acc is None else acc + term
    o_ref[...] = acc.astype(o_ref.dtype)


def merge_call(p, branches, wgu, wbr, layer, tm=1024, tn=512):
    t = p.shape[0]
    d = wgu.shape[-1]
    row = lambda w, cb: pl.BlockSpec((tm, w), lambda i, j: (i, cb))
    return pl.pallas_call(
        _merge_kernel,
        grid=(t // tm, d // tn),
        in_specs=[row(GATE_RANK, G0 // GATE_RANK)] + [row(MIX_W, 0)] * 4 + [
            pl.BlockSpec((None, 4, GATE_RANK, tn), lambda i, j: (layer, 0, 0, j)),
            pl.BlockSpec((None, 4, MIX_W, tn), lambda i, j: (layer, 0, 0, j)),
        ],
        out_specs=pl.BlockSpec((tm, tn), lambda i, j: (i, j)),
        out_shape=jax.ShapeDtypeStruct((t, d), BF16),
        compiler_params=_cparams("parallel", "arbitrary"),
        name="merge",
    )(p, *branches, wgu, wbr)


def _rearranged_w_in(w):
    o_a, o_q, o_kv, o_ng, o_c, o_d, o_g = 0, 1024, 2048, 3584, 3608, 5656, 7704
    d = w.shape[0]
    hpg3 = NSA_HPG * 3
    zpad = lambda n: jnp.zeros((d, n), BF16)
    parts = [
        w[:, o_c:o_c + 2048], w[:, o_d:o_d + 2048], w[:, o_a:o_a + 1024],
        w[:, o_q:o_q + 1024], w[:, o_kv:o_kv + 1536], w[:, o_g:o_g + 512],
        w[:, o_ng:o_ng + hpg3], zpad(LANES - hpg3),
        w[:, o_ng + hpg3:o_ng + 2 * hpg3], zpad(LANES - hpg3), zpad(PCOLS - NG0 - 2 * LANES),
    ]
    return jnp.concatenate([part.astype(BF16) for part in parts], axis=-1)


def kernel(x, rel_bias, norm_mix_g, norm_ffn_g, w_in, pool_w, pool_scale, q_norm_g, k_norm_g, cmp_pos_k, cmp_w1_k, cmp_w2_k, cmp_pos_v, cmp_w1_v, cmp_w2_v, gmlp_ln_g, gmlp_ln_b, gmlp_ws, gmlp_bs, conv_w, conv_b, conv_ln_g, conv_ln_b, w_branch, w_gate_up, w_out, ffn_w_gate, ffn_w_up, ffn_w_down, moe_router, moe_router_b, moe_w_gate, moe_w_up, moe_w_down):
    batch, seq, d = x.shape
    t = batch * seq
    depth = w_in.shape[0]
    nc = seq // CMP_STRIDE
    half = CMP_STRIDE * HEAD_DIM
    xf = x.reshape(t, d)

    bcn, bsl_t, bwn_t, ovl, e_tab = _bias_tables(rel_bias, seq)
    row = lambda v: v.reshape(1, -1)
    w_gate_up_b, w_branch_b, w_out_b = (w.astype(BF16) for w in (w_gate_up, w_branch, w_out))
    ffn_w_gate_b, ffn_w_up_b, ffn_w_down_b = (w.astype(BF16) for w in (ffn_w_gate, ffn_w_up, ffn_w_down))
    moe_w_gate_b, moe_w_up_b, moe_w_down_b = (w.astype(BF16) for w in (moe_w_gate, moe_w_up, moe_w_down))

    for l in range(depth):
        h = rmsnorm_call(xf, row(norm_mix_g[l]))
        p = matmul_call(h, _rearranged_w_in(w_in[l]), name="in_proj")
        o_a = pool_call(p, pool_w[l].astype(BF16), row(pool_scale[l]), batch)
        qn, ksn, kwn, vst, vwt = nsa_prep_call(p, row(q_norm_g[l]), row(k_norm_g[l]), batch)
        kvc = p[:, KV0:KV0 + 2 * NSA_KV * HEAD_DIM].reshape(batch, nc, CMP_STRIDE, 2, NSA_KV, HEAD_DIM)
        kvc = kvc.transpose(3, 0, 4, 1, 2, 5).reshape(2, batch * NSA_KV, nc, half)
        kcp, vcp = compress_call(
            kvc[0], kvc[1], cmp_pos_k[l].reshape(2, half), cmp_pos_v[l].reshape(2, half),
            cmp_w1_k[l].reshape(2, half, HEAD_DIM).astype(BF16), cmp_w2_k[l].astype(BF16),
            cmp_w1_v[l].reshape(2, half, HEAD_DIM).astype(BF16), cmp_w2_v[l].astype(BF16), row(k_norm_g[l]))
        o_b = nsa_call(p, qn, ksn, kwn, vst, vwt, kcp, vcp, bcn, bsl_t, bwn_t, ovl, e_tab, batch)
        bs_b = jnp.repeat(gmlp_bs[l].T, HEAD_DIM, axis=1)
        o_c = gmlp_call(p, row(gmlp_ln_g[l]), row(gmlp_ln_b[l]), gmlp_ws[l], bs_b)
        o_d = conv_call(p, conv_w[l], row(conv_b[l]), row(conv_ln_g[l]), row(conv_ln_b[l]), batch)
        mix = merge_call(p, (o_a, o_b, o_c, o_d), w_gate_up_b, w_branch_b, l)
        xf = matmul_call(mix, w_out_b, layer=l, res=xf, out_dtype=F32, name="out_proj")
        i = l // 2
        if l % 2 == 0:
            h = rmsnorm_call(xf, row(norm_ffn_g[l]))
            act = swiglu_call(h, ffn_w_gate_b, ffn_w_up_b, i)
            xf = matmul_call(act, ffn_w_down_b, layer=i, res=xf, out_dtype=F32, name="ffn_down")
        else:
            h, hp = rmsnorm_call(xf, row(norm_ffn_g[l]), packed=True)
            rw = jnp.pad(moe_router[i], ((0, 0), (0, LANES - N_EXPERTS))).astype(BF16)
            rb = jnp.pad(moe_router_b[i], (0, LANES - N_EXPERTS)).reshape(1, LANES)
            xf = moe_layer(xf, h, hp, rw, rb, moe_w_gate_b, moe_w_up_b, moe_w_down_b, i)
    return xf.reshape(batch, seq, d)
```

```python
import functools
import math

import jax
import jax.numpy as jnp
import numpy as np
from jax import lax
from jax.experimental import pallas as pl
from jax.experimental.pallas import tpu as pltpu

F32 = jnp.float32
BF16 = jnp.bfloat16

VMEM_LIMIT_BYTES = 56 * 1024 * 1024
LANES = 128

EPS = 1e-6
HEAD_DIM = 128
MIX_W = 1024
POOL_WINDOWS = (2, 4, 8, 16)
POOL_GW = MIX_W // len(POOL_WINDOWS)
NSA_HEADS = 8
NSA_KV = 2
NSA_HPG = NSA_HEADS // NSA_KV
CMP_LEN = 32
CMP_STRIDE = 16
SLC_LEN = 64
SLC_TOPK = 16
WINDOW = 512
Q_BLOCK = 128
FORCE_BONUS = 1e4
NEG = -1e9
MASKED = -1e30
GMLP_CHUNK = 128
CONV_W = 31
GATE_RANK = 512
REL_BUCKETS = 32
REL_MAX_DIST = 128
N_EXPERTS = 8
TOP_K = 2

C0 = 0
D0 = 2048
A0 = 4096
Q0 = 5120
KV0 = 6144
G0 = 7680
NG0 = 8192
PCOLS = 8704
KC_PAD = 128
VT_ROWS = HEAD_DIM + 16


def _cparams(*sem):
    return pltpu.CompilerParams(dimension_semantics=sem, vmem_limit_bytes=VMEM_LIMIT_BYTES)


def _dot(a, b):
    return jnp.dot(a, b, preferred_element_type=F32)


def _dot_nt(a, b):
    return lax.dot_general(a, b, (((1,), (1,)), ((), ())), preferred_element_type=F32)


def _gelu_tanh(x):
    return 0.5 * x * (1.0 + jnp.tanh(math.sqrt(2.0 / math.pi) * (x + 0.044715 * (x * x * x))))


def _sigmoid(x):
    return 1.0 / (1.0 + jnp.exp(-x))


def _exp_bf16(x):
    return jnp.exp(x.astype(BF16))


def _rmsnorm_kernel(x_ref, g_ref, o_ref, *packed_ref):
    xf = x_ref[...]
    r = lax.rsqrt(jnp.mean(xf * xf, axis=-1, keepdims=True) + EPS)
    y = (xf * r) * g_ref[...]
    o_ref[...] = y.astype(o_ref.dtype)
    if packed_ref:
        half = y.shape[1] // 2
        packed_ref[0][...] = _pack_bf16_pairs(y[:, :half], y[:, half:])


def rmsnorm_call(x, g, tr=256, packed=False):
    t, d = x.shape
    out_specs = [pl.BlockSpec((tr, d), lambda i: (i, 0))]
    out_shape = [jax.ShapeDtypeStruct((t, d), BF16)]
    if packed:
        out_specs.append(pl.BlockSpec((tr, d // 2), lambda i: (i, 0)))
        out_shape.append(jax.ShapeDtypeStruct((t, d // 2), jnp.uint32))
    out = pl.pallas_call(
        _rmsnorm_kernel,
        grid=(t // tr,),
        in_specs=[pl.BlockSpec((tr, d), lambda i: (i, 0)), pl.BlockSpec((1, d), lambda i: (0, 0))],
        out_specs=out_specs,
        out_shape=out_shape,
        compiler_params=_cparams("parallel"),
        name="rmsnorm",
    )(x, g)
    return out if packed else out[0]


def _mm_kernel(*refs, nk, has_res):
    if has_res:
        a_ref, w_ref, r_ref, o_ref = refs[:4]
        scr = refs[4:]
    else:
        a_ref, w_ref, o_ref = refs[:3]
        r_ref = None
        scr = refs[3:]
    part = _dot(a_ref[...], w_ref[...])

    def finish(acc):
        if has_res:
            acc = acc + r_ref[...]
        o_ref[...] = acc.astype(o_ref.dtype)

    if nk == 1:
        finish(part)
    else:
        acc_ref = scr[0]
        k = pl.program_id(2)

        @pl.when(k == 0)
        def _():
            acc_ref[...] = part

        @pl.when(k > 0)
        def _():
            acc_ref[...] += part

        @pl.when(k == nk - 1)
        def _():
            finish(acc_ref[...])


def matmul_call(a, w, layer=None, res=None, out_dtype=BF16, tm=1024, tn=512, tk=4096, name="matmul"):
    m, kdim = a.shape
    n = w.shape[-1]
    tk = min(tk, kdim)
    nk = kdim // tk
    assert m % tm == 0 and n % tn == 0 and kdim % tk == 0
    if layer is None:
        w_spec = pl.BlockSpec((tk, tn), lambda i, j, k: (k, j))
    else:
        w_spec = pl.BlockSpec((None, tk, tn), lambda i, j, k: (layer, k, j))
    in_specs = [pl.BlockSpec((tm, tk), lambda i, j, k: (i, k)), w_spec]
    args = [a, w]
    if res is not None:
        in_specs.append(pl.BlockSpec((tm, tn), lambda i, j, k: (i, j)))
        args.append(res)
    return pl.pallas_call(
        functools.partial(_mm_kernel, nk=nk, has_res=res is not None),
        grid=(m // tm, n // tn, nk),
        in_specs=in_specs,
        out_specs=pl.BlockSpec((tm, tn), lambda i, j, k: (i, j)),
        out_shape=jax.ShapeDtypeStruct((m, n), out_dtype),
        scratch_shapes=[pltpu.VMEM((tm, tn), F32)] if nk > 1 else [],
        compiler_params=_cparams("parallel", "parallel", "arbitrary"),
        name=name,
    )(*args)


def _swiglu_kernel(h_ref, wg_ref, wu_ref, o_ref):
    h = h_ref[...]
    hg = _dot(h, wg_ref[...])
    hu = _dot(h, wu_ref[...])
    o_ref[...] = (hg * _sigmoid(hg) * hu).astype(o_ref.dtype)


def swiglu_call(h, wg, wu, layer, tm=1024, tn=512):
    t, d = h.shape
    f = wg.shape[-1]
    assert f % tn == 0 and t % tm == 0
    return pl.pallas_call(
        _swiglu_kernel,
        grid=(t // tm, f // tn),
        in_specs=[
            pl.BlockSpec((tm, d), lambda i, j: (i, 0)),
            pl.BlockSpec((None, d, tn), lambda i, j: (layer, 0, j)),
            pl.BlockSpec((None, d, tn), lambda i, j: (layer, 0, j)),
        ],
        out_specs=pl.BlockSpec((tm, tn), lambda i, j: (i, j)),
        out_shape=jax.ShapeDtypeStruct((t, f), BF16),
        compiler_params=_cparams("parallel", "arbitrary"),
        name="swiglu",
    )(h, wg, wu)


def _router_kernel(h_ref, w_ref, b_ref, info_ref, cnt_ref, run_ref, *, tm):
    @pl.when(pl.program_id(0) == 0)
    def _():
        run_ref[...] = jnp.zeros_like(run_ref)

    logits = _dot(h_ref[...], w_ref[...]) + b_ref[...]
    lane = lax.broadcasted_iota(jnp.int32, logits.shape, 1)
    logits = jnp.where(lane < N_EXPERTS, logits, MASKED)
    v1 = jnp.max(logits, axis=-1, keepdims=True)
    i1 = jnp.min(jnp.where(logits == v1, lane, LANES), axis=-1, keepdims=True)
    rest = jnp.where(lane == i1, MASKED, logits)
    v2 = jnp.max(rest, axis=-1, keepdims=True)
    i2 = jnp.min(jnp.where(rest == v2, lane, LANES), axis=-1, keepdims=True)
    e2 = jnp.exp(v2 - v1)
    inv = 1.0 / (1.0 + e2)
    chosen = jnp.where((lane == i1) | (lane == i2), 1.0, 0.0)
    earlier = lax.broadcasted_iota(jnp.int32, (tm, tm), 1) < lax.broadcasted_iota(jnp.int32, (tm, tm), 0)
    rank = _dot(jnp.where(earlier, 1.0, 0.0).astype(BF16), chosen.astype(BF16)) + run_ref[...]
    run_ref[...] += jnp.sum(chosen, axis=0, keepdims=True)
    cnt_ref[...] = run_ref[...]
    r1 = jnp.sum(jnp.where(lane == i1, rank, 0.0), axis=-1, keepdims=True)
    r2 = jnp.sum(jnp.where(lane == i2, rank, 0.0), axis=-1, keepdims=True)
    cols = (i1.astype(F32), i2.astype(F32), inv, e2 * inv, r1, r2)
    info = jnp.zeros(logits.shape, F32)
    for c, col in enumerate(cols):
        info = jnp.where(lane == c, col, info)
    info_ref[...] = info


def router_call(h, w, b, tm=1024):
    t, d = h.shape
    return pl.pallas_call(
        functools.partial(_router_kernel, tm=tm),
        grid=(t // tm,),
        in_specs=[pl.BlockSpec((tm, d), lambda i: (i, 0)), pl.BlockSpec((d, LANES), lambda i: (0, 0)),
                  pl.BlockSpec((1, LANES), lambda i: (0, 0))],
        out_specs=[pl.BlockSpec((tm, LANES), lambda i: (i, 0)), pl.BlockSpec((1, LANES), lambda i: (0, 0))],
        out_shape=[jax.ShapeDtypeStruct((t, LANES), F32), jax.ShapeDtypeStruct((1, LANES), F32)],
        scratch_shapes=[pltpu.VMEM((1, LANES), F32)],
        compiler_params=_cparams("arbitrary"),
        name="router",
    )(h, w, b)


def _pack_bf16_pairs(lo, hi):
    lo_bits = pltpu.bitcast(lo.astype(BF16).astype(F32), jnp.uint32)
    hi_bits = pltpu.bitcast(hi.astype(BF16).astype(F32), jnp.uint32)
    return jnp.right_shift(lo_bits, jnp.uint32(16)) | (hi_bits & jnp.uint32(0xFFFF0000))


def _unpack_bf16_pairs(u):
    lo = pltpu.bitcast(jnp.left_shift(u, jnp.uint32(16)), F32)
    hi = pltpu.bitcast(u & jnp.uint32(0xFFFF0000), F32)
    return lo, hi


def _row_copy(src_ref, src_row, dst_ref, dst_row, sem):
    return pltpu.make_async_copy(src_ref.at[pl.ds(src_row, 1), :], dst_ref.at[pl.ds(dst_row, 1), :], sem)


def _dispatch_kernel(row_token_ref, hp_ref, xs_ref, sem, *, tt):
    i = pl.program_id(0)

    def start(t, c):
        _row_copy(hp_ref, row_token_ref[i * tt + t], xs_ref, t, sem).start()
        return c

    def wait(t, c):
        _row_copy(hp_ref, 0, xs_ref, 0, sem).wait()
        return c

    lax.fori_loop(0, tt, start, 0, unroll=8)
    lax.fori_loop(0, tt, wait, 0, unroll=8)


def moe_dispatch_call(row_token, hp, tt=256):
    n_rows = row_token.shape[0]
    half = hp.shape[1]
    return pl.pallas_call(
        functools.partial(_dispatch_kernel, tt=tt),
        grid_spec=pltpu.PrefetchScalarGridSpec(
            num_scalar_prefetch=1,
            grid=(n_rows // tt,),
            in_specs=[pl.BlockSpec(memory_space=pl.ANY)],
            out_specs=pl.BlockSpec((tt, half), lambda i, rt: (i, 0)),
            scratch_shapes=[pltpu.SemaphoreType.DMA(())],
        ),
        out_shape=jax.ShapeDtypeStruct((n_rows, half), jnp.uint32),
        compiler_params=_cparams("arbitrary"),
        name="moe_dispatch",
    )(row_token, hp)


def _moe_ffn_kernel(te_ref, nv_ref, xs_ref, wg_ref, wu_ref, wdl_ref, wdh_ref, y_ref, xb_ref, act_ref, *, nj, half):
    r = pl.program_id(0)
    s = pl.program_id(1)
    valid = r < nv_ref[0]
    up_phase = s < nj

    @pl.when(valid & (s == 0))
    def _():
        lo, hi = _unpack_bf16_pairs(xs_ref[...])
        xb_ref[:, :half] = lo.astype(BF16)
        xb_ref[:, half:] = hi.astype(BF16)

    @pl.when(valid & up_phase)
    def _():
        xb = xb_ref[...]
        hg = _dot(xb, wg_ref[...])
        hu = _dot(xb, wu_ref[...])
        act_ref[s] = (hg * _sigmoid(hg) * hu).astype(BF16)

    @pl.when(valid & jnp.logical_not(up_phase))
    def _():
        act = jnp.concatenate([act_ref[k] for k in range(nj)], axis=1)
        y_ref[...] = _pack_bf16_pairs(_dot(act, wdl_ref[...]), _dot(act, wdh_ref[...]))

    @pl.when(jnp.logical_not(valid) & jnp.logical_not(up_phase))
    def _():
        y_ref[...] = jnp.zeros_like(y_ref)


def moe_ffn_call(tile_expert, n_valid, xs, wg, wu, wd, layer, tm, tf=256, tn=512):
    n_rows, half = xs.shape
    d, f = wg.shape[-2:]
    nj = f // tf
    nb = half // tn
    assert f % tf == 0 and half % tn == 0 and n_rows % tm == 0 and d == 2 * half

    def tile(r, nv):
        return jnp.minimum(r, nv[0] - 1)

    def up_blk(r, s, nv):
        return jnp.where(r < nv[0], jnp.minimum(s, nj - 1), nj - 1)

    def down_blk(r, s, nv):
        return jnp.where(r < nv[0], jnp.maximum(s - nj, 0), nb - 1)

    up_spec = pl.BlockSpec((None, None, d, tf), lambda r, s, te, nv: (layer, te[tile(r, nv)], 0, up_blk(r, s, nv)))
    return pl.pallas_call(
        functools.partial(_moe_ffn_kernel, nj=nj, half=half),
        grid_spec=pltpu.PrefetchScalarGridSpec(
            num_scalar_prefetch=2,
            grid=(n_rows // tm, nj + nb),
            in_specs=[
                pl.BlockSpec((tm, half), lambda r, s, te, nv: (tile(r, nv), 0)),
                up_spec, up_spec,
                pl.BlockSpec((None, None, f, tn), lambda r, s, te, nv: (layer, te[tile(r, nv)], 0, down_blk(r, s, nv))),
                pl.BlockSpec((None, None, f, tn), lambda r, s, te, nv: (layer, te[tile(r, nv)], 0, nb + down_blk(r, s, nv))),
            ],
            out_specs=pl.BlockSpec((tm, tn), lambda r, s, te, nv: (r, jnp.maximum(s - nj, 0))),
            scratch_shapes=[pltpu.VMEM((tm, d), BF16), pltpu.VMEM((nj, tm, tf), BF16)],
        ),
        out_shape=jax.ShapeDtypeStruct((n_rows, half), jnp.uint32),
        compiler_params=_cparams("arbitrary", "arbitrary"),
        name="moe_ffn",
    )(tile_expert, n_valid, xs, wg, wu, wd, wd)


def _combine_kernel(dest_ref, y_ref, x_ref, info_ref, o_ref, buf_ref, sem, *, tt, half):
    i = pl.program_id(0)

    def start(t, c):
        tok = i * tt + t
        for k in range(TOP_K):
            _row_copy(y_ref, dest_ref[TOP_K * tok + k], buf_ref.at[k], t, sem).start()
        return c

    def wait(t, c):
        for k in range(TOP_K):
            _row_copy(y_ref, 0, buf_ref.at[k], 0, sem).wait()
        return c

    lax.fori_loop(0, tt, start, 0, unroll=8)
    lax.fori_loop(0, tt, wait, 0, unroll=8)
    info = info_ref[...]
    lo_sum = x_ref[:, :half]
    hi_sum = x_ref[:, half:]
    for k in range(TOP_K):
        wk = info[:, 2 + k:3 + k]
        lo, hi = _unpack_bf16_pairs(buf_ref[k])
        lo_sum = lo_sum + wk * lo
        hi_sum = hi_sum + wk * hi
    o_ref[:, :half] = lo_sum
    o_ref[:, half:] = hi_sum


def moe_combine_call(dest, y, x, info, tt=256):
    t, d = x.shape
    half = d // 2
    return pl.pallas_call(
        functools.partial(_combine_kernel, tt=tt, half=half),
        grid_spec=pltpu.PrefetchScalarGridSpec(
            num_scalar_prefetch=1,
            grid=(t // tt,),
            in_specs=[pl.BlockSpec(memory_space=pl.ANY),
                      pl.BlockSpec((tt, d), lambda i, dest_: (i, 0)),
                      pl.BlockSpec((tt, LANES), lambda i, dest_: (i, 0))],
            out_specs=pl.BlockSpec((tt, d), lambda i, dest_: (i, 0)),
            scratch_shapes=[pltpu.VMEM((TOP_K, tt, half), jnp.uint32), pltpu.SemaphoreType.DMA(())],
        ),
        out_shape=jax.ShapeDtypeStruct((t, d), F32),
        compiler_params=_cparams("arbitrary"),
        name="moe_combine",
    )(dest, y, x, info)


def moe_layer(xf, h, hp, router_w, router_b, wg, wu, wd, layer, tm=512):
    t, d = xf.shape
    ne = wg.shape[1]
    info, cnt = router_call(h, router_w, router_b)
    experts = info[:, 0:TOP_K].astype(jnp.int32)
    ranks = info[:, 4:4 + TOP_K].astype(jnp.int32)
    counts = cnt[0, :ne].astype(jnp.int32)
    padded = ((counts + tm - 1) // tm) * tm
    ends = jnp.cumsum(padded)
    dest = ((ends - padded)[experts] + ranks).reshape(-1)
    n_tiles = (t * TOP_K) // tm + ne
    tile_expert = jnp.minimum(jnp.sum(jnp.arange(n_tiles)[:, None] * tm >= ends[None, :], axis=-1), ne - 1).astype(jnp.int32)
    n_valid = (ends[-1:] // tm).astype(jnp.int32)
    slot_token = jnp.repeat(jnp.arange(t, dtype=jnp.int32), TOP_K)
    row_token = jnp.zeros((n_tiles * tm,), jnp.int32).at[dest].set(slot_token, unique_indices=True)
    xs = moe_dispatch_call(row_token, hp)
    y = moe_ffn_call(tile_expert, n_valid, xs, wg, wu, wd, layer, tm)
    return moe_combine_call(dest, y, xf, info)


def _pool_kernel(cur_ref, halo_ref, w_ref, scale_ref, o_ref, xs_ref, *, ts, halo):
    i = pl.program_id(1)
    prev = halo_ref[...].astype(F32)
    xs_ref[0:halo, :] = jnp.where(i == 0, 0.0, prev)
    xs_ref[halo:halo + ts, :] = cur_ref[...].astype(F32)
    row = lax.broadcasted_iota(jnp.int32, (ts, POOL_GW), 0) + i * ts + 1
    for gi, w in enumerate(POOL_WINDOWS):
        c0 = gi * POOL_GW
        x = xs_ref[halo:halo + ts, c0:c0 + POOL_GW]
        wsum = x
        for k in range(1, w):
            wsum = wsum + xs_ref[halo - k:halo - k + ts, c0:c0 + POOL_GW]
        cnt = jnp.minimum(row, w).astype(F32)
        diff = (wsum / cnt - x).astype(BF16)
        y = _dot(diff, w_ref[gi]) * scale_ref[:, c0:c0 + POOL_GW]
        o_ref[:, c0:c0 + POOL_GW] = y.astype(o_ref.dtype)


def pool_call(p, w_pool, scale, batch, ts=512):
    t = p.shape[0]
    seq = t // batch
    nts = seq // ts
    halo = 16
    cb = A0 // MIX_W
    return pl.pallas_call(
        functools.partial(_pool_kernel, ts=ts, halo=halo),
        grid=(batch, nts),
        in_specs=[
            pl.BlockSpec((ts, MIX_W), lambda b, i: (b * nts + i, cb)),
            pl.BlockSpec((halo, MIX_W), lambda b, i: (jnp.maximum((b * nts + i) * (ts // halo) - 1, 0), cb)),
            pl.BlockSpec((len(POOL_WINDOWS), POOL_GW, POOL_GW), lambda b, i: (0, 0, 0)),
            pl.BlockSpec((1, MIX_W), lambda b, i: (0, 0)),
        ],
        out_specs=pl.BlockSpec((ts, MIX_W), lambda b, i: (b * nts + i, 0)),
        out_shape=jax.ShapeDtypeStruct((t, MIX_W), BF16),
        scratch_shapes=[pltpu.VMEM((halo + ts, MIX_W), F32)],
        compiler_params=_cparams("parallel", "arbitrary"),
        name="pool",
    )(p, p, w_pool, scale)


def _layernorm(v, g, b):
    mu = jnp.mean(v, axis=-1, keepdims=True)
    vc = v - mu
    var = jnp.mean(vc * vc, axis=-1, keepdims=True)
    return (vc * lax.rsqrt(var + EPS)) * g + b


def _gmlp_kernel(z_ref, g_ref, b_ref, ws_ref, bs_ref, o_ref, *, ts):
    z = _gelu_tanh(z_ref[...].astype(F32))
    u = z[:, :MIX_W]
    v = _layernorm(z[:, MIX_W:], g_ref[...], b_ref[...]).astype(BF16)
    ri = lax.broadcasted_iota(jnp.int32, (GMLP_CHUNK, GMLP_CHUNK), 0)
    ci = lax.broadcasted_iota(jnp.int32, (GMLP_CHUNK, GMLP_CHUNK), 1)
    tri = ci <= ri
    for h in range(MIX_W // HEAD_DIM):
        w = jnp.where(tri, ws_ref[h], 0.0).astype(BF16)
        bias = bs_ref[:, h * HEAD_DIM:(h + 1) * HEAD_DIM]
        for c in range(ts // GMLP_CHUNK):
            rows = slice(c * GMLP_CHUNK, (c + 1) * GMLP_CHUNK)
            cols = slice(h * HEAD_DIM, (h + 1) * HEAD_DIM)
            s = _dot(w, v[rows, cols]) + bias
            o_ref[rows, cols] = (u[rows, cols] * s).astype(o_ref.dtype)


def gmlp_call(p, ln_g, ln_b, ws, bs_b, ts=512):
    t = p.shape[0]
    return pl.pallas_call(
        functools.partial(_gmlp_kernel, ts=ts),
        grid=(t // ts,),
        in_specs=[
            pl.BlockSpec((ts, 2 * MIX_W), lambda i: (i, C0 // (2 * MIX_W))),
            pl.BlockSpec((1, MIX_W), lambda i: (0, 0)),
            pl.BlockSpec((1, MIX_W), lambda i: (0, 0)),
            pl.BlockSpec(ws.shape, lambda i: (0, 0, 0)),
            pl.BlockSpec(bs_b.shape, lambda i: (0, 0)),
        ],
        out_specs=pl.BlockSpec((ts, MIX_W), lambda i: (i, 0)),
        out_shape=jax.ShapeDtypeStruct((t, MIX_W), BF16),
        compiler_params=_cparams("parallel"),
        name="gmlp",
    )(p, ln_g, ln_b, ws, bs_b)


def _conv_kernel(cur_ref, halo_ref, w_ref, b_ref, g_ref, beta_ref, o_ref, hs_ref, sh_ref, *, ts, halo):
    i = pl.program_id(1)

    def glu(z):
        zf = z.astype(F32)
        return zf[:, :MIX_W] * _sigmoid(zf[:, MIX_W:])

    hs_ref[0:halo, :] = jnp.where(i == 0, 0.0, glu(halo_ref[...]))
    hs_ref[halo:halo + ts, :] = glu(cur_ref[...])
    sub = 8
    shift_rows = sh_ref.shape[1]
    for s in range(1, sub):
        sh_ref[s - 1] = hs_ref[s:s + shift_rows, :]
    off = halo - (CONV_W - 1)
    acc = jnp.zeros((ts, MIX_W), F32) + b_ref[...]
    for k in range(CONV_W):
        s = (off + k) % sub
        base = off + k - s
        rows = hs_ref[base:base + ts, :] if s == 0 else sh_ref[s - 1, base:base + ts, :]
        acc = acc + w_ref[k:k + 1, :] * rows
    y = _layernorm(acc, g_ref[...], beta_ref[...])
    o_ref[...] = (y * _sigmoid(y)).astype(o_ref.dtype)


def conv_call(p, w, b, ln_g, ln_b, batch, ts=256):
    t = p.shape[0]
    seq = t // batch
    nts = seq // ts
    halo = 32
    cb = D0 // (2 * MIX_W)
    return pl.pallas_call(
        functools.partial(_conv_kernel, ts=ts, halo=halo),
        grid=(batch, nts),
        in_specs=[
            pl.BlockSpec((ts, 2 * MIX_W), lambda b_, i: (b_ * nts + i, cb)),
            pl.BlockSpec((halo, 2 * MIX_W), lambda b_, i: (jnp.maximum((b_ * nts + i) * (ts // halo) - 1, 0), cb)),
            pl.BlockSpec((CONV_W, MIX_W), lambda b_, i: (0, 0)),
            pl.BlockSpec((1, MIX_W), lambda b_, i: (0, 0)),
            pl.BlockSpec((1, MIX_W), lambda b_, i: (0, 0)),
            pl.BlockSpec((1, MIX_W), lambda b_, i: (0, 0)),
        ],
        out_specs=pl.BlockSpec((ts, MIX_W), lambda b_, i: (b_ * nts + i, 0)),
        out_shape=jax.ShapeDtypeStruct((t, MIX_W), BF16),
        scratch_shapes=[pltpu.VMEM((halo + ts, MIX_W), F32), pltpu.VMEM((7, halo + ts - 8, MIX_W), F32)],
        compiler_params=_cparams("parallel", "arbitrary"),
        name="conv",
    )(p, p, w, b, ln_g, ln_b)


def _head_rms(x, g):
    xf = x.astype(F32)
    r = lax.rsqrt(jnp.mean(xf * xf, axis=-1, keepdims=True) + EPS)
    return (xf * r) * g


def _nsa_prep_kernel(q_ref, ks_ref, vs_ref, kw_ref, vw_ref, qg_ref, kg_ref, qo_ref, kso_ref, kwo_ref, vso_ref, vwo_ref, *, tr):
    qg = qg_ref[...]
    kg = kg_ref[...]
    for h in range(NSA_HEADS):
        cols = slice(h * HEAD_DIM, (h + 1) * HEAD_DIM)
        qo_ref[:, cols] = (_head_rms(q_ref[:, cols], qg) * (HEAD_DIM ** -0.5)).astype(qo_ref.dtype)
    ones = jnp.ones((VT_ROWS - HEAD_DIM, Q_BLOCK), vso_ref.dtype)
    for g in range(NSA_KV):
        cols = slice(g * HEAD_DIM, (g + 1) * HEAD_DIM)
        kso_ref[:, cols] = _head_rms(ks_ref[:, cols], kg).astype(kso_ref.dtype)
        kwo_ref[:, cols] = _head_rms(kw_ref[:, cols], kg).astype(kwo_ref.dtype)
        for u in range(tr // Q_BLOCK):
            rows = slice(u * Q_BLOCK, (u + 1) * Q_BLOCK)
            for v_ref, vo_ref in ((vs_ref, vso_ref), (vw_ref, vwo_ref)):
                vo_ref[g, u, 0:HEAD_DIM, :] = v_ref[rows, cols].astype(F32).T.astype(vo_ref.dtype)
                vo_ref[g, u, HEAD_DIM:VT_ROWS, :] = ones


def nsa_prep_call(p, q_g, k_g, batch, tr=512):
    t = p.shape[0]
    seq = t // batch
    nst = seq // tr
    kvw = NSA_KV * HEAD_DIM
    kv_spec = lambda c: pl.BlockSpec((tr, kvw), lambda i: (i, (KV0 + c * kvw) // kvw))
    vt_sds = jax.ShapeDtypeStruct((batch * NSA_KV, seq // Q_BLOCK, VT_ROWS, Q_BLOCK), BF16)
    vt_spec = pl.BlockSpec((NSA_KV, tr // Q_BLOCK, VT_ROWS, Q_BLOCK), lambda i: (i // nst, i % nst, 0, 0))
    return pl.pallas_call(
        functools.partial(_nsa_prep_kernel, tr=tr),
        grid=(t // tr,),
        in_specs=[
            pl.BlockSpec((tr, MIX_W), lambda i: (i, Q0 // MIX_W)),
            kv_spec(2), kv_spec(3), kv_spec(4), kv_spec(5),
            pl.BlockSpec((1, HEAD_DIM), lambda i: (0, 0)),
            pl.BlockSpec((1, HEAD_DIM), lambda i: (0, 0)),
        ],
        out_specs=[
            pl.BlockSpec((tr, MIX_W), lambda i: (i, 0)),
            pl.BlockSpec((tr, kvw), lambda i: (i, 0)),
            pl.BlockSpec((tr, kvw), lambda i: (i, 0)),
            vt_spec, vt_spec,
        ],
        out_shape=[jax.ShapeDtypeStruct((t, MIX_W), BF16), jax.ShapeDtypeStruct((t, kvw), BF16),
                   jax.ShapeDtypeStruct((t, kvw), BF16), vt_sds, vt_sds],
        compiler_params=_cparams("parallel"),
        name="nsa_prep",
    )(p, p, p, p, p, q_g, k_g)


def _compress_kernel(ck_ref, cv_ref, pk_ref, pv_ref, w1k_ref, w2k_ref, w1v_ref, w2v_ref, kg_ref, ko_ref, vo_ref, *, nc):
    def comp(c_ref, p_ref, w1_ref, w2_ref):
        c = c_ref[...].astype(F32)
        xa = (c + p_ref[0:1, :]).astype(BF16)
        xb = (c + p_ref[1:2, :]).astype(BF16)
        first = _dot(xa, w1_ref[0])
        second = _dot(xb, w1_ref[1])
        hdn = _gelu_tanh(first + pltpu.roll(second, nc - 1, 0))
        return _dot(hdn.astype(BF16), w2_ref[...])

    row = lax.broadcasted_iota(jnp.int32, (nc, HEAD_DIM), 0)
    real = row < nc - 1
    kc = _head_rms(comp(ck_ref, pk_ref, w1k_ref, w2k_ref), kg_ref[...])
    vc = comp(cv_ref, pv_ref, w1v_ref, w2v_ref)
    zeros = jnp.zeros((KC_PAD, HEAD_DIM), F32)
    ko_ref[0:KC_PAD, :] = zeros
    vo_ref[0:KC_PAD, :] = zeros
    ko_ref[KC_PAD:KC_PAD + nc, :] = jnp.where(real, kc, 0.0)
    vo_ref[KC_PAD:KC_PAD + nc, :] = jnp.where(real, vc, 0.0)


def compress_call(ck, cv, pos_k, pos_v, w1k, w2k, w1v, w2v, k_g):
    bg, nc, half = ck.shape
    full2 = lambda shape: pl.BlockSpec(shape, lambda i: (0,) * len(shape))
    out_sds = jax.ShapeDtypeStruct((bg, KC_PAD + nc, HEAD_DIM), F32)
    return pl.pallas_call(
        functools.partial(_compress_kernel, nc=nc),
        grid=(bg,),
        in_specs=[
            pl.BlockSpec((None, nc, half), lambda i: (i, 0, 0)),
            pl.BlockSpec((None, nc, half), lambda i: (i, 0, 0)),
            full2((2, half)), full2((2, half)),
            full2((2, half, HEAD_DIM)), full2((HEAD_DIM, HEAD_DIM)),
            full2((2, half, HEAD_DIM)), full2((HEAD_DIM, HEAD_DIM)),
            full2((1, HEAD_DIM)),
        ],
        out_specs=[pl.BlockSpec((None, KC_PAD + nc, HEAD_DIM), lambda i: (i, 0, 0))] * 2,
        out_shape=[out_sds, out_sds],
        compiler_params=_cparams("parallel"),
        name="nsa_compress",
    )(ck, cv, pos_k, pos_v, w1k, w2k, w1v, w2v, k_g)


def _softmax_rows(parts):
    m = parts[0].max(axis=-1, keepdims=True)
    for x in parts[1:]:
        m = jnp.maximum(m, x.max(axis=-1, keepdims=True))
    m = jnp.maximum(m, -1e20)
    ps = [jnp.exp(x - m) for x in parts]
    s = ps[0].sum(axis=-1, keepdims=True)
    for p in ps[1:]:
        s = s + p.sum(axis=-1, keepdims=True)
    inv = 1.0 / jnp.maximum(s, 1e-30)
    return ps, inv


def _split_hi_lo(x):
    hi = x.astype(BF16)
    lo = (x - hi.astype(F32)).astype(BF16)
    return hi, lo


def _nsa_kernel(q_ref, kc_ref, vc_ref, ks_ref, vs_ref, kw_ref, vw_ref, ng_ref,
                bcn_ref, bs_ref, bw_ref, ov_ref, e_ref, o_ref, m_ref, acc_ref, ow_ref):
    i = pl.program_id(2)
    qb = Q_BLOCK
    rows = NSA_HPG * qb
    hd = HEAD_DIM
    qs = jnp.concatenate([q_ref[:, r * hd:(r + 1) * hd] for r in range(NSA_HPG)], axis=0)
    kcp = kc_ref.shape[0]

    nwin = WINDOW // qb + 1
    wtiles = [jnp.maximum(i - (nwin - 1 - tt), 0) for tt in range(nwin)]
    k_win = jnp.concatenate([kw_ref[pl.ds(pl.multiple_of(wt * qb, qb), qb), :] for wt in wtiles], axis=0)
    ltw = _dot_nt(k_win, qs) + bw_ref[...]
    wrow = lax.broadcasted_iota(jnp.int32, ltw.shape, 0)
    ltw = jnp.where(wrow >= (nwin - 1 - i) * qb, ltw, MASKED)
    mw = jnp.max(ltw, axis=0, keepdims=True)
    pw = _exp_bf16(ltw - mw)
    acc_w = _dot(jnp.concatenate([vw_ref[wt] for wt in wtiles], axis=1), pw)
    ow_ref[...] = acc_w[0:hd] * (1.0 / jnp.maximum(acc_w[hd:hd + 1], 1e-30))

    near0 = pl.multiple_of(8 * i + 8, 8)
    lf = _dot_nt(qs, kc_ref[...].astype(BF16))
    npad = lax.broadcasted_iota(jnp.int32, (rows, kcp), 1)
    lf = jnp.where((npad >= KC_PAD) & (npad < near0), lf, MASKED)
    ln = _dot_nt(qs, kc_ref[pl.ds(near0, qb), :].astype(BF16)) + bcn_ref[...]
    ncol = lax.broadcasted_iota(jnp.int32, (rows, qb), 1)
    ln = jnp.where(ncol >= KC_PAD - 8 - 8 * i, ln, MASKED)
    (pf, pn), inv = _softmax_rows([lf, ln])
    pf = pf * inv
    pn = pn * inv
    o_c = _dot(pf.astype(BF16), vc_ref[...].astype(BF16)) + _dot(pn.astype(BF16), vc_ref[pl.ds(near0, qb), :].astype(BF16))

    pf_g = pf[0:qb]
    pn_g = pn[0:qb]
    for r in range(1, NSA_HPG):
        pf_g = pf_g + pf[r * qb:(r + 1) * qb]
        pn_g = pn_g + pn[r * qb:(r + 1) * qb]
    ov_all = ov_ref[...].astype(BF16)
    ov_near = ov_ref[pl.ds(near0, qb), :].astype(BF16)
    imp = jnp.zeros((qb, ov_ref.shape[1]), F32)
    for part in _split_hi_lo(pf_g):
        imp = imp + _dot(part, ov_all)
    for part in _split_hi_lo(pn_g):
        imp = imp + _dot(part, ov_near)
    nblk = imp.shape[1]
    imp_t = imp.T
    jrow_i = lax.broadcasted_iota(jnp.int32, imp_t.shape, 0)
    cur = jnp.right_shift(lax.broadcasted_iota(jnp.int32, imp_t.shape, 1) + i * qb, 6)
    forced = (jrow_i == 0) | (jrow_i == cur) | (jrow_i == cur - 1)
    sc = jnp.where((jrow_i >= 1) & (jrow_i <= cur - 2), imp_t, -3e38)
    jrow = jrow_i.astype(F32)
    picked = jnp.zeros(sc.shape, F32)
    for _ in range(SLC_TOPK - 3):
        mx = jnp.max(sc, axis=0, keepdims=True)
        first = jnp.min(jnp.where(sc == mx, jrow, float(nblk)), axis=0, keepdims=True)
        pick = jrow == first
        picked = jnp.where(pick, 1.0, picked)
        sc = jnp.where(pick, -3e38, sc)
    sel_t = jnp.where(cur < SLC_TOPK, jnp.where(jrow_i <= cur, 1.0, 0.0), jnp.where(forced, 1.0, picked))
    sel = sel_t.T

    sel_neg = jnp.where(sel > 0.5, 0.0, MASKED).astype(BF16)
    q_aug = jnp.concatenate([qs, jnp.concatenate([sel_neg] * NSA_HPG, axis=0)], axis=1)

    def block_onehot(e_rows, valid):
        return jnp.where(valid, e_rows, jnp.ones_like(e_rows))

    prev_t = jnp.maximum(i - 1, 0)
    prev0 = pl.multiple_of(prev_t * qb, qb)
    diag0 = pl.multiple_of(i * qb, qb)
    k_near = jnp.concatenate([ks_ref[pl.ds(prev0, qb), :], ks_ref[pl.ds(diag0, qb), :]], axis=0)
    e_near = jnp.concatenate([e_ref[pl.ds(prev0, qb), :], e_ref[pl.ds(diag0, qb), :]], axis=0)
    row_near = lax.broadcasted_iota(jnp.int32, e_near.shape, 0)
    e_near = block_onehot(e_near, row_near >= jnp.where(i == 0, qb, 0))
    lt = _dot_nt(jnp.concatenate([k_near, e_near], axis=1), q_aug) + bs_ref[...]
    m0 = jnp.maximum(jnp.max(lt, axis=0, keepdims=True), -1e20).astype(BF16).astype(F32)
    p0 = _exp_bf16(lt - m0)
    vt_near = jnp.concatenate([vs_ref[prev_t], vs_ref[i]], axis=1)
    m_ref[...] = m0
    acc_ref[...] = _dot(vt_near, p0)

    far_end = (i - 1) * qb
    tiles = 4
    chunk = tiles * qb
    halves = 2

    def far_body(c, carry):
        lts, vts = [], []
        for hf in range(halves):
            t0 = (c * halves + hf) * tiles
            c0 = pl.multiple_of(t0 * qb, chunk)
            e_c = e_ref[pl.ds(c0, chunk), :]
            row_c = lax.broadcasted_iota(jnp.int32, e_c.shape, 0) + c0
            k_aug = jnp.concatenate([ks_ref[pl.ds(c0, chunk), :], block_onehot(e_c, row_c < far_end)], axis=1)
            lts.append(_dot_nt(k_aug, q_aug).astype(BF16))
            vts.append(jnp.concatenate([vs_ref[t0 + u] for u in range(tiles)], axis=1))
        m_old = m_ref[...]
        m_new = m_old
        for lt_c in lts:
            m_new = jnp.maximum(m_new, jnp.max(lt_c, axis=0, keepdims=True).astype(F32))
        acc = jnp.exp(m_old - m_new) * acc_ref[...]
        m_new_b = m_new.astype(BF16)
        for lt_c, vt_c in zip(lts, vts):
            acc = acc + _dot(vt_c, jnp.exp(lt_c - m_new_b))
        m_ref[...] = m_new
        acc_ref[...] = acc
        return carry

    lax.fori_loop(0, (i - 1 + halves * tiles - 1) // (halves * tiles), far_body, 0)
    acc_s = acc_ref[...]
    o_s_t = acc_s[0:hd] * (1.0 / jnp.maximum(acc_s[hd:hd + 1], 1e-30))
    o_w_t = ow_ref[...]

    gate = _sigmoid(ng_ref[...].astype(F32))
    gate_t = gate.T
    for r in range(NSA_HPG):
        rs = slice(r * qb, (r + 1) * qb)
        o_sw_t = gate_t[3 * r + 1:3 * r + 2, :] * o_s_t[:, rs] + gate_t[3 * r + 2:3 * r + 3, :] * o_w_t[:, rs]
        o = gate[:, 3 * r:3 * r + 1] * o_c[rs] + o_sw_t.T
        o_ref[:, r * hd:(r + 1) * hd] = o.astype(o_ref.dtype)


def nsa_call(p, qn, ksn, kwn, vst, vwt, kcp, vcp, bcn, bsl_t, bwn_t, ovl, e_tab, batch):
    t = p.shape[0]
    seq = t // batch
    nq = seq // Q_BLOCK
    assert nq % 8 == 0
    g_ = NSA_KV
    kcp_rows = kcp.shape[1]
    rows = NSA_HPG * Q_BLOCK
    ng_blk = NG0 // LANES
    seq_spec = pl.BlockSpec((seq, HEAD_DIM), lambda b, g, i: (b, g))
    vt_spec = pl.BlockSpec((None, nq, VT_ROWS, Q_BLOCK), lambda b, g, i: (b * g_ + g, 0, 0, 0))
    cmp_spec = pl.BlockSpec((None, kcp_rows, HEAD_DIM), lambda b, g, i: (b * g_ + g, 0, 0))
    tbl_spec = lambda arr: pl.BlockSpec((None,) + arr.shape[1:], lambda b, g, i: (g, 0, 0))
    return pl.pallas_call(
        _nsa_kernel,
        grid=(batch, g_, nq),
        in_specs=[
            pl.BlockSpec((Q_BLOCK, rows), lambda b, g, i: (b * nq + i, g)),
            cmp_spec, cmp_spec,
            seq_spec, vt_spec, seq_spec, vt_spec,
            pl.BlockSpec((Q_BLOCK, LANES), lambda b, g, i: (b * nq + i, ng_blk + g)),
            tbl_spec(bcn), tbl_spec(bsl_t), tbl_spec(bwn_t),
            pl.BlockSpec(ovl.shape, lambda b, g, i: (0, 0)),
            pl.BlockSpec(e_tab.shape, lambda b, g, i: (0, 0)),
        ],
        out_specs=pl.BlockSpec((Q_BLOCK, rows), lambda b, g, i: (b * nq + i, g)),
        out_shape=jax.ShapeDtypeStruct((t, MIX_W), BF16),
        scratch_shapes=[pltpu.VMEM((1, rows), F32), pltpu.VMEM((VT_ROWS, rows), F32), pltpu.VMEM((HEAD_DIM, rows), F32)],
        compiler_params=_cparams("parallel", "parallel", "arbitrary"),
        name="nsa_attention",
    )(qn, kcp, vcp, ksn, vst, kwn, vwt, p, bcn, bsl_t, bwn_t, ovl, e_tab)


def _t5_bucket_np(dist):
    n = np.maximum(dist, 0)
    max_exact = REL_BUCKETS // 2
    nf = np.maximum(n, 1).astype(np.float32)
    large = max_exact + (np.log(nf / max_exact) / math.log(REL_MAX_DIST / max_exact)
                         * (REL_BUCKETS - max_exact)).astype(np.int32)
    large = np.minimum(large, REL_BUCKETS - 1)
    return np.where(n < max_exact, n, large)


def _bias_table_kernel(rb_ref, idx_ref, o_ref):
    h = pl.program_id(0)
    idx = idx_ref[...]
    far = rb_ref[REL_BUCKETS - 1, h]
    acc = jnp.zeros(idx.shape, F32)
    for b in range(REL_BUCKETS):
        acc = jnp.where(idx == b, rb_ref[b, h] - far, acc)
    o_ref[...] = jnp.where(idx < 0, MASKED, acc)


def bias_table_call(rel_bias, bucket):
    nh = rel_bias.shape[1]
    return pl.pallas_call(
        _bias_table_kernel,
        grid=(nh,),
        in_specs=[pl.BlockSpec(memory_space=pltpu.SMEM), pl.BlockSpec(bucket.shape, lambda h: (0, 0))],
        out_specs=pl.BlockSpec((None,) + bucket.shape, lambda h: (h, 0, 0)),
        out_shape=jax.ShapeDtypeStruct((nh,) + bucket.shape, F32),
        compiler_params=_cparams("arbitrary"),
        name="bias_tables",
    )(rel_bias, bucket)


def _bias_tables(rel_bias, seq):
    qb = Q_BLOCK
    iq = np.arange(qb)[:, None]
    jk = np.arange(qb)[None, :]
    d_diag = iq - jk
    d_prev = qb + iq - jk
    d_edge = WINDOW + iq - jk
    d_far = np.full((qb, qb), 2 * qb)
    d_sel = np.concatenate([d_prev, d_diag], axis=1)
    d_win = np.concatenate([d_edge] + [d_far] * (WINDOW // qb - 2) + [d_prev, d_diag], axis=1)
    d_cmp = iq - CMP_STRIDE * (jk - (KC_PAD - 8)) - (CMP_LEN - 1)
    parts = [(d_cmp, d_cmp >= 0), (d_sel, d_sel >= 0), (d_win, (d_win >= 0) & (d_win < WINDOW))]
    bucket = np.concatenate([np.where(vis, _t5_bucket_np(dist), -1) for dist, vis in parts], axis=1).astype(np.int32)
    tables = bias_table_call(rel_bias, jnp.asarray(bucket))
    tables = tables.reshape(NSA_KV, NSA_HPG * qb, bucket.shape[1])
    w_cmp, w_sel = d_cmp.shape[1], d_sel.shape[1]
    bcn = tables[:, :, :w_cmp]
    bsl_t = tables[:, :, w_cmp:w_cmp + w_sel].transpose(0, 2, 1)
    bwn_t = tables[:, :, w_cmp + w_sel:].transpose(0, 2, 1)
    nc = seq // CMP_STRIDE
    n_slc = seq // SLC_LEN
    c0 = (np.arange(nc) * CMP_STRIDE)[:, None]
    s0 = (np.arange(n_slc) * SLC_LEN)[None, :]
    ov = ((c0 < s0 + SLC_LEN) & (c0 + CMP_LEN > s0) & (np.arange(nc)[:, None] < nc - 1)).astype(np.float32)
    ovl = np.concatenate([np.zeros((KC_PAD, n_slc), np.float32), ov], axis=0)
    e_tab = (np.arange(seq)[:, None] // SLC_LEN == np.arange(n_slc)[None, :]).astype(np.float32)
    return bcn, bsl_t, bwn_t, jnp.asarray(ovl), jnp.asarray(e_tab, dtype=BF16)


def _merge_kernel(g_ref, a_ref, b_ref, c_ref, d_ref, wgu_ref, wbr_ref, o_ref):
    gin = g_ref[...]
    acc = None
    for bi, br in enumerate((a_ref, b_ref, c_ref, d_ref)):
        gate = _sigmoid(_dot(gin, wgu_ref[bi]))
        term = gate * _dot(br[...], wbr_ref[bi])
        acc = term if acc is None else acc + term
    o_ref[...] = acc.astype(o_ref.dtype)


def merge_call(p, branches, wgu, wbr, layer, tm=1024, tn=512):
    t = p.shape[0]
    d = wgu.shape[-1]
    row = lambda w, cb: pl.BlockSpec((tm, w), lambda i, j: (i, cb))
    return pl.pallas_call(
        _merge_kernel,
        grid=(t // tm, d // tn),
        in_specs=[row(GATE_RANK, G0 // GATE_RANK)] + [row(MIX_W, 0)] * 4 + [
            pl.BlockSpec((None, 4, GATE_RANK, tn), lambda i, j: (layer, 0, 0, j)),
            pl.BlockSpec((None, 4, MIX_W, tn), lambda i, j: (layer, 0, 0, j)),
        ],
        out_specs=pl.BlockSpec((tm, tn), lambda i, j: (i, j)),
        out_shape=jax.ShapeDtypeStruct((t, d), BF16),
        compiler_params=_cparams("parallel", "arbitrary"),
        name="merge",
    )(p, *branches, wgu, wbr)


def _rearranged_w_in(w):
    o_a, o_q, o_kv, o_ng, o_c, o_d, o_g = 0, 1024, 2048, 3584, 3608, 5656, 7704
    d = w.shape[0]
    hpg3 = NSA_HPG * 3
    zpad = lambda n: jnp.zeros((d, n), BF16)
    parts = [
        w[:, o_c:o_c + 2048], w[:, o_d:o_d + 2048], w[:, o_a:o_a + 1024],
        w[:, o_q:o_q + 1024], w[:, o_kv:o_kv + 1536], w[:, o_g:o_g + 512],
        w[:, o_ng:o_ng + hpg3], zpad(LANES - hpg3),
        w[:, o_ng + hpg3:o_ng + 2 * hpg3], zpad(LANES - hpg3), zpad(PCOLS - NG0 - 2 * LANES),
    ]
    return jnp.concatenate([part.astype(BF16) for part in parts], axis=-1)


def kernel(x, rel_bias, norm_mix_g, norm_ffn_g, w_in, pool_w, pool_scale, q_norm_g, k_norm_g, cmp_pos_k, cmp_w1_k, cmp_w2_k, cmp_pos_v, cmp_w1_v, cmp_w2_v, gmlp_ln_g, gmlp_ln_b, gmlp_ws, gmlp_bs, conv_w, conv_b, conv_ln_g, conv_ln_b, w_branch, w_gate_up, w_out, ffn_w_gate, ffn_w_up, ffn_w_down, moe_router, moe_router_b, moe_w_gate, moe_w_up, moe_w_down):
    batch, seq, d = x.shape
    t = batch * seq
    depth = w_in.shape[0]
    nc = seq // CMP_STRIDE
    half = CMP_STRIDE * HEAD_DIM
    xf = x.reshape(t, d)

    bcn, bsl_t, bwn_t, ovl, e_tab = _bias_tables(rel_bias, seq)
    row = lambda v: v.reshape(1, -1)
    w_gate_up_b, w_branch_b, w_out_b = (w.astype(BF16) for w in (w_gate_up, w_branch, w_out))
    ffn_w_gate_b, ffn_w_up_b, ffn_w_down_b = (w.astype(BF16) for w in (ffn_w_gate, ffn_w_up, ffn_w_down))
    moe_w_gate_b, moe_w_up_b, moe_w_down_b = (w.astype(BF16) for w in (moe_w_gate, moe_w_up, moe_w_down))

    for l in range(depth):
        h = rmsnorm_call(xf, row(norm_mix_g[l]))
        p = matmul_call(h, _rearranged_w_in(w_in[l]), name="in_proj")
        o_a = pool_call(p, pool_w[l].astype(BF16), row(pool_scale[l]), batch)
        qn, ksn, kwn, vst, vwt = nsa_prep_call(p, row(q_norm_g[l]), row(k_norm_g[l]), batch)
        kvc = p[:, KV0:KV0 + 2 * NSA_KV * HEAD_DIM].reshape(batch, nc, CMP_STRIDE, 2, NSA_KV, HEAD_DIM)
        kvc = kvc.transpose(3, 0, 4, 1, 2, 5).reshape(2, batch * NSA_KV, nc, half)
        kcp, vcp = compress_call(
            kvc[0], kvc[1], cmp_pos_k[l].reshape(2, half), cmp_pos_v[l].reshape(2, half),
            cmp_w1_k[l].reshape(2, half, HEAD_DIM).astype(BF16), cmp_w2_k[l].astype(BF16),
            cmp_w1_v[l].reshape(2, half, HEAD_DIM).astype(BF16), cmp_w2_v[l].astype(BF16), row(k_norm_g[l]))
        o_b = nsa_call(p, qn, ksn, kwn, vst, vwt, kcp, vcp, bcn, bsl_t, bwn_t, ovl, e_tab, batch)
        bs_b = jnp.repeat(gmlp_bs[l].T, HEAD_DIM, axis=1)
        o_c = gmlp_call(p, row(gmlp_ln_g[l]), row(gmlp_ln_b[l]), gmlp_ws[l], bs_b)
        o_d = conv_call(p, conv_w[l], row(conv_b[l]), row(conv_ln_g[l]), row(conv_ln_b[l]), batch)
        mix = merge_call(p, (o_a, o_b, o_c, o_d), w_gate_up_b, w_branch_b, l)
        xf = matmul_call(mix, w_out_b, layer=l, res=xf, out_dtype=F32, name="out_proj")
        i = l // 2
        if l % 2 == 0:
            h = rmsnorm_call(xf, row(norm_ffn_g[l]))
            act = swiglu_call(h, ffn_w_gate_b, ffn_w_up_b, i)
            xf = matmul_call(act, ffn_w_down_b, layer=i, res=xf, out_dtype=F32, name="ffn_down")
        else:
            h, hp = rmsnorm_call(xf, row(norm_ffn_g[l]), packed=True)
            rw = jnp.pad(moe_router[i], ((0, 0), (0, LANES - N_EXPERTS))).astype(BF16)
            rb = jnp.pad(moe_router_b[i], (0, LANES - N_EXPERTS)).reshape(1, LANES)
            xf = moe_layer(xf, h, hp, rw, rb, moe_w_gate_b, moe_w_up_b, moe_w_down_b, i)
    return xf.reshape(batch, seq, d)
```

```python
import functools
import math

import jax
import jax.numpy as jnp
import numpy as np
from jax import lax
from jax.experimental import pallas as pl
from jax.experimental.pallas import tpu as pltpu

F32 = jnp.float32
BF16 = jnp.bfloat16

VMEM_LIMIT_BYTES = 56 * 1024 * 1024
LANES = 128

EPS = 1e-6
HEAD_DIM = 128
MIX_W = 1024
POOL_WINDOWS = (2, 4, 8, 16)
POOL_GW = MIX_W // len(POOL_WINDOWS)
NSA_HEADS = 8
NSA_KV = 2
NSA_HPG = NSA_HEADS // NSA_KV
CMP_LEN = 32
CMP_STRIDE = 16
SLC_LEN = 64
SLC_TOPK = 16
WINDOW = 512
Q_BLOCK = 128
FORCE_BONUS = 1e4
NEG = -1e9
MASKED = -1e30
GMLP_CHUNK = 128
CONV_W = 31
GATE_RANK = 512
REL_BUCKETS = 32
REL_MAX_DIST = 128
N_EXPERTS = 8
TOP_K = 2

C0 = 0
D0 = 2048
A0 = 4096
Q0 = 5120
KV0 = 6144
G0 = 7680
NG0 = 8192
PCOLS = 8704
KC_PAD = 128
VT_ROWS = HEAD_DIM + 16


def _cparams(*sem):
    return pltpu.CompilerParams(dimension_semantics=sem, vmem_limit_bytes=VMEM_LIMIT_BYTES)


def _dot(a, b):
    return jnp.dot(a, b, preferred_element_type=F32)


def _dot_nt(a, b):
    return lax.dot_general(a, b, (((1,), (1,)), ((), ())), preferred_element_type=F32)


def _gelu_tanh(x):
    return 0.5 * x * (1.0 + jnp.tanh(math.sqrt(2.0 / math.pi) * (x + 0.044715 * (x * x * x))))


def _sigmoid(x):
    return 1.0 / (1.0 + jnp.exp(-x))


def _exp_bf16(x):
    return jnp.exp(x.astype(BF16))


def _rmsnorm_kernel(x_ref, g_ref, o_ref, *packed_ref):
    xf = x_ref[...]
    r = lax.rsqrt(jnp.mean(xf * xf, axis=-1, keepdims=True) + EPS)
    y = (xf * r) * g_ref[...]
    o_ref[...] = y.astype(o_ref.dtype)
    if packed_ref:
        half = y.shape[1] // 2
        packed_ref[0][...] = _pack_bf16_pairs(y[:, :half], y[:, half:])


def rmsnorm_call(x, g, tr=256, packed=False):
    t, d = x.shape
    out_specs = [pl.BlockSpec((tr, d), lambda i: (i, 0))]
    out_shape = [jax.ShapeDtypeStruct((t, d), BF16)]
    if packed:
        out_specs.append(pl.BlockSpec((tr, d // 2), lambda i: (i, 0)))
        out_shape.append(jax.ShapeDtypeStruct((t, d // 2), jnp.uint32))
    out = pl.pallas_call(
        _rmsnorm_kernel,
        grid=(t // tr,),
        in_specs=[pl.BlockSpec((tr, d), lambda i: (i, 0)), pl.BlockSpec((1, d), lambda i: (0, 0))],
        out_specs=out_specs,
        out_shape=out_shape,
        compiler_params=_cparams("parallel"),
        name="rmsnorm",
    )(x, g)
    return out if packed else out[0]


def _mm_kernel(*refs, nk, has_res):
    if has_res:
        a_ref, w_ref, r_ref, o_ref = refs[:4]
        scr = refs[4:]
    else:
        a_ref, w_ref, o_ref = refs[:3]
        r_ref = None
        scr = refs[3:]
    part = _dot(a_ref[...], w_ref[...])

    def finish(acc):
        if has_res:
            acc = acc + r_ref[...]
        o_ref[...] = acc.astype(o_ref.dtype)

    if nk == 1:
        finish(part)
    else:
        acc_ref = scr[0]
        k = pl.program_id(2)

        @pl.when(k == 0)
        def _():
            acc_ref[...] = part

        @pl.when(k > 0)
        def _():
            acc_ref[...] += part

        @pl.when(k == nk - 1)
        def _():
            finish(acc_ref[...])


def matmul_call(a, w, layer=None, res=None, out_dtype=BF16, tm=1024, tn=512, tk=4096, name="matmul"):
    m, kdim = a.shape
    n = w.shape[-1]
    tk = min(tk, kdim)
    nk = kdim // tk
    assert m % tm == 0 and n % tn == 0 and kdim % tk == 0
    if layer is None:
        w_spec = pl.BlockSpec((tk, tn), lambda i, j, k: (k, j))
    else:
        w_spec = pl.BlockSpec((None, tk, tn), lambda i, j, k: (layer, k, j))
    in_specs = [pl.BlockSpec((tm, tk), lambda i, j, k: (i, k)), w_spec]
    args = [a, w]
    if res is not None:
        in_specs.append(pl.BlockSpec((tm, tn), lambda i, j, k: (i, j)))
        args.append(res)
    return pl.pallas_call(
        functools.partial(_mm_kernel, nk=nk, has_res=res is not None),
        grid=(m // tm, n // tn, nk),
        in_specs=in_specs,
        out_specs=pl.BlockSpec((tm, tn), lambda i, j, k: (i, j)),
        out_shape=jax.ShapeDtypeStruct((m, n), out_dtype),
        scratch_shapes=[pltpu.VMEM((tm, tn), F32)] if nk > 1 else [],
        compiler_params=_cparams("parallel", "parallel", "arbitrary"),
        name=name,
    )(*args)


def _swiglu_kernel(h_ref, wg_ref, wu_ref, o_ref):
    h = h_ref[...]
    hg = _dot(h, wg_ref[...])
    hu = _dot(h, wu_ref[...])
    o_ref[...] = (hg * _sigmoid(hg) * hu).astype(o_ref.dtype)


def swiglu_call(h, wg, wu, layer, tm=1024, tn=512):
    t, d = h.shape
    f = wg.shape[-1]
    assert f % tn == 0 and t % tm == 0
    return pl.pallas_call(
        _swiglu_kernel,
        grid=(t // tm, f // tn),
        in_specs=[
            pl.BlockSpec((tm, d), lambda i, j: (i, 0)),
            pl.BlockSpec((None, d, tn), lambda i, j: (layer, 0, j)),
            pl.BlockSpec((None, d, tn), lambda i, j: (layer, 0, j)),
        ],
        out_specs=pl.BlockSpec((tm, tn), lambda i, j: (i, j)),
        out_shape=jax.ShapeDtypeStruct((t, f), BF16),
        compiler_params=_cparams("parallel", "arbitrary"),
        name="swiglu",
    )(h, wg, wu)


def _router_kernel(h_ref, w_ref, b_ref, info_ref, cnt_ref, run_ref, *, tm):
    @pl.when(pl.program_id(0) == 0)
    def _():
        run_ref[...] = jnp.zeros_like(run_ref)

    logits = _dot(h_ref[...], w_ref[...]) + b_ref[...]
    lane = lax.broadcasted_iota(jnp.int32, logits.shape, 1)
    logits = jnp.where(lane < N_EXPERTS, logits, MASKED)
    v1 = jnp.max(logits, axis=-1, keepdims=True)
    i1 = jnp.min(jnp.where(logits == v1, lane, LANES), axis=-1, keepdims=True)
    rest = jnp.where(lane == i1, MASKED, logits)
    v2 = jnp.max(rest, axis=-1, keepdims=True)
    i2 = jnp.min(jnp.where(rest == v2, lane, LANES), axis=-1, keepdims=True)
    e2 = jnp.exp(v2 - v1)
    inv = 1.0 / (1.0 + e2)
    chosen = jnp.where((lane == i1) | (lane == i2), 1.0, 0.0)
    earlier = lax.broadcasted_iota(jnp.int32, (tm, tm), 1) < lax.broadcasted_iota(jnp.int32, (tm, tm), 0)
    rank = _dot(jnp.where(earlier, 1.0, 0.0).astype(BF16), chosen.astype(BF16)) + run_ref[...]
    run_ref[...] += jnp.sum(chosen, axis=0, keepdims=True)
    cnt_ref[...] = run_ref[...]
    r1 = jnp.sum(jnp.where(lane == i1, rank, 0.0), axis=-1, keepdims=True)
    r2 = jnp.sum(jnp.where(lane == i2, rank, 0.0), axis=-1, keepdims=True)
    cols = (i1.astype(F32), i2.astype(F32), inv, e2 * inv, r1, r2)
    info = jnp.zeros(logits.shape, F32)
    for c, col in enumerate(cols):
        info = jnp.where(lane == c, col, info)
    info_ref[...] = info


def router_call(h, w, b, tm=1024):
    t, d = h.shape
    return pl.pallas_call(
        functools.partial(_router_kernel, tm=tm),
        grid=(t // tm,),
        in_specs=[pl.BlockSpec((tm, d), lambda i: (i, 0)), pl.BlockSpec((d, LANES), lambda i: (0, 0)),
                  pl.BlockSpec((1, LANES), lambda i: (0, 0))],
        out_specs=[pl.BlockSpec((tm, LANES), lambda i: (i, 0)), pl.BlockSpec((1, LANES), lambda i: (0, 0))],
        out_shape=[jax.ShapeDtypeStruct((t, LANES), F32), jax.ShapeDtypeStruct((1, LANES), F32)],
        scratch_shapes=[pltpu.VMEM((1, LANES), F32)],
        compiler_params=_cparams("arbitrary"),
        name="router",
    )(h, w, b)


def _pack_bf16_pairs(lo, hi):
    lo_bits = pltpu.bitcast(lo.astype(BF16).astype(F32), jnp.uint32)
    hi_bits = pltpu.bitcast(hi.astype(BF16).astype(F32), jnp.uint32)
    return jnp.right_shift(lo_bits, jnp.uint32(16)) | (hi_bits & jnp.uint32(0xFFFF0000))


def _unpack_bf16_pairs(u):
    lo = pltpu.bitcast(jnp.left_shift(u, jnp.uint32(16)), F32)
    hi = pltpu.bitcast(u & jnp.uint32(0xFFFF0000), F32)
    return lo, hi


def _row_copy(src_ref, src_row, dst_ref, dst_row, sem):
    return pltpu.make_async_copy(src_ref.at[pl.ds(src_row, 1), :], dst_ref.at[pl.ds(dst_row, 1), :], sem)


def _dispatch_kernel(row_token_ref, hp_ref, xs_ref, sem, *, tt):
    i = pl.program_id(0)

    def start(t2, c):
        for prio in range(2):
            t = 2 * t2 + prio
            _row_copy(hp_ref, row_token_ref[i * tt + t], xs_ref, t, sem).start(priority=prio)
        return c

    def wait(t, c):
        _row_copy(hp_ref, 0, xs_ref, 0, sem).wait()
        return c

    lax.fori_loop(0, tt // 2, start, 0, unroll=4)
    lax.fori_loop(0, tt, wait, 0, unroll=8)


def moe_dispatch_call(row_token, hp, tt=256):
    n_rows = row_token.shape[0]
    half = hp.shape[1]
    return pl.pallas_call(
        functools.partial(_dispatch_kernel, tt=tt),
        grid_spec=pltpu.PrefetchScalarGridSpec(
            num_scalar_prefetch=1,
            grid=(n_rows // tt,),
            in_specs=[pl.BlockSpec(memory_space=pl.ANY)],
            out_specs=pl.BlockSpec((tt, half), lambda i, rt: (i, 0)),
            scratch_shapes=[pltpu.SemaphoreType.DMA(())],
        ),
        out_shape=jax.ShapeDtypeStruct((n_rows, half), jnp.uint32),
        compiler_params=_cparams("arbitrary"),
        name="moe_dispatch",
    )(row_token, hp)


def _moe_ffn_kernel(te_ref, nv_ref, xs_ref, wg_ref, wu_ref, wdl_ref, wdh_ref, y_ref, xb_ref, act_ref, *, nj, half):
    r = pl.program_id(0)
    s = pl.program_id(1)
    valid = r < nv_ref[0]
    up_phase = s < nj

    @pl.when(valid & (s == 0))
    def _():
        lo, hi = _unpack_bf16_pairs(xs_ref[...])
        xb_ref[:, :half] = lo.astype(BF16)
        xb_ref[:, half:] = hi.astype(BF16)

    @pl.when(valid & up_phase)
    def _():
        xb = xb_ref[...]
        hg = _dot(xb, wg_ref[...])
        hu = _dot(xb, wu_ref[...])
        act_ref[s] = (hg * _sigmoid(hg) * hu).astype(BF16)

    @pl.when(valid & jnp.logical_not(up_phase))
    def _():
        act = jnp.concatenate([act_ref[k] for k in range(nj)], axis=1)
        y_ref[...] = _pack_bf16_pairs(_dot(act, wdl_ref[...]), _dot(act, wdh_ref[...]))

    @pl.when(jnp.logical_not(valid) & jnp.logical_not(up_phase))
    def _():
        y_ref[...] = jnp.zeros_like(y_ref)


def moe_ffn_call(tile_expert, n_valid, xs, wg, wu, wd, layer, tm, tf=256, tn=512):
    n_rows, half = xs.shape
    d, f = wg.shape[-2:]
    nj = f // tf
    nb = half // tn
    assert f % tf == 0 and half % tn == 0 and n_rows % tm == 0 and d == 2 * half

    def tile(r, nv):
        return jnp.minimum(r, nv[0] - 1)

    def up_blk(r, s, nv):
        return jnp.where(r < nv[0], jnp.minimum(s, nj - 1), nj - 1)

    def down_blk(r, s, nv):
        return jnp.where(r < nv[0], jnp.maximum(s - nj, 0), nb - 1)

    up_spec = pl.BlockSpec((None, None, d, tf), lambda r, s, te, nv: (layer, te[tile(r, nv)], 0, up_blk(r, s, nv)))
    return pl.pallas_call(
        functools.partial(_moe_ffn_kernel, nj=nj, half=half),
        grid_spec=pltpu.PrefetchScalarGridSpec(
            num_scalar_prefetch=2,
            grid=(n_rows // tm, nj + nb),
            in_specs=[
                pl.BlockSpec((tm, half), lambda r, s, te, nv: (tile(r, nv), 0)),
                up_spec, up_spec,
                pl.BlockSpec((None, None, f, tn), lambda r, s, te, nv: (layer, te[tile(r, nv)], 0, down_blk(r, s, nv))),
                pl.BlockSpec((None, None, f, tn), lambda r, s, te, nv: (layer, te[tile(r, nv)], 0, nb + down_blk(r, s, nv))),
            ],
            out_specs=pl.BlockSpec((tm, tn), lambda r, s, te, nv: (r, jnp.maximum(s - nj, 0))),
            scratch_shapes=[pltpu.VMEM((tm, d), BF16), pltpu.VMEM((nj, tm, tf), BF16)],
        ),
        out_shape=jax.ShapeDtypeStruct((n_rows, half), jnp.uint32),
        compiler_params=_cparams("arbitrary", "arbitrary"),
        name="moe_ffn",
    )(tile_expert, n_valid, xs, wg, wu, wd, wd)


def _combine_kernel(dest_ref, y_ref, x_ref, info_ref, o_ref, buf_ref, sem, *, tt, half):
    i = pl.program_id(0)

    def start(t, c):
        tok = i * tt + t
        for k in range(TOP_K):
            _row_copy(y_ref, dest_ref[TOP_K * tok + k], buf_ref.at[k], t, sem).start(priority=k % 2)
        return c

    def wait(t, c):
        for k in range(TOP_K):
            _row_copy(y_ref, 0, buf_ref.at[k], 0, sem).wait()
        return c

    lax.fori_loop(0, tt, start, 0, unroll=8)
    lax.fori_loop(0, tt, wait, 0, unroll=8)
    info = info_ref[...]
    lo_sum = x_ref[:, :half]
    hi_sum = x_ref[:, half:]
    for k in range(TOP_K):
        wk = info[:, 2 + k:3 + k]
        lo, hi = _unpack_bf16_pairs(buf_ref[k])
        lo_sum = lo_sum + wk * lo
        hi_sum = hi_sum + wk * hi
    o_ref[:, :half] = lo_sum
    o_ref[:, half:] = hi_sum


def moe_combine_call(dest, y, x, info, tt=256):
    t, d = x.shape
    half = d // 2
    return pl.pallas_call(
        functools.partial(_combine_kernel, tt=tt, half=half),
        grid_spec=pltpu.PrefetchScalarGridSpec(
            num_scalar_prefetch=1,
            grid=(t // tt,),
            in_specs=[pl.BlockSpec(memory_space=pl.ANY),
                      pl.BlockSpec((tt, d), lambda i, dest_: (i, 0)),
                      pl.BlockSpec((tt, LANES), lambda i, dest_: (i, 0))],
            out_specs=pl.BlockSpec((tt, d), lambda i, dest_: (i, 0)),
            scratch_shapes=[pltpu.VMEM((TOP_K, tt, half), jnp.uint32), pltpu.SemaphoreType.DMA(())],
        ),
        out_shape=jax.ShapeDtypeStruct((t, d), F32),
        compiler_params=_cparams("arbitrary"),
        name="moe_combine",
    )(dest, y, x, info)


def moe_layer(xf, h, hp, router_w, router_b, wg, wu, wd, layer, tm=512):
    t, d = xf.shape
    ne = wg.shape[1]
    info, cnt = router_call(h, router_w, router_b)
    experts = info[:, 0:TOP_K].astype(jnp.int32)
    ranks = info[:, 4:4 + TOP_K].astype(jnp.int32)
    counts = cnt[0, :ne].astype(jnp.int32)
    padded = ((counts + tm - 1) // tm) * tm
    ends = jnp.cumsum(padded)
    dest = ((ends - padded)[experts] + ranks).reshape(-1)
    n_tiles = (t * TOP_K) // tm + ne
    tile_expert = jnp.minimum(jnp.sum(jnp.arange(n_tiles)[:, None] * tm >= ends[None, :], axis=-1), ne - 1).astype(jnp.int32)
    n_valid = (ends[-1:] // tm).astype(jnp.int32)
    slot_token = jnp.repeat(jnp.arange(t, dtype=jnp.int32), TOP_K)
    row_token = jnp.zeros((n_tiles * tm,), jnp.int32).at[dest].set(slot_token, unique_indices=True)
    xs = moe_dispatch_call(row_token, hp)
    y = moe_ffn_call(tile_expert, n_valid, xs, wg, wu, wd, layer, tm)
    return moe_combine_call(dest, y, xf, info)


def _pool_kernel(cur_ref, halo_ref, w_ref, scale_ref, o_ref, xs_ref, *, ts, halo):
    i = pl.program_id(1)
    prev = halo_ref[...].astype(F32)
    xs_ref[0:halo, :] = jnp.where(i == 0, 0.0, prev)
    xs_ref[halo:halo + ts, :] = cur_ref[...].astype(F32)
    row = lax.broadcasted_iota(jnp.int32, (ts, POOL_GW), 0) + i * ts + 1
    for gi, w in enumerate(POOL_WINDOWS):
        c0 = gi * POOL_GW
        x = xs_ref[halo:halo + ts, c0:c0 + POOL_GW]
        wsum = x
        for k in range(1, w):
            wsum = wsum + xs_ref[halo - k:halo - k + ts, c0:c0 + POOL_GW]
        cnt = jnp.minimum(row, w).astype(F32)
        diff = (wsum / cnt - x).astype(BF16)
        y = _dot(diff, w_ref[gi]) * scale_ref[:, c0:c0 + POOL_GW]
        o_ref[:, c0:c0 + POOL_GW] = y.astype(o_ref.dtype)


def pool_call(p, w_pool, scale, batch, ts=512):
    t = p.shape[0]
    seq = t // batch
    nts = seq // ts
    halo = 16
    cb = A0 // MIX_W
    return pl.pallas_call(
        functools.partial(_pool_kernel, ts=ts, halo=halo),
        grid=(batch, nts),
        in_specs=[
            pl.BlockSpec((ts, MIX_W), lambda b, i: (b * nts + i, cb)),
            pl.BlockSpec((halo, MIX_W), lambda b, i: (jnp.maximum((b * nts + i) * (ts // halo) - 1, 0), cb)),
            pl.BlockSpec((len(POOL_WINDOWS), POOL_GW, POOL_GW), lambda b, i: (0, 0, 0)),
            pl.BlockSpec((1, MIX_W), lambda b, i: (0, 0)),
        ],
        out_specs=pl.BlockSpec((ts, MIX_W), lambda b, i: (b * nts + i, 0)),
        out_shape=jax.ShapeDtypeStruct((t, MIX_W), BF16),
        scratch_shapes=[pltpu.VMEM((halo + ts, MIX_W), F32)],
        compiler_params=_cparams("parallel", "arbitrary"),
        name="pool",
    )(p, p, w_pool, scale)


def _layernorm(v, g, b):
    mu = jnp.mean(v, axis=-1, keepdims=True)
    vc = v - mu
    var = jnp.mean(vc * vc, axis=-1, keepdims=True)
    return (vc * lax.rsqrt(var + EPS)) * g + b


def _gmlp_kernel(z_ref, g_ref, b_ref, ws_ref, bs_ref, o_ref, *, ts):
    z = _gelu_tanh(z_ref[...].astype(F32))
    u = z[:, :MIX_W]
    v = _layernorm(z[:, MIX_W:], g_ref[...], b_ref[...]).astype(BF16)
    ri = lax.broadcasted_iota(jnp.int32, (GMLP_CHUNK, GMLP_CHUNK), 0)
    ci = lax.broadcasted_iota(jnp.int32, (GMLP_CHUNK, GMLP_CHUNK), 1)
    tri = ci <= ri
    for h in range(MIX_W // HEAD_DIM):
        w = jnp.where(tri, ws_ref[h], 0.0).astype(BF16)
        bias = bs_ref[:, h * HEAD_DIM:(h + 1) * HEAD_DIM]
        for c in range(ts // GMLP_CHUNK):
            rows = slice(c * GMLP_CHUNK, (c + 1) * GMLP_CHUNK)
            cols = slice(h * HEAD_DIM, (h + 1) * HEAD_DIM)
            s = _dot(w, v[rows, cols]) + bias
            o_ref[rows, cols] = (u[rows, cols] * s).astype(o_ref.dtype)


def gmlp_call(p, ln_g, ln_b, ws, bs_b, ts=512):
    t = p.shape[0]
    return pl.pallas_call(
        functools.partial(_gmlp_kernel, ts=ts),
        grid=(t // ts,),
        in_specs=[
            pl.BlockSpec((ts, 2 * MIX_W), lambda i: (i, C0 // (2 * MIX_W))),
            pl.BlockSpec((1, MIX_W), lambda i: (0, 0)),
            pl.BlockSpec((1, MIX_W), lambda i: (0, 0)),
            pl.BlockSpec(ws.shape, lambda i: (0, 0, 0)),
            pl.BlockSpec(bs_b.shape, lambda i: (0, 0)),
        ],
        out_specs=pl.BlockSpec((ts, MIX_W), lambda i: (i, 0)),
        out_shape=jax.ShapeDtypeStruct((t, MIX_W), BF16),
        compiler_params=_cparams("parallel"),
        name="gmlp",
    )(p, ln_g, ln_b, ws, bs_b)


def _conv_kernel(cur_ref, halo_ref, w_ref, b_ref, g_ref, beta_ref, o_ref, hs_ref, sh_ref, *, ts, halo):
    i = pl.program_id(1)

    def glu(z):
        zf = z.astype(F32)
        return zf[:, :MIX_W] * _sigmoid(zf[:, MIX_W:])

    hs_ref[0:halo, :] = jnp.where(i == 0, 0.0, glu(halo_ref[...]))
    hs_ref[halo:halo + ts, :] = glu(cur_ref[...])
    sub = 8
    shift_rows = sh_ref.shape[1]
    for s in range(1, sub):
        sh_ref[s - 1] = hs_ref[s:s + shift_rows, :]
    off = halo - (CONV_W - 1)
    acc = jnp.zeros((ts, MIX_W), F32) + b_ref[...]
    for k in range(CONV_W):
        s = (off + k) % sub
        base = off + k - s
        rows = hs_ref[base:base + ts, :] if s == 0 else sh_ref[s - 1, base:base + ts, :]
        acc = acc + w_ref[k:k + 1, :] * rows
    y = _layernorm(acc, g_ref[...], beta_ref[...])
    o_ref[...] = (y * _sigmoid(y)).astype(o_ref.dtype)


def conv_call(p, w, b, ln_g, ln_b, batch, ts=256):
    t = p.shape[0]
    seq = t // batch
    nts = seq // ts
    halo = 32
    cb = D0 // (2 * MIX_W)
    return pl.pallas_call(
        functools.partial(_conv_kernel, ts=ts, halo=halo),
        grid=(batch, nts),
        in_specs=[
            pl.BlockSpec((ts, 2 * MIX_W), lambda b_, i: (b_ * nts + i, cb)),
            pl.BlockSpec((halo, 2 * MIX_W), lambda b_, i: (jnp.maximum((b_ * nts + i) * (ts // halo) - 1, 0), cb)),
            pl.BlockSpec((CONV_W, MIX_W), lambda b_, i: (0, 0)),
            pl.BlockSpec((1, MIX_W), lambda b_, i: (0, 0)),
            pl.BlockSpec((1, MIX_W), lambda b_, i: (0, 0)),
            pl.BlockSpec((1, MIX_W), lambda b_, i: (0, 0)),
        ],
        out_specs=pl.BlockSpec((ts, MIX_W), lambda b_, i: (b_ * nts + i, 0)),
        out_shape=jax.ShapeDtypeStruct((t, MIX_W), BF16),
        scratch_shapes=[pltpu.VMEM((halo + ts, MIX_W), F32), pltpu.VMEM((7, halo + ts - 8, MIX_W), F32)],
        compiler_params=_cparams("parallel", "arbitrary"),
        name="conv",
    )(p, p, w, b, ln_g, ln_b)


def _head_rms(x, g):
    xf = x.astype(F32)
    r = lax.rsqrt(jnp.mean(xf * xf, axis=-1, keepdims=True) + EPS)
    return (xf * r) * g


def _nsa_prep_kernel(q_ref, ks_ref, vs_ref, kw_ref, vw_ref, qg_ref, kg_ref, qo_ref, kso_ref, kwo_ref, vso_ref, vwo_ref, *, tr):
    qg = qg_ref[...]
    kg = kg_ref[...]
    for h in range(NSA_HEADS):
        cols = slice(h * HEAD_DIM, (h + 1) * HEAD_DIM)
        qo_ref[:, cols] = (_head_rms(q_ref[:, cols], qg) * (HEAD_DIM ** -0.5)).astype(qo_ref.dtype)
    ones = jnp.ones((VT_ROWS - HEAD_DIM, Q_BLOCK), vso_ref.dtype)
    for g in range(NSA_KV):
        cols = slice(g * HEAD_DIM, (g + 1) * HEAD_DIM)
        kso_ref[:, cols] = _head_rms(ks_ref[:, cols], kg).astype(kso_ref.dtype)
        kwo_ref[:, cols] = _head_rms(kw_ref[:, cols], kg).astype(kwo_ref.dtype)
        for u in range(tr // Q_BLOCK):
            rows = slice(u * Q_BLOCK, (u + 1) * Q_BLOCK)
            for v_ref, vo_ref in ((vs_ref, vso_ref), (vw_ref, vwo_ref)):
                vo_ref[g, u, 0:HEAD_DIM, :] = v_ref[rows, cols].astype(F32).T.astype(vo_ref.dtype)
                vo_ref[g, u, HEAD_DIM:VT_ROWS, :] = ones


def nsa_prep_call(p, q_g, k_g, batch, tr=512):
    t = p.shape[0]
    seq = t // batch
    nst = seq // tr
    kvw = NSA_KV * HEAD_DIM
    kv_spec = lambda c: pl.BlockSpec((tr, kvw), lambda i: (i, (KV0 + c * kvw) // kvw))
    vt_sds = jax.ShapeDtypeStruct((batch * NSA_KV, seq // Q_BLOCK, VT_ROWS, Q_BLOCK), BF16)
    vt_spec = pl.BlockSpec((NSA_KV, tr // Q_BLOCK, VT_ROWS, Q_BLOCK), lambda i: (i // nst, i % nst, 0, 0))
    return pl.pallas_call(
        functools.partial(_nsa_prep_kernel, tr=tr),
        grid=(t // tr,),
        in_specs=[
            pl.BlockSpec((tr, MIX_W), lambda i: (i, Q0 // MIX_W)),
            kv_spec(2), kv_spec(3), kv_spec(4), kv_spec(5),
            pl.BlockSpec((1, HEAD_DIM), lambda i: (0, 0)),
            pl.BlockSpec((1, HEAD_DIM), lambda i: (0, 0)),
        ],
        out_specs=[
            pl.BlockSpec((tr, MIX_W), lambda i: (i, 0)),
            pl.BlockSpec((tr, kvw), lambda i: (i, 0)),
            pl.BlockSpec((tr, kvw), lambda i: (i, 0)),
            vt_spec, vt_spec,
        ],
        out_shape=[jax.ShapeDtypeStruct((t, MIX_W), BF16), jax.ShapeDtypeStruct((t, kvw), BF16),
                   jax.ShapeDtypeStruct((t, kvw), BF16), vt_sds, vt_sds],
        compiler_params=_cparams("parallel"),
        name="nsa_prep",
    )(p, p, p, p, p, q_g, k_g)


def _compress_kernel(ck_ref, cv_ref, pk_ref, pv_ref, w1k_ref, w2k_ref, w1v_ref, w2v_ref, kg_ref, ko_ref, vo_ref, *, nc):
    def comp(c_ref, p_ref, w1_ref, w2_ref):
        c = c_ref[...].astype(F32)
        xa = (c + p_ref[0:1, :]).astype(BF16)
        xb = (c + p_ref[1:2, :]).astype(BF16)
        first = _dot(xa, w1_ref[0])
        second = _dot(xb, w1_ref[1])
        hdn = _gelu_tanh(first + pltpu.roll(second, nc - 1, 0))
        return _dot(hdn.astype(BF16), w2_ref[...])

    row = lax.broadcasted_iota(jnp.int32, (nc, HEAD_DIM), 0)
    real = row < nc - 1
    kc = _head_rms(comp(ck_ref, pk_ref, w1k_ref, w2k_ref), kg_ref[...])
    vc = comp(cv_ref, pv_ref, w1v_ref, w2v_ref)
    zeros = jnp.zeros((KC_PAD, HEAD_DIM), F32)
    ko_ref[0:KC_PAD, :] = zeros
    vo_ref[0:KC_PAD, :] = zeros
    ko_ref[KC_PAD:KC_PAD + nc, :] = jnp.where(real, kc, 0.0)
    vo_ref[KC_PAD:KC_PAD + nc, :] = jnp.where(real, vc, 0.0)


def compress_call(ck, cv, pos_k, pos_v, w1k, w2k, w1v, w2v, k_g):
    bg, nc, half = ck.shape
    full2 = lambda shape: pl.BlockSpec(shape, lambda i: (0,) * len(shape))
    out_sds = jax.ShapeDtypeStruct((bg, KC_PAD + nc, HEAD_DIM), F32)
    return pl.pallas_call(
        functools.partial(_compress_kernel, nc=nc),
        grid=(bg,),
        in_specs=[
            pl.BlockSpec((None, nc, half), lambda i: (i, 0, 0)),
            pl.BlockSpec((None, nc, half), lambda i: (i, 0, 0)),
            full2((2, half)), full2((2, half)),
            full2((2, half, HEAD_DIM)), full2((HEAD_DIM, HEAD_DIM)),
            full2((2, half, HEAD_DIM)), full2((HEAD_DIM, HEAD_DIM)),
            full2((1, HEAD_DIM)),
        ],
        out_specs=[pl.BlockSpec((None, KC_PAD + nc, HEAD_DIM), lambda i: (i, 0, 0))] * 2,
        out_shape=[out_sds, out_sds],
        compiler_params=_cparams("parallel"),
        name="nsa_compress",
    )(ck, cv, pos_k, pos_v, w1k, w2k, w1v, w2v, k_g)


def _softmax_rows(parts):
    m = parts[0].max(axis=-1, keepdims=True)
    for x in parts[1:]:
        m = jnp.maximum(m, x.max(axis=-1, keepdims=True))
    m = jnp.maximum(m, -1e20)
    ps = [jnp.exp(x - m) for x in parts]
    s = ps[0].sum(axis=-1, keepdims=True)
    for p in ps[1:]:
        s = s + p.sum(axis=-1, keepdims=True)
    inv = 1.0 / jnp.maximum(s, 1e-30)
    return ps, inv


def _split_hi_lo(x):
    hi = x.astype(BF16)
    lo = (x - hi.astype(F32)).astype(BF16)
    return hi, lo


def _nsa_kernel(q_ref, kc_ref, vc_ref, ks_ref, vs_ref, kw_ref, vw_ref, ng_ref,
                bcn_ref, bs_ref, bw_ref, ov_ref, e_ref, o_ref, m_ref, acc_ref, ow_ref):
    i = pl.program_id(2)
    qb = Q_BLOCK
    rows = NSA_HPG * qb
    hd = HEAD_DIM
    qs = jnp.concatenate([q_ref[:, r * hd:(r + 1) * hd] for r in range(NSA_HPG)], axis=0)
    kcp = kc_ref.shape[0]

    nwin = WINDOW // qb + 1
    wtiles = [jnp.maximum(i - (nwin - 1 - tt), 0) for tt in range(nwin)]
    k_win = jnp.concatenate([kw_ref[pl.ds(pl.multiple_of(wt * qb, qb), qb), :] for wt in wtiles], axis=0)
    ltw = _dot_nt(k_win, qs) + bw_ref[...]
    wrow = lax.broadcasted_iota(jnp.int32, ltw.shape, 0)
    ltw = jnp.where(wrow >= (nwin - 1 - i) * qb, ltw, MASKED)
    mw = jnp.max(ltw, axis=0, keepdims=True)
    pw = _exp_bf16(ltw - mw)
    acc_w = _dot(jnp.concatenate([vw_ref[wt] for wt in wtiles], axis=1), pw)
    ow_ref[...] = acc_w[0:hd] * (1.0 / jnp.maximum(acc_w[hd:hd + 1], 1e-30))

    near0 = pl.multiple_of(8 * i + 8, 8)
    lf = _dot_nt(qs, kc_ref[...].astype(BF16))
    npad = lax.broadcasted_iota(jnp.int32, (rows, kcp), 1)
    lf = jnp.where((npad >= KC_PAD) & (npad < near0), lf, MASKED)
    ln = _dot_nt(qs, kc_ref[pl.ds(near0, qb), :].astype(BF16)) + bcn_ref[...]
    ncol = lax.broadcasted_iota(jnp.int32, (rows, qb), 1)
    ln = jnp.where(ncol >= KC_PAD - 8 - 8 * i, ln, MASKED)
    (pf, pn), inv = _softmax_rows([lf, ln])
    pf = pf * inv
    pn = pn * inv
    o_c = _dot(pf.astype(BF16), vc_ref[...].astype(BF16)) + _dot(pn.astype(BF16), vc_ref[pl.ds(near0, qb), :].astype(BF16))

    pf_g = pf[0:qb]
    pn_g = pn[0:qb]
    for r in range(1, NSA_HPG):
        pf_g = pf_g + pf[r * qb:(r + 1) * qb]
        pn_g = pn_g + pn[r * qb:(r + 1) * qb]
    ov_all = ov_ref[...].astype(BF16)
    ov_near = ov_ref[pl.ds(near0, qb), :].astype(BF16)
    imp = jnp.zeros((qb, ov_ref.shape[1]), F32)
    for part in _split_hi_lo(pf_g):
        imp = imp + _dot(part, ov_all)
    for part in _split_hi_lo(pn_g):
        imp = imp + _dot(part, ov_near)
    nblk = imp.shape[1]
    imp_t = imp.T
    jrow_i = lax.broadcasted_iota(jnp.int32, imp_t.shape, 0)
    cur = jnp.right_shift(lax.broadcasted_iota(jnp.int32, imp_t.shape, 1) + i * qb, 6)
    forced = (jrow_i == 0) | (jrow_i == cur) | (jrow_i == cur - 1)
    sc = jnp.where((jrow_i >= 1) & (jrow_i <= cur - 2), imp_t, -3e38)
    jrow = jrow_i.astype(F32)
    picked = jnp.zeros(sc.shape, F32)
    for _ in range(SLC_TOPK - 3):
        mx = jnp.max(sc, axis=0, keepdims=True)
        first = jnp.min(jnp.where(sc == mx, jrow, float(nblk)), axis=0, keepdims=True)
        pick = jrow == first
        picked = jnp.where(pick, 1.0, picked)
        sc = jnp.where(pick, -3e38, sc)
    sel_t = jnp.where(cur < SLC_TOPK, jnp.where(jrow_i <= cur, 1.0, 0.0), jnp.where(forced, 1.0, picked))
    sel = sel_t.T

    sel_neg = jnp.where(sel > 0.5, 0.0, MASKED).astype(BF16)
    q_aug = jnp.concatenate([qs, jnp.concatenate([sel_neg] * NSA_HPG, axis=0)], axis=1)

    def block_onehot(e_rows, valid):
        return jnp.where(valid, e_rows, jnp.ones_like(e_rows))

    prev_t = jnp.maximum(i - 1, 0)
    prev0 = pl.multiple_of(prev_t * qb, qb)
    diag0 = pl.multiple_of(i * qb, qb)
    k_near = jnp.concatenate([ks_ref[pl.ds(prev0, qb), :], ks_ref[pl.ds(diag0, qb), :]], axis=0)
    e_near = jnp.concatenate([e_ref[pl.ds(prev0, qb), :], e_ref[pl.ds(diag0, qb), :]], axis=0)
    row_near = lax.broadcasted_iota(jnp.int32, e_near.shape, 0)
    e_near = block_onehot(e_near, row_near >= jnp.where(i == 0, qb, 0))
    lt = _dot_nt(jnp.concatenate([k_near, e_near], axis=1), q_aug) + bs_ref[...]
    m0 = jnp.maximum(jnp.max(lt, axis=0, keepdims=True), -1e20).astype(BF16).astype(F32)
    p0 = _exp_bf16(lt - m0)
    vt_near = jnp.concatenate([vs_ref[prev_t], vs_ref[i]], axis=1)
    m_ref[...] = m0
    acc_ref[...] = _dot(vt_near, p0)

    far_end = (i - 1) * qb
    tiles = 4
    chunk = tiles * qb
    halves = 2

    def far_body(c, carry):
        lts, vts = [], []
        for hf in range(halves):
            t0 = (c * halves + hf) * tiles
            c0 = pl.multiple_of(t0 * qb, chunk)
            e_c = e_ref[pl.ds(c0, chunk), :]
            row_c = lax.broadcasted_iota(jnp.int32, e_c.shape, 0) + c0
            k_aug = jnp.concatenate([ks_ref[pl.ds(c0, chunk), :], block_onehot(e_c, row_c < far_end)], axis=1)
            lts.append(_dot_nt(k_aug, q_aug).astype(BF16))
            vts.append(jnp.concatenate([vs_ref[t0 + u] for u in range(tiles)], axis=1))
        m_old = m_ref[...]
        m_new = m_old
        for lt_c in lts:
            m_new = jnp.maximum(m_new, jnp.max(lt_c, axis=0, keepdims=True).astype(F32))
        acc = jnp.exp(m_old - m_new) * acc_ref[...]
        m_new_b = m_new.astype(BF16)
        for lt_c, vt_c in zip(lts, vts):
            acc = acc + _dot(vt_c, jnp.exp(lt_c - m_new_b))
        m_ref[...] = m_new
        acc_ref[...] = acc
        return carry

    lax.fori_loop(0, (i - 1 + halves * tiles - 1) // (halves * tiles), far_body, 0)
    acc_s = acc_ref[...]
    o_s_t = acc_s[0:hd] * (1.0 / jnp.maximum(acc_s[hd:hd + 1], 1e-30))
    o_w_t = ow_ref[...]

    gate = _sigmoid(ng_ref[...].astype(F32))
    gate_t = gate.T
    for r in range(NSA_HPG):
        rs = slice(r * qb, (r + 1) * qb)
        o_sw_t = gate_t[3 * r + 1:3 * r + 2, :] * o_s_t[:, rs] + gate_t[3 * r + 2:3 * r + 3, :] * o_w_t[:, rs]
        o = gate[:, 3 * r:3 * r + 1] * o_c[rs] + o_sw_t.T
        o_ref[:, r * hd:(r + 1) * hd] = o.astype(o_ref.dtype)


def nsa_call(p, qn, ksn, kwn, vst, vwt, kcp, vcp, bcn, bsl_t, bwn_t, ovl, e_tab, batch):
    t = p.shape[0]
    seq = t // batch
    nq = seq // Q_BLOCK
    assert nq % 8 == 0
    g_ = NSA_KV
    kcp_rows = kcp.shape[1]
    rows = NSA_HPG * Q_BLOCK
    ng_blk = NG0 // LANES
    seq_spec = pl.BlockSpec((seq, HEAD_DIM), lambda b, g, i: (b, g))
    vt_spec = pl.BlockSpec((None, nq, VT_ROWS, Q_BLOCK), lambda b, g, i: (b * g_ + g, 0, 0, 0))
    cmp_spec = pl.BlockSpec((None, kcp_rows, HEAD_DIM), lambda b, g, i: (b * g_ + g, 0, 0))
    tbl_spec = lambda arr: pl.BlockSpec((None,) + arr.shape[1:], lambda b, g, i: (g, 0, 0))
    return pl.pallas_call(
        _nsa_kernel,
        grid=(batch, g_, nq),
        in_specs=[
            pl.BlockSpec((Q_BLOCK, rows), lambda b, g, i: (b * nq + i, g)),
            cmp_spec, cmp_spec,
            seq_spec, vt_spec, seq_spec, vt_spec,
            pl.BlockSpec((Q_BLOCK, LANES), lambda b, g, i: (b * nq + i, ng_blk + g)),
            tbl_spec(bcn), tbl_spec(bsl_t), tbl_spec(bwn_t),
            pl.BlockSpec(ovl.shape, lambda b, g, i: (0, 0)),
            pl.BlockSpec(e_tab.shape, lambda b, g, i: (0, 0)),
        ],
        out_specs=pl.BlockSpec((Q_BLOCK, rows), lambda b, g, i: (b * nq + i, g)),
        out_shape=jax.ShapeDtypeStruct((t, MIX_W), BF16),
        scratch_shapes=[pltpu.VMEM((1, rows), F32), pltpu.VMEM((VT_ROWS, rows), F32), pltpu.VMEM((HEAD_DIM, rows), F32)],
        compiler_params=_cparams("parallel", "parallel", "arbitrary"),
        name="nsa_attention",
    )(qn, kcp, vcp, ksn, vst, kwn, vwt, p, bcn, bsl_t, bwn_t, ovl, e_tab)


def _t5_bucket_np(dist):
    n = np.maximum(dist, 0)
    max_exact = REL_BUCKETS // 2
    nf = np.maximum(n, 1).astype(np.float32)
    large = max_exact + (np.log(nf / max_exact) / math.log(REL_MAX_DIST / max_exact)
                         * (REL_BUCKETS - max_exact)).astype(np.int32)
    large = np.minimum(large, REL_BUCKETS - 1)
    return np.where(n < max_exact, n, large)


def _bias_table_kernel(rb_ref, idx_ref, o_ref):
    h = pl.program_id(0)
    idx = idx_ref[...]
    far = rb_ref[REL_BUCKETS - 1, h]
    acc = jnp.zeros(idx.shape, F32)
    for b in range(REL_BUCKETS):
        acc = jnp.where(idx == b, rb_ref[b, h] - far, acc)
    o_ref[...] = jnp.where(idx < 0, MASKED, acc)


def bias_table_call(rel_bias, bucket):
    nh = rel_bias.shape[1]
    return pl.pallas_call(
        _bias_table_kernel,
        grid=(nh,),
        in_specs=[pl.BlockSpec(memory_space=pltpu.SMEM), pl.BlockSpec(bucket.shape, lambda h: (0, 0))],
        out_specs=pl.BlockSpec((None,) + bucket.shape, lambda h: (h, 0, 0)),
        out_shape=jax.ShapeDtypeStruct((nh,) + bucket.shape, F32),
        compiler_params=_cparams("arbitrary"),
        name="bias_tables",
    )(rel_bias, bucket)


def _bias_tables(rel_bias, seq):
    qb = Q_BLOCK
    iq = np.arange(qb)[:, None]
    jk = np.arange(qb)[None, :]
    d_diag = iq - jk
    d_prev = qb + iq - jk
    d_edge = WINDOW + iq - jk
    d_far = np.full((qb, qb), 2 * qb)
    d_sel = np.concatenate([d_prev, d_diag], axis=1)
    d_win = np.concatenate([d_edge] + [d_far] * (WINDOW // qb - 2) + [d_prev, d_diag], axis=1)
    d_cmp = iq - CMP_STRIDE * (jk - (KC_PAD - 8)) - (CMP_LEN - 1)
    parts = [(d_cmp, d_cmp >= 0), (d_sel, d_sel >= 0), (d_win, (d_win >= 0) & (d_win < WINDOW))]
    bucket = np.concatenate([np.where(vis, _t5_bucket_np(dist), -1) for dist, vis in parts], axis=1).astype(np.int32)
    tables = bias_table_call(rel_bias, jnp.asarray(bucket))
    tables = tables.reshape(NSA_KV, NSA_HPG * qb, bucket.shape[1])
    w_cmp, w_sel = d_cmp.shape[1], d_sel.shape[1]
    bcn = tables[:, :, :w_cmp]
    bsl_t = tables[:, :, w_cmp:w_cmp + w_sel].transpose(0, 2, 1)
    bwn_t = tables[:, :, w_cmp + w_sel:].transpose(0, 2, 1)
    nc = seq // CMP_STRIDE
    n_slc = seq // SLC_LEN
    c0 = (np.arange(nc) * CMP_STRIDE)[:, None]
    s0 = (np.arange(n_slc) * SLC_LEN)[None, :]
    ov = ((c0 < s0 + SLC_LEN) & (c0 + CMP_LEN > s0) & (np.arange(nc)[:, None] < nc - 1)).astype(np.float32)
    ovl = np.concatenate([np.zeros((KC_PAD, n_slc), np.float32), ov], axis=0)
    e_tab = (np.arange(seq)[:, None] // SLC_LEN == np.arange(n_slc)[None, :]).astype(np.float32)
    return bcn, bsl_t, bwn_t, jnp.asarray(ovl), jnp.asarray(e_tab, dtype=BF16)


def _merge_kernel(g_ref, a_ref, b_ref, c_ref, d_ref, wgu_ref, wbr_ref, o_ref):
    gin = g_ref[...]
    acc = None
    for bi, br in enumerate((a_ref, b_ref, c_ref, d_ref)):
        gate = _sigmoid(_dot(gin, wgu_ref[bi]))
        term = gate * _dot(br[...], wbr_ref[bi])
        acc = term if acc is None else acc + term
    o_ref[...] = acc.astype(o_ref.dtype)


def merge_call(p, branches, wgu, wbr, layer, tm=1024, tn=512):
    t = p.shape[0]
    d = wgu.shape[-1]
    row = lambda w, cb: pl.BlockSpec((tm, w), lambda i, j: (i, cb))
    return pl.pallas_call(
        _merge_kernel,
        grid=(t // tm, d // tn),
        in_specs=[row(GATE_RANK, G0 // GATE_RANK)] + [row(MIX_W, 0)] * 4 + [
            pl.BlockSpec((None, 4, GATE_RANK, tn), lambda i, j: (layer, 0, 0, j)),
            pl.BlockSpec((None, 4, MIX_W, tn), lambda i, j: (layer, 0, 0, j)),
        ],
        out_specs=pl.BlockSpec((tm, tn), lambda i, j: (i, j)),
        out_shape=jax.ShapeDtypeStruct((t, d), BF16),
        compiler_params=_cparams("parallel", "arbitrary"),
        name="merge",
    )(p, *branches, wgu, wbr)


def _rearranged_w_in(w):
    o_a, o_q, o_kv, o_ng, o_c, o_d, o_g = 0, 1024, 2048, 3584, 3608, 5656, 7704
    d = w.shape[0]
    hpg3 = NSA_HPG * 3
    zpad = lambda n: jnp.zeros((d, n), BF16)
    parts = [
        w[:, o_c:o_c + 2048], w[:, o_d:o_d + 2048], w[:, o_a:o_a + 1024],
        w[:, o_q:o_q + 1024], w[:, o_kv:o_kv + 1536], w[:, o_g:o_g + 512],
        w[:, o_ng:o_ng + hpg3], zpad(LANES - hpg3),
        w[:, o_ng + hpg3:o_ng + 2 * hpg3], zpad(LANES - hpg3), zpad(PCOLS - NG0 - 2 * LANES),
    ]
    return jnp.concatenate([part.astype(BF16) for part in parts], axis=-1)


def kernel(x, rel_bias, norm_mix_g, norm_ffn_g, w_in, pool_w, pool_scale, q_norm_g, k_norm_g, cmp_pos_k, cmp_w1_k, cmp_w2_k, cmp_pos_v, cmp_w1_v, cmp_w2_v, gmlp_ln_g, gmlp_ln_b, gmlp_ws, gmlp_bs, conv_w, conv_b, conv_ln_g, conv_ln_b, w_branch, w_gate_up, w_out, ffn_w_gate, ffn_w_up, ffn_w_down, moe_router, moe_router_b, moe_w_gate, moe_w_up, moe_w_down):
    batch, seq, d = x.shape
    t = batch * seq
    depth = w_in.shape[0]
    nc = seq // CMP_STRIDE
    half = CMP_STRIDE * HEAD_DIM
    xf = x.reshape(t, d)

    bcn, bsl_t, bwn_t, ovl, e_tab = _bias_tables(rel_bias, seq)
    row = lambda v: v.reshape(1, -1)
    w_gate_up_b, w_branch_b, w_out_b = (w.astype(BF16) for w in (w_gate_up, w_branch, w_out))
    ffn_w_gate_b, ffn_w_up_b, ffn_w_down_b = (w.astype(BF16) for w in (ffn_w_gate, ffn_w_up, ffn_w_down))
    moe_w_gate_b, moe_w_up_b, moe_w_down_b = (w.astype(BF16) for w in (moe_w_gate, moe_w_up, moe_w_down))

    for l in range(depth):
        h = rmsnorm_call(xf, row(norm_mix_g[l]))
        p = matmul_call(h, _rearranged_w_in(w_in[l]), name="in_proj")
        o_a = pool_call(p, pool_w[l].astype(BF16), row(pool_scale[l]), batch)
        qn, ksn, kwn, vst, vwt = nsa_prep_call(p, row(q_norm_g[l]), row(k_norm_g[l]), batch)
        kvc = p[:, KV0:KV0 + 2 * NSA_KV * HEAD_DIM].reshape(batch, nc, CMP_STRIDE, 2, NSA_KV, HEAD_DIM)
        kvc = kvc.transpose(3, 0, 4, 1, 2, 5).reshape(2, batch * NSA_KV, nc, half)
        kcp, vcp = compress_call(
            kvc[0], kvc[1], cmp_pos_k[l].reshape(2, half), cmp_pos_v[l].reshape(2, half),
            cmp_w1_k[l].reshape(2, half, HEAD_DIM).astype(BF16), cmp_w2_k[l].astype(BF16),
            cmp_w1_v[l].reshape(2, half, HEAD_DIM).astype(BF16), cmp_w2_v[l].astype(BF16), row(k_norm_g[l]))
        o_b = nsa_call(p, qn, ksn, kwn, vst, vwt, kcp, vcp, bcn, bsl_t, bwn_t, ovl, e_tab, batch)
        bs_b = jnp.repeat(gmlp_bs[l].T, HEAD_DIM, axis=1)
        o_c = gmlp_call(p, row(gmlp_ln_g[l]), row(gmlp_ln_b[l]), gmlp_ws[l], bs_b)
        o_d = conv_call(p, conv_w[l], row(conv_b[l]), row(conv_ln_g[l]), row(conv_ln_b[l]), batch)
        mix = merge_call(p, (o_a, o_b, o_c, o_d), w_gate_up_b, w_branch_b, l)
        xf = matmul_call(mix, w_out_b, layer=l, res=xf, out_dtype=F32, name="out_proj")
        i = l // 2
        if l % 2 == 0:
            h = rmsnorm_call(xf, row(norm_ffn_g[l]))
            act = swiglu_call(h, ffn_w_gate_b, ffn_w_up_b, i)
            xf = matmul_call(act, ffn_w_down_b, layer=i, res=xf, out_dtype=F32, name="ffn_down")
        else:
            h, hp = rmsnorm_call(xf, row(norm_ffn_g[l]), packed=True)
            rw = jnp.pad(moe_router[i], ((0, 0), (0, LANES - N_EXPERTS))).astype(BF16)
            rb = jnp.pad(moe_router_b[i], (0, LANES - N_EXPERTS)).reshape(1, LANES)
            xf = moe_layer(xf, h, hp, rw, rb, moe_w_gate_b, moe_w_up_b, moe_w_down_b, i)
    return xf.reshape(batch, seq, d)
```
